```python
import jax, jax.numpy as jnp
from jax import lax
import numpy as np

D_MODEL = 1024
BATCH = 32
SEQ = 2048
DEPTH = 1
DEC_BATCH = 16
DEC_SEQ = 64
PAST_LEN = 4096

CHUNK = 64
D_MIX = D_MODEL
GLA_HEADS = 4
GLA_WIDTH = D_MIX // 2
GLA_DV = GLA_WIDTH // GLA_HEADS
GLA_DK = GLA_DV // 2
GLA_KW = GLA_HEADS * GLA_DK
GATE_RANK = 16
GATE_TAU = 16.0
POOL_WIDTH = D_MIX - GLA_WIDTH
POOL_GROUPS = 4
POOL_GC = POOL_WIDTH // POOL_GROUPS
POOL_WINDOWS = (2, 4, 8, 16)
POOL_BUF = max(POOL_WINDOWS) - 1
SPLITS = (GLA_KW, 2 * GLA_KW, 2 * GLA_KW + GLA_WIDTH, 2 * GLA_KW + 2 * GLA_WIDTH,
          2 * GLA_KW + 2 * GLA_WIDTH + GATE_RANK)
D_IN = 2 * GLA_KW + 2 * GLA_WIDTH + GATE_RANK + POOL_WIDTH
N_MEM = 256
MEM_HEADS = 4
MEM_HD = D_MODEL // MEM_HEADS
N_EXPERTS = 32
TOP_K = 4
D_FF = D_MODEL
SWIGLU_ALPHA = 1.702
SWIGLU_LIMIT = 7.0
MOE_BLOCK = 128
EPS = 1e-6

kernel_name = "hymba_gla_pool_memxattn_moe_stream_step"


def rmsnorm(x, g):
    xf = x.astype(jnp.float32)
    y = xf * lax.rsqrt(jnp.mean(xf * xf, axis=-1, keepdims=True) + EPS)
    return (y * g.astype(jnp.float32)).astype(x.dtype)


def gla_chunked(q, k, v, log_a, s0):
    B, L, H, DK = q.shape
    DV = v.shape[-1]
    C = min(CHUNK, L)
    N = L // C
    f32 = jnp.float32
    q = q.astype(f32).reshape(B, N, C, H, DK)
    k = k.astype(f32).reshape(B, N, C, H, DK)
    v = v.astype(f32).reshape(B, N, C, H, DV)
    b = jnp.cumsum(log_a.astype(f32).reshape(B, N, C, H, DK), axis=2)
    b_last = b[:, :, -1]
    qd = q * jnp.exp(b)
    kd = k * jnp.exp(-b)
    kl = k * jnp.exp(b_last[:, :, None] - b)
    mask = jnp.tril(jnp.ones((C, C), dtype=bool))
    a = jnp.einsum('bnihd,bnjhd->bnhij', qd, kd)
    a = jnp.where(mask, a, 0.0)
    o_intra = jnp.einsum('bnhij,bnjhe->bnihe', a, v)
    ds = jnp.einsum('bnchd,bnche->nbhde', kl, v)
    dec = jnp.exp(b_last).transpose(1, 0, 2, 3)

    def step(s, inp):
        d, dsn = inp
        return d[..., None] * s + dsn, s

    s_fin, s_in = lax.scan(step, s0.astype(f32), (dec, ds))
    o_inter = jnp.einsum('bnchd,nbhde->bnche', qd, s_in)
    o = (o_intra + o_inter).reshape(B, L, H, DV)
    return o, s_fin


def pool_mix(u, prev, hist_len, pool_w, pool_scale):
    B, L, _ = u.shape
    f32 = jnp.float32
    full = jnp.concatenate([prev.astype(u.dtype), u], axis=1)
    valid = jnp.concatenate([(jnp.arange(POOL_BUF) >= POOL_BUF - hist_len).astype(f32),
                             jnp.ones((L,), f32)])
    fullf = full.astype(f32) * valid[None, :, None]
    cs = jnp.concatenate([jnp.zeros((B, 1, POOL_WIDTH), f32), jnp.cumsum(fullf, axis=1)], axis=1)
    cv = jnp.concatenate([jnp.zeros((1,), f32), jnp.cumsum(valid)])
    uf = u.astype(f32)
    e = POOL_BUF + 1
    diffs = []
    for g, w in enumerate(POOL_WINDOWS):
        sl = slice(g * POOL_GC, (g + 1) * POOL_GC)
        s = cs[:, e:e + L, sl] - cs[:, e - w:e - w + L, sl]
        n = cv[e:e + L] - cv[e - w:e - w + L]
        diffs.append(s / n[None, :, None] - uf[..., sl])
    d = jnp.stack(diffs, axis=2)
    y = jnp.einsum('blgc,gcd->blgd', d, pool_w.astype(f32)).reshape(B, L, POOL_WIDTH)
    y = y * pool_scale.astype(f32)
    return y.astype(u.dtype), full[:, -POOL_BUF:]


def token_mix(h, s_gla, pool_prev, hist_len, w_in, w_gate, b_gate, gla_gain, pool_w, pool_scale, w_out):
    B, L, _ = h.shape
    z = h @ w_in
    q, k, v, g, r, u = jnp.split(z, SPLITS, axis=-1)
    q = q.reshape(B, L, GLA_HEADS, GLA_DK) * (GLA_DK ** -0.5)
    k = k.reshape(B, L, GLA_HEADS, GLA_DK)
    v = v.reshape(B, L, GLA_HEADS, GLA_DV)
    log_a = jax.nn.log_sigmoid((r @ w_gate + b_gate).astype(jnp.float32)) / GATE_TAU
    log_a = log_a.reshape(B, L, GLA_HEADS, GLA_DK)
    o, s_new = gla_chunked(q, k, v, log_a, s_gla)
    o = rmsnorm(o, gla_gain.reshape(GLA_HEADS, GLA_DV))
    o = o.reshape(B, L, GLA_WIDTH).astype(h.dtype) * jax.nn.silu(g)
    p, pool_new = pool_mix(u, pool_prev, hist_len, pool_w, pool_scale)
    y = jnp.concatenate([o, p], axis=-1) @ w_out
    return y, s_new.astype(h.dtype), pool_new


def mem_kv(mem, norm_mem_kv, w_mk, w_mv):
    B = mem.shape[0]
    m = rmsnorm(mem, norm_mem_kv)
    k = (m @ w_mk).reshape(B, N_MEM, MEM_HEADS, MEM_HD)
    v = (m @ w_mv).reshape(B, N_MEM, MEM_HEADS, MEM_HD)
    return k, v


def cross_attn(h, k, v, w_mq, w_mo):
    B, L, D = h.shape
    q = (h @ w_mq).reshape(B, L, MEM_HEADS, MEM_HD)
    s = jnp.einsum('blhd,bmhd->bhlm', q, k).astype(jnp.float32) * (MEM_HD ** -0.5)
    p = jax.nn.softmax(s, axis=-1).astype(v.dtype)
    o = jnp.einsum('bhlm,bmhd->blhd', p, v).reshape(B, L, D)
    return o @ w_mo


def moe(h, w_router, b_router, w_up, b_up, w_down, b_down):
    B, L, D = h.shape
    t = h.reshape(-1, D)
    T = t.shape[0]
    logits = (t @ w_router + b_router).astype(jnp.float32)
    top_v, top_i = lax.top_k(logits, TOP_K)
    gates = jax.nn.softmax(top_v, axis=-1)
    S = T * TOP_K
    e = top_i.reshape(-1).astype(jnp.int32)
    tok = jnp.repeat(jnp.arange(T, dtype=jnp.int32), TOP_K)
    wts = gates.reshape(-1)
    order = jnp.argsort(e)
    se, stok, sw = e[order], tok[order], wts[order]
    counts = jnp.bincount(e, length=N_EXPERTS)
    padded = ((counts + MOE_BLOCK - 1) // MOE_BLOCK) * MOE_BLOCK
    start = jnp.cumsum(counts) - counts
    pend = jnp.cumsum(padded)
    pstart = pend - padded
    dest = pstart[se] + jnp.arange(S, dtype=jnp.int32) - start[se]
    nb = (S + N_EXPERTS * (MOE_BLOCK - 1) + MOE_BLOCK - 1) // MOE_BLOCK
    P = nb * MOE_BLOCK
    tok_buf = jnp.zeros((P,), jnp.int32).at[dest].set(stok)
    w_buf = jnp.zeros((P,), jnp.float32).at[dest].set(sw)
    block_e = jnp.minimum(jnp.searchsorted(pend, jnp.arange(nb) * MOE_BLOCK, side='right'),
                          N_EXPERTS - 1).astype(jnp.int32)

    def expert_block(args):
        eid, tk, wt = args
        xb = t[tk]
        uu = xb @ w_up[eid] + b_up[eid]
        glu = jnp.minimum(uu[:, :D_FF], SWIGLU_LIMIT)
        lin = jnp.clip(uu[:, D_FF:], -SWIGLU_LIMIT, SWIGLU_LIMIT)
        act = glu * jax.nn.sigmoid(SWIGLU_ALPHA * glu) * (lin + 1.0)
        yb = act @ w_down[eid] + b_down[eid]
        return yb * wt[:, None].astype(yb.dtype)

    ys = lax.map(expert_block, (block_e, tok_buf.reshape(nb, MOE_BLOCK), w_buf.reshape(nb, MOE_BLOCK)))
    out = jnp.zeros((T, D), ys.dtype).at[tok_buf].add(ys.reshape(P, D))
    return out.reshape(B, L, D).astype(h.dtype)


def block(x, s_gla, pool_prev, hist_len, mk, mv,
          norm_mix, w_in, w_gate, b_gate, gla_gain, pool_w, pool_scale, w_out,
          norm_mem_q, w_mq, w_mo, norm_moe, w_router, b_router, w_up, b_up, w_down, b_down):
    y, s_new, pool_new = token_mix(rmsnorm(x, norm_mix), s_gla, pool_prev, hist_len,
                                   w_in, w_gate, b_gate, gla_gain, pool_w, pool_scale, w_out)
    x = x + y
    x = x + cross_attn(rmsnorm(x, norm_mem_q), mk, mv, w_mq, w_mo)
    x = x + moe(rmsnorm(x, norm_moe), w_router, b_router, w_up, b_up, w_down, b_down)
    return x, s_new, pool_new


def setup_inputs(seed: int = 0) -> dict:
    key = jax.random.key(seed)
    ks = iter(jax.random.split(key, 40))
    f32 = jnp.float32

    def nrm(shape, s):
        return jax.random.normal(next(ks), shape, f32) * s

    def gain(n):
        return 1.0 + 0.05 * jax.random.normal(next(ks), (DEPTH, n), f32)

    return {
        "x_prompt": nrm((BATCH, SEQ, D_MODEL), 1.0),
        "x_sample": nrm((DEC_BATCH, DEC_SEQ, D_MODEL), 1.0),
        "mem_prompt": nrm((BATCH, N_MEM, D_MODEL), 1.0),
        "state_gla": nrm((DEPTH, DEC_BATCH, GLA_HEADS, GLA_DK, GLA_DV), 0.5),
        "state_pool": nrm((DEPTH, DEC_BATCH, POOL_BUF, POOL_WIDTH), 1.0),
        "cache_mem_k": nrm((DEPTH, DEC_BATCH, N_MEM, MEM_HEADS, MEM_HD), 1.0),
        "cache_mem_v": nrm((DEPTH, DEC_BATCH, N_MEM, MEM_HEADS, MEM_HD), 1.0),
        "norm_mix": gain(D_MODEL),
        "w_in": nrm((DEPTH, D_MODEL, D_IN), D_MODEL ** -0.5),
        "w_gate": nrm((DEPTH, GATE_RANK, GLA_KW), GATE_RANK ** -0.5),
        "b_gate": nrm((DEPTH, GLA_KW), 0.1),
        "gla_gain": gain(GLA_WIDTH),
        "pool_w": nrm((DEPTH, POOL_GROUPS, POOL_GC, POOL_GC), POOL_GC ** -0.5),
        "pool_scale": gain(POOL_WIDTH),
        "w_out": nrm((DEPTH, D_MIX, D_MODEL), D_MIX ** -0.5),
        "norm_mem_q": gain(D_MODEL),
        "norm_mem_kv": gain(D_MODEL),
        "w_mq": nrm((DEPTH, D_MODEL, D_MODEL), D_MODEL ** -0.5),
        "w_mk": nrm((DEPTH, D_MODEL, D_MODEL), D_MODEL ** -0.5),
        "w_mv": nrm((DEPTH, D_MODEL, D_MODEL), D_MODEL ** -0.5),
        "w_mo": nrm((DEPTH, D_MODEL, D_MODEL), D_MODEL ** -0.5),
        "norm_moe": gain(D_MODEL),
        "w_router": nrm((DEPTH, D_MODEL, N_EXPERTS), D_MODEL ** -0.5),
        "b_router": nrm((DEPTH, N_EXPERTS), 0.01),
        "w_up": nrm((DEPTH, N_EXPERTS, D_MODEL, 2 * D_FF), D_MODEL ** -0.5),
        "b_up": nrm((DEPTH, N_EXPERTS, 2 * D_FF), 0.01),
        "w_down": nrm((DEPTH, N_EXPERTS, D_FF, D_MODEL), D_FF ** -0.5),
        "b_down": nrm((DEPTH, N_EXPERTS, D_MODEL), 0.01),
        "norm_final": gain(D_MODEL)[0],
    }


def reference(x_prompt, x_sample, mem_prompt, state_gla, state_pool, cache_mem_k, cache_mem_v,
              norm_mix, w_in, w_gate, b_gate, gla_gain, pool_w, pool_scale, w_out,
              norm_mem_q, norm_mem_kv, w_mq, w_mk, w_mv, w_mo,
              norm_moe, w_router, b_router, w_up, b_up, w_down, b_down, norm_final):
    xp, xs = x_prompt, x_sample
    Bp = xp.shape[0]
    gla_p, pool_p, mk_p, mv_p, gla_s, pool_s = [], [], [], [], [], []
    for l in range(DEPTH):
        lw = (norm_mix[l], w_in[l], w_gate[l], b_gate[l], gla_gain[l], pool_w[l], pool_scale[l], w_out[l],
              norm_mem_q[l], w_mq[l], w_mo[l], norm_moe[l], w_router[l], b_router[l],
              w_up[l], b_up[l], w_down[l], b_down[l])
        mk, mv = mem_kv(mem_prompt, norm_mem_kv[l], w_mk[l], w_mv[l])
        s0 = jnp.zeros((Bp, GLA_HEADS, GLA_DK, GLA_DV), xp.dtype)
        p0 = jnp.zeros((Bp, POOL_BUF, POOL_WIDTH), xp.dtype)
        xp, sp, pp = block(xp, s0, p0, 0, mk, mv, *lw)
        gla_p.append(sp)
        pool_p.append(pp)
        mk_p.append(mk)
        mv_p.append(mv)
        xs, ss, ps = block(xs, state_gla[l], state_pool[l], PAST_LEN, cache_mem_k[l], cache_mem_v[l], *lw)
        gla_s.append(ss)
        pool_s.append(ps)
    y_prompt = rmsnorm(xp, norm_final)
    y_sample = rmsnorm(xs, norm_final)
    return (y_prompt, y_sample, jnp.stack(gla_p), jnp.stack(pool_p), jnp.stack(mk_p), jnp.stack(mv_p),
            jnp.stack(gla_s), jnp.stack(pool_s))
```

```python
import functools

import numpy as np
import jax
import jax.numpy as jnp
from jax import lax
from jax.experimental import pallas as pl
from jax.experimental.pallas import tpu as pltpu

F32 = jnp.float32
BF16 = jnp.bfloat16
I32 = jnp.int32

D_MODEL = 1024
CHUNK = 64
GLA_HEADS = 4
GLA_DK = 64
GLA_DV = 128
GLA_KW = GLA_HEADS * GLA_DK
GLA_WIDTH = GLA_HEADS * GLA_DV
GATE_RANK = 16
GATE_TAU = 16.0
POOL_WIDTH = 512
POOL_GC = 128
POOL_WINDOWS = (2, 4, 8, 16)
POOL_BUF = 15
PAST_LEN = 4096
POOL_ROWS = 16
N_MEM = 256
MEM_HEADS = 4
MEM_HD = 256
N_EXPERTS = 32
TOP_K = 4
D_FF = 1024
SWIGLU_ALPHA = 1.702
SWIGLU_LIMIT = 7.0
EPS = 1e-6
LANES = 128
NEG_BIG = -1e30

PREMOE_ROWS = 256
MOE_ROWS = 512
FINAL_ROWS = 256
VMEM_LIMIT = 56 * 1024 * 1024

NT_DIMS = (((1,), (1,)), ((), ()))
TN_DIMS = (((0,), (0,)), ((), ()))


def _rms(x, g):
    ms = jnp.mean(x * x, axis=-1, keepdims=True)
    return x * lax.rsqrt(ms + EPS) * g


def _const_spec(shape):
    nd = len(shape)
    return pl.BlockSpec(shape, lambda *_: (0,) * nd, pipeline_mode=pl.Buffered(1))


def _memkv_body(m_ref, g_ref, wk_ref, wv_ref, k_ref, v_ref):
    m = _rms(m_ref[...], g_ref[...]).astype(BF16)
    k_ref[...] = jnp.dot(m, wk_ref[...], preferred_element_type=F32)
    v_ref[...] = jnp.dot(m, wv_ref[...], preferred_element_type=F32)


def _mem_kv(mem2d, g, wk, wv):
    n = mem2d.shape[0]
    tm = 512
    row = pl.BlockSpec((tm, D_MODEL), lambda i: (i, 0))
    return pl.pallas_call(
        _memkv_body,
        grid=(n // tm,),
        in_specs=[row, _const_spec((1, D_MODEL)), _const_spec((D_MODEL, D_MODEL)), _const_spec((D_MODEL, D_MODEL))],
        out_specs=[row, row],
        out_shape=[jax.ShapeDtypeStruct((n, D_MODEL), F32)] * 2,
        compiler_params=pltpu.CompilerParams(dimension_semantics=("arbitrary",), vmem_limit_bytes=VMEM_LIMIT),
        name="mem_kv",
    )(mem2d, g, wk, wv)


def _premoe_body(hist, rows,
                 x_ref, s0_ref, p0_ref, k_ref, v_ref,
                 nmix_ref, wqkvg_ref, wr_ref, wu_ref, wgate_ref, bgate_ref, ggain_ref, poolw_ref, pscale_ref,
                 wout_ref, nq_ref, wmq_ref, wmo_ref, nmoe_ref, wrt_ref, brt_ref,
                 tri2_ref, ltri_ref, band_ref,
                 x2_ref, ri_ref, rg_ref, cnt_ref, sto_ref, po_ref,
                 st_sc, prev_sc, cnt_sc):
    b = pl.program_id(0)
    l = pl.program_id(1)
    n_chunks = rows // CHUNK

    @pl.when(l == 0)
    def _():
        st_sc[...] = s0_ref[0]
        prow = lax.broadcasted_iota(I32, (POOL_ROWS, POOL_WIDTH), 0)
        prev_sc[...] = jnp.where(prow >= POOL_ROWS - hist, p0_ref[0], 0.0)

    @pl.when((b == 0) & (l == 0))
    def _():
        cnt_sc[...] = jnp.zeros_like(cnt_sc)

    x = x_ref[...]
    h = _rms(x, nmix_ref[...]).astype(BF16)
    z = jnp.dot(h, wqkvg_ref[...], preferred_element_type=F32)
    q = z[:, 0:GLA_KW] * (GLA_DK ** -0.5)
    k = z[:, GLA_KW:2 * GLA_KW]
    v = z[:, 2 * GLA_KW:2 * GLA_KW + GLA_WIDTH]
    g = z[:, 2 * GLA_KW + GLA_WIDTH:]

    r = jnp.dot(h, wr_ref[...], preferred_element_type=F32)
    gp = jnp.dot(r.astype(BF16), wgate_ref[...], preferred_element_type=F32) + bgate_ref[...]
    la = jax.nn.log_sigmoid(gp) * (1.0 / GATE_TAU)

    hi = la.astype(BF16)
    r1 = la - hi.astype(F32)
    mid = r1.astype(BF16)
    lo = (r1 - mid.astype(F32)).astype(BF16)
    la3 = jnp.concatenate([hi, mid, lo], axis=1)
    bb = jnp.dot(tri2_ref[...], la3, preferred_element_type=F32)
    bsum = (bb[:, 0:GLA_KW] + bb[:, GLA_KW:2 * GLA_KW]) + bb[:, 2 * GLA_KW:]
    bcum = bsum[:rows]
    btot = bsum[rows:]

    qd = q * jnp.exp(bcum)
    kd = k * jnp.exp(-bcum)
    kl = k * jnp.exp(btot - bcum)

    lane_kw = lax.broadcasted_iota(I32, (rows, GLA_KW), 1)
    rowi = lax.broadcasted_iota(I32, (rows, rows), 0)
    coli = lax.broadcasted_iota(I32, (rows, rows), 1)
    amask = (coli <= rowi) & (coli >= (rowi & ~(CHUNK - 1)))
    vb = v.astype(BF16)

    o_parts = []
    for hh in range(GLA_HEADS):
        mh = (lane_kw >> 6) == hh
        qh = jnp.where(mh, qd, 0.0).astype(BF16)
        kh = jnp.where(mh, kd, 0.0).astype(BF16)
        a = lax.dot_general(qh, kh, NT_DIMS, preferred_element_type=F32)
        a = jnp.where(amask, a, 0.0).astype(BF16)
        o_parts.append(jnp.dot(a, vb[:, hh * GLA_DV:(hh + 1) * GLA_DV], preferred_element_type=F32))
    o_intra = jnp.concatenate(o_parts, axis=1)

    srow = lax.broadcasted_iota(I32, (GLA_WIDTH, GLA_KW), 0)
    scol = lax.broadcasted_iota(I32, (GLA_WIDTH, GLA_KW), 1)
    smask = (srow >> 7) == (scol >> 6)
    qdb = qd.astype(BF16)
    klb = kl.astype(BF16)
    st = st_sc[...]
    oi_parts = []
    for n in range(n_chunks):
        lo_r, hi_r = n * CHUNK, (n + 1) * CHUNK
        oi_parts.append(lax.dot_general(qdb[lo_r:hi_r], st.astype(BF16), NT_DIMS, preferred_element_type=F32))
        upd = lax.dot_general(vb[lo_r:hi_r], klb[lo_r:hi_r], TN_DIMS, preferred_element_type=F32)
        dec = jnp.exp(btot[lo_r:lo_r + 1, :])
        st = st * dec + jnp.where(smask, upd, 0.0)
    st_sc[...] = st
    sto_ref[0] = st
    o_inter = oi_parts[0] if n_chunks == 1 else jnp.concatenate(oi_parts, axis=0)
    o = o_intra + o_inter

    gains = ggain_ref[...]
    on_parts = []
    for hh in range(GLA_HEADS):
        oh = o[:, hh * GLA_DV:(hh + 1) * GLA_DV]
        ms = jnp.mean(oh * oh, axis=-1, keepdims=True)
        on_parts.append(oh * lax.rsqrt(ms + EPS) * gains[:, hh * GLA_DV:(hh + 1) * GLA_DV])
    on = jnp.concatenate(on_parts, axis=1) * (g * jax.nn.sigmoid(g))

    u = jnp.dot(h, wu_ref[...], preferred_element_type=F32)
    ext = jnp.concatenate([prev_sc[...], u], axis=0).astype(BF16)
    pos = l * rows + lax.broadcasted_iota(I32, (rows, POOL_GC), 0)
    p_parts = []
    for gi, w in enumerate(POOL_WINDOWS):
        cols = slice(gi * POOL_GC, (gi + 1) * POOL_GC)
        s = jnp.dot(band_ref[gi], ext[:, cols], preferred_element_type=F32)
        cnt_w = jnp.minimum(w, pos + 1 + hist).astype(F32)
        dd = s / cnt_w - u[:, cols]
        p_parts.append(jnp.dot(dd.astype(BF16), poolw_ref[gi], preferred_element_type=F32))
    p = jnp.concatenate(p_parts, axis=1) * pscale_ref[...]
    tail = u[rows - POOL_ROWS:rows, :]
    prev_sc[...] = tail
    po_ref[0] = tail

    cat = jnp.concatenate([on, p], axis=1).astype(BF16)
    x1 = x + jnp.dot(cat, wout_ref[...], preferred_element_type=F32)

    h2 = _rms(x1, nq_ref[...]).astype(BF16)
    qm = jnp.dot(h2, wmq_ref[...], preferred_element_type=F32).astype(BF16)
    kk = k_ref[0].astype(BF16)
    vv = v_ref[0].astype(BF16)
    a_parts = []
    for hh in range(MEM_HEADS):
        cols = slice(hh * MEM_HD, (hh + 1) * MEM_HD)
        s = lax.dot_general(qm[:, cols], kk[:, cols], NT_DIMS, preferred_element_type=F32) * (MEM_HD ** -0.5)
        e = jnp.exp(s - jnp.max(s, axis=-1, keepdims=True))
        pr = e / jnp.sum(e, axis=-1, keepdims=True)
        a_parts.append(jnp.dot(pr.astype(BF16), vv[:, cols], preferred_element_type=F32))
    att = jnp.concatenate(a_parts, axis=1).astype(BF16)
    x2 = x1 + jnp.dot(att, wmo_ref[...], preferred_element_type=F32)
    x2_ref[...] = x2

    h3 = _rms(x2, nmoe_ref[...]).astype(BF16)
    logits = jnp.dot(h3, wrt_ref[...], preferred_element_type=F32) + brt_ref[...]
    lane = lax.broadcasted_iota(I32, (rows, LANES), 1)
    lane_f = lane.astype(F32)
    work = logits
    vals, idxs = [], []
    for _ in range(TOP_K):
        m = jnp.max(work, axis=-1, keepdims=True)
        idx = jnp.min(jnp.where(work == m, lane_f, float(LANES)), axis=-1, keepdims=True)
        vals.append(m)
        idxs.append(idx)
        work = jnp.where(lane_f == idx, -jnp.inf, work)
    exps = [jnp.exp(vk - vals[0]) for vk in vals]
    den = (exps[0] + exps[1]) + (exps[2] + exps[3])
    gates = [ek / den for ek in exps]

    hot = jnp.zeros((rows, LANES), F32)
    for idx in idxs:
        hot = hot + jnp.where(lane_f == idx, 1.0, 0.0)
    prefix = jnp.dot(ltri_ref[...], hot.astype(BF16), preferred_element_type=F32) + cnt_sc[0:1, :]
    ri = jnp.zeros((rows, LANES), F32)
    rg = jnp.zeros((rows, LANES), F32)
    for kk_, idx in enumerate(idxs):
        rank = jnp.sum(jnp.where(lane_f == idx, prefix, 0.0), axis=-1, keepdims=True)
        ri = jnp.where(lane == kk_, idx, ri)
        ri = jnp.where(lane == TOP_K + kk_, rank, ri)
        rg = jnp.where(lane == kk_, gates[kk_], rg)
    ri_ref[...] = ri.astype(I32)
    rg_ref[...] = rg
    new_cnt = cnt_sc[0:1, :] + jnp.sum(hot, axis=0, keepdims=True)
    cnt_sc[...] = jnp.broadcast_to(new_cnt, cnt_sc.shape)
    cnt_ref[...] = jnp.broadcast_to(new_cnt, cnt_ref.shape)


def _premoe_consts(rows):
    i = np.arange(rows)[:, None]
    j = np.arange(rows)[None, :]
    same = (i // CHUNK) == (j // CHUNK)
    tri = (same & (j <= i)).astype(np.float32)
    tot = same.astype(np.float32)
    tri2 = np.concatenate([tri, tot], axis=0)
    ltri = (j < i).astype(np.float32)
    je = np.arange(rows + POOL_ROWS)[None, :]
    band = np.stack([((je <= i + POOL_ROWS) & (je > i + POOL_ROWS - w)).astype(np.float32) for w in POOL_WINDOWS])
    return jnp.asarray(tri2, BF16), jnp.asarray(ltri, BF16), jnp.asarray(band, BF16)


def _premoe(x2d, batch, seq, hist_len, s0t, pool0, kmem, vmem, w):
    rows = min(PREMOE_ROWS, seq)
    nl = seq // rows
    t = batch * seq
    hist = min(int(hist_len), POOL_BUF)
    tri2, ltri, band = _premoe_consts(rows)
    row_spec = pl.BlockSpec((rows, D_MODEL), lambda b, l: (b * nl + l, 0))
    lane_spec = pl.BlockSpec((rows, LANES), lambda b, l: (b * nl + l, 0))

    def per_batch(shape):
        return pl.BlockSpec((1,) + shape, lambda b, l: (b, 0, 0))

    consts = [w["norm_mix"], w["w_qkvg"], w["w_r"], w["w_u"], w["w_gate"], w["b_gate"], w["gla_gain"], w["pool_w"],
              w["pool_scale"], w["w_out"], w["norm_mem_q"], w["w_mq"], w["w_mo"], w["norm_moe"], w["w_router"],
              w["b_router"], tri2, ltri, band]
    in_specs = [row_spec, per_batch((GLA_WIDTH, GLA_KW)), per_batch((POOL_ROWS, POOL_WIDTH)),
                per_batch((N_MEM, D_MODEL)), per_batch((N_MEM, D_MODEL))] + [_const_spec(c.shape) for c in consts]
    out_specs = [row_spec, lane_spec, lane_spec, pl.BlockSpec((8, LANES), lambda b, l: (0, 0)),
                 per_batch((GLA_WIDTH, GLA_KW)), per_batch((POOL_ROWS, POOL_WIDTH))]
    out_shape = [jax.ShapeDtypeStruct((t, D_MODEL), F32), jax.ShapeDtypeStruct((t, LANES), I32),
                 jax.ShapeDtypeStruct((t, LANES), F32), jax.ShapeDtypeStruct((8, LANES), F32),
                 jax.ShapeDtypeStruct((batch, GLA_WIDTH, GLA_KW), F32),
                 jax.ShapeDtypeStruct((batch, POOL_ROWS, POOL_WIDTH), F32)]
    return pl.pallas_call(
        functools.partial(_premoe_body, hist, rows),
        grid=(batch, nl),
        in_specs=in_specs,
        out_specs=out_specs,
        out_shape=out_shape,
        scratch_shapes=[pltpu.VMEM((GLA_WIDTH, GLA_KW), F32), pltpu.VMEM((POOL_ROWS, POOL_WIDTH), F32),
                        pltpu.VMEM((8, LANES), F32)],
        compiler_params=pltpu.CompilerParams(dimension_semantics=("arbitrary", "arbitrary"),
                                             vmem_limit_bytes=VMEM_LIMIT),
        name="premoe",
    )(x2d, s0t, pool0, kmem, vmem, *consts)


def _moe_body(tm, nu_ref, be_ref, nv_ref, tokc_ref, tokn_ref, dst_ref, wt_ref, x2_hbm, nmoe_ref,
              wup_ref, bup_ref, wdn_ref, bdn_ref, y_hbm, xbuf, ybuf, gsem, ssem):
    del be_ref
    b = pl.program_id(0)
    nu = nu_ref[0]
    slot = lax.rem(b, 2)

    def gather_copy(tok, j, s):
        return pltpu.make_async_copy(x2_hbm.at[pl.ds(tok, 1)], xbuf.at[s, pl.ds(j, 1)], gsem.at[s])

    def scatter_copy(dst, j, s):
        return pltpu.make_async_copy(ybuf.at[s, pl.ds(j, 1)], y_hbm.at[pl.ds(dst, 1)], ssem.at[s])

    def gather_start(tok_ref, s):
        def body(j, c):
            gather_copy(tok_ref[0, 0, j], j, s).start()
            return c
        lax.fori_loop(0, tm, body, 0, unroll=8)

    def gather_wait(s):
        def body(j, c):
            gather_copy(0, j, s).wait()
            return c
        lax.fori_loop(0, tm, body, 0, unroll=8)

    def scatter_start(blk, s):
        def body(j, c):
            scatter_copy(dst_ref[0, 0, j], j, s).start()
            return c
        lax.fori_loop(0, nv_ref[blk], body, 0)

    def scatter_wait(blk, s):
        def body(j, c):
            scatter_copy(0, j, s).wait()
            return c
        lax.fori_loop(0, nv_ref[blk], body, 0)

    @pl.when(b < nu)
    def _():
        @pl.when(b == 0)
        def _():
            gather_start(tokc_ref, 0)

        @pl.when(b + 1 < nu)
        def _():
            gather_start(tokn_ref, 1 - slot)

        gather_wait(slot)

        @pl.when(b >= 2)
        def _():
            scatter_wait(b - 2, slot)

        hh = _rms(xbuf[slot], nmoe_ref[...]).astype(BF16)
        uu = jnp.dot(hh, wup_ref[0], preferred_element_type=F32) + bup_ref[0]
        glu = jnp.minimum(uu[:, :D_FF], SWIGLU_LIMIT)
        lin = jnp.clip(uu[:, D_FF:], -SWIGLU_LIMIT, SWIGLU_LIMIT)
        act = glu * jax.nn.sigmoid(SWIGLU_ALPHA * glu) * (lin + 1.0)
        yb = jnp.dot(act.astype(BF16), wdn_ref[0], preferred_element_type=F32) + bdn_ref[0]
        ybuf[slot] = yb * wt_ref[...]
        scatter_start(b, slot)

        @pl.when(b == nu - 1)
        def _():
            scatter_wait(b, slot)

            @pl.when(b >= 1)
            def _():
                scatter_wait(b - 1, 1 - slot)


def _moe(x2, route_i, route_g, counts_f, w):
    t = x2.shape[0]
    tm = MOE_ROWS
    s_total = t * TOP_K
    nb = (s_total + N_EXPERTS * (tm - 1) + tm - 1) // tm
    p_total = nb * tm

    e = route_i[:, :TOP_K]
    rank = route_i[:, TOP_K:2 * TOP_K]
    gates = route_g[:, :TOP_K]
    counts = counts_f[0, :N_EXPERTS].astype(I32)
    padded = ((counts + tm - 1) // tm) * tm
    pend = jnp.cumsum(padded)
    pstart = pend - padded
    dest = (pstart[e] + rank).reshape(-1)
    tok = jnp.repeat(jnp.arange(t, dtype=I32), TOP_K)
    kplane = jnp.tile(jnp.arange(TOP_K, dtype=I32), t)
    tok_buf = jnp.zeros((p_total,), I32).at[dest].set(tok)
    dst_buf = jnp.zeros((p_total,), I32).at[dest].set(kplane * t + tok)
    w_buf = jnp.zeros((p_total,), F32).at[dest].set(gates.reshape(-1))
    blk_start = jnp.arange(nb, dtype=I32) * tm
    block_e = jnp.minimum(jnp.searchsorted(pend, blk_start, side="right"), N_EXPERTS - 1).astype(I32)
    n_valid = jnp.clip(pstart[block_e] + counts[block_e] - blk_start, 0, tm).astype(I32)
    n_used = (pend[-1] // tm).astype(I32).reshape(1)

    tok3 = tok_buf.reshape(nb, 1, tm)
    dst3 = dst_buf.reshape(nb, 1, tm)
    smem_cur = pl.BlockSpec((1, 1, tm), lambda b, *_: (b, 0, 0), memory_space=pltpu.SMEM)
    smem_next = pl.BlockSpec((1, 1, tm), lambda b, *_: (jnp.minimum(b + 1, nb - 1), 0, 0), memory_space=pltpu.SMEM)

    def per_expert(shape):
        return pl.BlockSpec((1,) + shape, lambda b, nu, be, nv: (be[b], 0, 0))

    grid_spec = pltpu.PrefetchScalarGridSpec(
        num_scalar_prefetch=3,
        grid=(nb,),
        in_specs=[
            smem_cur, smem_next, smem_cur,
            pl.BlockSpec((tm, 1), lambda b, *_: (b, 0)),
            pl.BlockSpec(memory_space=pl.ANY),
            pl.BlockSpec((1, D_MODEL), lambda b, *_: (0, 0)),
            per_expert((D_MODEL, 2 * D_FF)), per_expert((1, 2 * D_FF)),
            per_expert((D_FF, D_MODEL)), per_expert((1, D_MODEL)),
        ],
        out_specs=pl.BlockSpec(memory_space=pl.ANY),
        scratch_shapes=[pltpu.VMEM((2, tm, D_MODEL), F32), pltpu.VMEM((2, tm, D_MODEL), F32),
                        pltpu.SemaphoreType.DMA((2,)), pltpu.SemaphoreType.DMA((2,))],
    )
    return pl.pallas_call(
        functools.partial(_moe_body, tm),
        grid_spec=grid_spec,
        out_shape=jax.ShapeDtypeStruct((TOP_K * t, D_MODEL), F32),
        compiler_params=pltpu.CompilerParams(dimension_semantics=("arbitrary",), vmem_limit_bytes=VMEM_LIMIT),
        name="moe",
    )(n_used, block_e, n_valid, tok3, tok3, dst3, w_buf.reshape(p_total, 1), x2, w["norm_moe"],
      w["w_up"], w["b_up"], w["w_down"], w["b_down"])


def _final_body(x_ref, y0_ref, y1_ref, y2_ref, y3_ref, g_ref, o_ref):
    acc = x_ref[...] + ((y0_ref[...] + y1_ref[...]) + (y2_ref[...] + y3_ref[...]))
    o_ref[...] = _rms(acc, g_ref[...])


def _final(x2, y, g):
    t = x2.shape[0]
    tm = min(FINAL_ROWS, t)
    nblk = t // tm
    row = pl.BlockSpec((tm, D_MODEL), lambda i: (i, 0))

    def plane(kk):
        return pl.BlockSpec((tm, D_MODEL), lambda i: (kk * nblk + i, 0))

    return pl.pallas_call(
        _final_body,
        grid=(nblk,),
        in_specs=[row, plane(0), plane(1), plane(2), plane(3), _const_spec((1, D_MODEL))],
        out_specs=row,
        out_shape=jax.ShapeDtypeStruct((t, D_MODEL), F32),
        compiler_params=pltpu.CompilerParams(dimension_semantics=("arbitrary",), vmem_limit_bytes=VMEM_LIMIT),
        name="final",
    )(x2, y, y, y, y, g)


def _state_to_t(s):
    bsz = s.shape[0]
    st = jnp.zeros((bsz, GLA_HEADS, GLA_DV, GLA_HEADS, GLA_DK), F32)
    for hh in range(GLA_HEADS):
        st = st.at[:, hh, :, hh, :].set(jnp.swapaxes(s[:, hh], 1, 2).astype(F32))
    return st.reshape(bsz, GLA_WIDTH, GLA_KW)


def _state_from_t(st):
    bsz = st.shape[0]
    s5 = st.reshape(bsz, GLA_HEADS, GLA_DV, GLA_HEADS, GLA_DK)
    return jnp.stack([jnp.swapaxes(s5[:, hh, :, hh, :], 1, 2) for hh in range(GLA_HEADS)], axis=1)


def _prep_weights(norm_mix, w_in, w_gate, b_gate, gla_gain, pool_w, pool_scale, w_out, norm_mem_q, w_mq, w_mo,
                  norm_moe, w_router, b_router, w_up, b_up, w_down, b_down):
    n_qkvg = 2 * GLA_KW + 2 * GLA_WIDTH
    w_r = jnp.zeros((D_MODEL, LANES), BF16).at[:, :GATE_RANK].set(w_in[:, n_qkvg:n_qkvg + GATE_RANK].astype(BF16))
    w_g = jnp.zeros((LANES, GLA_KW), BF16).at[:GATE_RANK].set(w_gate.astype(BF16))
    w_rt = jnp.zeros((D_MODEL, LANES), BF16).at[:, :N_EXPERTS].set(w_router.astype(BF16))
    b_rt = jnp.full((1, LANES), NEG_BIG, F32).at[0, :N_EXPERTS].set(b_router)
    return {
        "norm_mix": norm_mix.reshape(1, -1),
        "w_qkvg": w_in[:, :n_qkvg].astype(BF16),
        "w_r": w_r,
        "w_u": w_in[:, n_qkvg + GATE_RANK:].astype(BF16),
        "w_gate": w_g,
        "b_gate": b_gate.reshape(1, -1),
        "gla_gain": gla_gain.reshape(1, -1),
        "pool_w": pool_w.astype(BF16),
        "pool_scale": pool_scale.reshape(1, -1),
        "w_out": w_out.astype(BF16),
        "norm_mem_q": norm_mem_q.reshape(1, -1),
        "w_mq": w_mq.astype(BF16),
        "w_mo": w_mo.astype(BF16),
        "norm_moe": norm_moe.reshape(1, -1),
        "w_router": w_rt,
        "b_router": b_rt,
        "w_up": w_up.astype(BF16),
        "b_up": b_up.reshape(N_EXPERTS, 1, -1),
        "w_down": w_down.astype(BF16),
        "b_down": b_down.reshape(N_EXPERTS, 1, -1),
    }


def _block(x, s_gla, pool_prev, hist_len, mk, mv, w, norm_final):
    batch, seq, _ = x.shape
    x2d = x.reshape(batch * seq, D_MODEL)
    pool0 = jnp.concatenate([jnp.zeros((batch, 1, POOL_WIDTH), F32), pool_prev.astype(F32)], axis=1)
    x2, ri, rg, cnt, st_t, pool_t = _premoe(x2d, batch, seq, hist_len, _state_to_t(s_gla), pool0,
                                            mk.reshape(batch, N_MEM, D_MODEL), mv.reshape(batch, N_MEM, D_MODEL), w)
    y = _moe(x2, ri, rg, cnt, w)
    out = _final(x2, y, norm_final.reshape(1, -1))
    return out.reshape(batch, seq, D_MODEL), _state_from_t(st_t), pool_t[:, 1:, :]


def kernel(x_prompt, x_sample, mem_prompt, state_gla, state_pool, cache_mem_k, cache_mem_v, norm_mix, w_in, w_gate, b_gate, gla_gain, pool_w, pool_scale, w_out, norm_mem_q, norm_mem_kv, w_mq, w_mk, w_mv, w_mo, norm_moe, w_router, b_router, w_up, b_up, w_down, b_down, norm_final):
    depth = w_in.shape[0]
    assert depth == 1
    xp, xs = x_prompt, x_sample
    bp = xp.shape[0]
    gla_p, pool_p, mk_p, mv_p, gla_s, pool_s = [], [], [], [], [], []
    for l in range(depth):
        w = _prep_weights(norm_mix[l], w_in[l], w_gate[l], b_gate[l], gla_gain[l], pool_w[l], pool_scale[l], w_out[l],
                          norm_mem_q[l], w_mq[l], w_mo[l], norm_moe[l], w_router[l], b_router[l],
                          w_up[l], b_up[l], w_down[l], b_down[l])
        mk2, mv2 = _mem_kv(mem_prompt.reshape(bp * N_MEM, D_MODEL), norm_mem_kv[l].reshape(1, -1),
                           w_mk[l].astype(BF16), w_mv[l].astype(BF16))
        mk = mk2.reshape(bp, N_MEM, MEM_HEADS, MEM_HD)
        mv = mv2.reshape(bp, N_MEM, MEM_HEADS, MEM_HD)
        xs, ss, ps = _block(xs, state_gla[l], state_pool[l], PAST_LEN, cache_mem_k[l], cache_mem_v[l], w, norm_final)
        gla_s.append(ss)
        pool_s.append(ps)
        s0 = jnp.zeros((bp, GLA_HEADS, GLA_DK, GLA_DV), F32)
        p0 = jnp.zeros((bp, POOL_BUF, POOL_WIDTH), F32)
        xp, sp, pp = _block(xp, s0, p0, 0, mk, mv, w, norm_final)
        gla_p.append(sp)
        pool_p.append(pp)
        mk_p.append(mk)
        mv_p.append(mv)
    return (xp, xs, jnp.stack(gla_p), jnp.stack(pool_p), jnp.stack(mk_p), jnp.stack(mv_p),
            jnp.stack(gla_s), jnp.stack(pool_s))
```

```python
import functools

import numpy as np
import jax
import jax.numpy as jnp
from jax import lax
from jax.experimental import pallas as pl
from jax.experimental.pallas import tpu as pltpu

F32 = jnp.float32
BF16 = jnp.bfloat16
I32 = jnp.int32

D_MODEL = 1024
CHUNK = 64
GLA_HEADS = 4
GLA_DK = 64
GLA_DV = 128
GLA_KW = GLA_HEADS * GLA_DK
GLA_WIDTH = GLA_HEADS * GLA_DV
GATE_RANK = 16
GATE_TAU = 16.0
POOL_WIDTH = 512
POOL_GC = 128
POOL_WINDOWS = (2, 4, 8, 16)
POOL_BUF = 15
PAST_LEN = 4096
POOL_ROWS = 16
N_MEM = 256
MEM_HEADS = 4
MEM_HD = 256
N_EXPERTS = 32
TOP_K = 4
D_FF = 1024
SWIGLU_ALPHA = 1.702
SWIGLU_LIMIT = 7.0
EPS = 1e-6
LANES = 128
NEG_BIG = -1e30

PREMOE_ROWS = 256
MOE_ROWS = 512
DISPATCH_ROWS = 256
COMBINE_ROWS = 256
VMEM_LIMIT = 56 * 1024 * 1024

NT_DIMS = (((1,), (1,)), ((), ()))
TN_DIMS = (((0,), (0,)), ((), ()))


def _rms(x, g):
    ms = jnp.mean(x * x, axis=-1, keepdims=True)
    return x * lax.rsqrt(ms + EPS) * g


def _const_spec(shape):
    nd = len(shape)
    return pl.BlockSpec(shape, lambda *_: (0,) * nd, pipeline_mode=pl.Buffered(1))


def _memkv_body(m_ref, g_ref, wk_ref, wv_ref, k_ref, v_ref):
    m = _rms(m_ref[...], g_ref[...]).astype(BF16)
    k_ref[...] = jnp.dot(m, wk_ref[...], preferred_element_type=F32)
    v_ref[...] = jnp.dot(m, wv_ref[...], preferred_element_type=F32)


def _mem_kv(mem2d, g, wk, wv):
    n = mem2d.shape[0]
    tm = 512
    row = pl.BlockSpec((tm, D_MODEL), lambda i: (i, 0))
    return pl.pallas_call(
        _memkv_body,
        grid=(n // tm,),
        in_specs=[row, _const_spec((1, D_MODEL)), _const_spec((D_MODEL, D_MODEL)), _const_spec((D_MODEL, D_MODEL))],
        out_specs=[row, row],
        out_shape=[jax.ShapeDtypeStruct((n, D_MODEL), F32)] * 2,
        compiler_params=pltpu.CompilerParams(dimension_semantics=("arbitrary",), vmem_limit_bytes=VMEM_LIMIT),
        name="mem_kv",
    )(mem2d, g, wk, wv)


def _premoe_body(hist, rows,
                 x_ref, s0_ref, p0_ref, k_ref, v_ref,
                 nmix_ref, wqkvg_ref, wr_ref, wu_ref, wgate_ref, bgate_ref, ggain_ref, poolw_ref, pscale_ref,
                 wout_ref, nq_ref, wmq_ref, wmo_ref, nmoe_ref, wrt_ref, brt_ref,
                 tri2_ref, ltri_ref, band_ref,
                 x2_ref, ri_ref, rg_ref, cnt_ref, sto_ref, po_ref,
                 st_sc, prev_sc, cnt_sc):
    b = pl.program_id(0)
    l = pl.program_id(1)
    n_chunks = rows // CHUNK

    @pl.when(l == 0)
    def _():
        st_sc[...] = s0_ref[0]
        prow = lax.broadcasted_iota(I32, (POOL_ROWS, POOL_WIDTH), 0)
        prev_sc[...] = jnp.where(prow >= POOL_ROWS - hist, p0_ref[0], 0.0)

    @pl.when((b == 0) & (l == 0))
    def _():
        cnt_sc[...] = jnp.zeros_like(cnt_sc)

    x = x_ref[...]
    h = _rms(x, nmix_ref[...]).astype(BF16)
    z = jnp.dot(h, wqkvg_ref[...], preferred_element_type=F32)
    q = z[:, 0:GLA_KW] * (GLA_DK ** -0.5)
    k = z[:, GLA_KW:2 * GLA_KW]
    v = z[:, 2 * GLA_KW:2 * GLA_KW + GLA_WIDTH]
    g = z[:, 2 * GLA_KW + GLA_WIDTH:]

    r = jnp.dot(h, wr_ref[...], preferred_element_type=F32)
    gp = jnp.dot(r.astype(BF16), wgate_ref[...], preferred_element_type=F32) + bgate_ref[...]
    la = jax.nn.log_sigmoid(gp) * (1.0 / GATE_TAU)

    hi = la.astype(BF16)
    r1 = la - hi.astype(F32)
    mid = r1.astype(BF16)
    lo = (r1 - mid.astype(F32)).astype(BF16)
    la3 = jnp.concatenate([hi, mid, lo], axis=1)
    bb = jnp.dot(tri2_ref[...], la3, preferred_element_type=F32)
    bsum = (bb[:, 0:GLA_KW] + bb[:, GLA_KW:2 * GLA_KW]) + bb[:, 2 * GLA_KW:]
    bcum = bsum[:rows]
    btot = bsum[rows:]

    qd = q * jnp.exp(bcum)
    kd = k * jnp.exp(-bcum)
    kl = k * jnp.exp(btot - bcum)

    lane_kw = lax.broadcasted_iota(I32, (rows, GLA_KW), 1)
    rowi = lax.broadcasted_iota(I32, (rows, rows), 0)
    coli = lax.broadcasted_iota(I32, (rows, rows), 1)
    amask = (coli <= rowi) & (coli >= (rowi & ~(CHUNK - 1)))
    vb = v.astype(BF16)

    o_parts = []
    for hh in range(GLA_HEADS):
        mh = (lane_kw >> 6) == hh
        qh = jnp.where(mh, qd, 0.0).astype(BF16)
        kh = jnp.where(mh, kd, 0.0).astype(BF16)
        a = lax.dot_general(qh, kh, NT_DIMS, preferred_element_type=F32)
        a = jnp.where(amask, a, 0.0).astype(BF16)
        o_parts.append(jnp.dot(a, vb[:, hh * GLA_DV:(hh + 1) * GLA_DV], preferred_element_type=F32))
    o_intra = jnp.concatenate(o_parts, axis=1)

    srow = lax.broadcasted_iota(I32, (GLA_WIDTH, GLA_KW), 0)
    scol = lax.broadcasted_iota(I32, (GLA_WIDTH, GLA_KW), 1)
    smask = (srow >> 7) == (scol >> 6)
    qdb = qd.astype(BF16)
    klb = kl.astype(BF16)
    st = st_sc[...]
    oi_parts = []
    for n in range(n_chunks):
        lo_r, hi_r = n * CHUNK, (n + 1) * CHUNK
        oi_parts.append(lax.dot_general(qdb[lo_r:hi_r], st.astype(BF16), NT_DIMS, preferred_element_type=F32))
        upd = lax.dot_general(vb[lo_r:hi_r], klb[lo_r:hi_r], TN_DIMS, preferred_element_type=F32)
        dec = jnp.exp(btot[lo_r:lo_r + 1, :])
        st = st * dec + jnp.where(smask, upd, 0.0)
    st_sc[...] = st
    sto_ref[0] = st
    o_inter = oi_parts[0] if n_chunks == 1 else jnp.concatenate(oi_parts, axis=0)
    o = o_intra + o_inter

    gains = ggain_ref[...]
    on_parts = []
    for hh in range(GLA_HEADS):
        oh = o[:, hh * GLA_DV:(hh + 1) * GLA_DV]
        ms = jnp.mean(oh * oh, axis=-1, keepdims=True)
        on_parts.append(oh * lax.rsqrt(ms + EPS) * gains[:, hh * GLA_DV:(hh + 1) * GLA_DV])
    on = jnp.concatenate(on_parts, axis=1) * (g * jax.nn.sigmoid(g))

    u = jnp.dot(h, wu_ref[...], preferred_element_type=F32)
    ext = jnp.concatenate([prev_sc[...], u], axis=0).astype(BF16)
    pos = l * rows + lax.broadcasted_iota(I32, (rows, POOL_GC), 0)
    p_parts = []
    for gi, w in enumerate(POOL_WINDOWS):
        cols = slice(gi * POOL_GC, (gi + 1) * POOL_GC)
        s = jnp.dot(band_ref[gi], ext[:, cols], preferred_element_type=F32)
        cnt_w = jnp.minimum(w, pos + 1 + hist).astype(F32)
        dd = s / cnt_w - u[:, cols]
        p_parts.append(jnp.dot(dd.astype(BF16), poolw_ref[gi], preferred_element_type=F32))
    p = jnp.concatenate(p_parts, axis=1) * pscale_ref[...]
    tail = u[rows - POOL_ROWS:rows, :]
    prev_sc[...] = tail
    po_ref[0] = tail

    cat = jnp.concatenate([on, p], axis=1).astype(BF16)
    x1 = x + jnp.dot(cat, wout_ref[...], preferred_element_type=F32)

    h2 = _rms(x1, nq_ref[...]).astype(BF16)
    qm = jnp.dot(h2, wmq_ref[...], preferred_element_type=F32).astype(BF16)
    kk = k_ref[0].astype(BF16)
    vv = v_ref[0].astype(BF16)
    a_parts = []
    for hh in range(MEM_HEADS):
        cols = slice(hh * MEM_HD, (hh + 1) * MEM_HD)
        s = lax.dot_general(qm[:, cols], kk[:, cols], NT_DIMS, preferred_element_type=F32) * (MEM_HD ** -0.5)
        e = jnp.exp(s - jnp.max(s, axis=-1, keepdims=True))
        pr = e / jnp.sum(e, axis=-1, keepdims=True)
        a_parts.append(jnp.dot(pr.astype(BF16), vv[:, cols], preferred_element_type=F32))
    att = jnp.concatenate(a_parts, axis=1).astype(BF16)
    x2 = x1 + jnp.dot(att, wmo_ref[...], preferred_element_type=F32)
    x2_ref[...] = x2

    h3 = _rms(x2, nmoe_ref[...]).astype(BF16)
    logits = jnp.dot(h3, wrt_ref[...], preferred_element_type=F32) + brt_ref[...]
    lane = lax.broadcasted_iota(I32, (rows, LANES), 1)
    lane_f = lane.astype(F32)
    work = logits
    vals, idxs = [], []
    for _ in range(TOP_K):
        m = jnp.max(work, axis=-1, keepdims=True)
        idx = jnp.min(jnp.where(work == m, lane_f, float(LANES)), axis=-1, keepdims=True)
        vals.append(m)
        idxs.append(idx)
        work = jnp.where(lane_f == idx, -jnp.inf, work)
    exps = [jnp.exp(vk - vals[0]) for vk in vals]
    den = (exps[0] + exps[1]) + (exps[2] + exps[3])
    gates = [ek / den for ek in exps]

    hot = jnp.zeros((rows, LANES), F32)
    for idx in idxs:
        hot = hot + jnp.where(lane_f == idx, 1.0, 0.0)
    prefix = jnp.dot(ltri_ref[...], hot.astype(BF16), preferred_element_type=F32) + cnt_sc[0:1, :]
    ri = jnp.zeros((rows, LANES), F32)
    rg = jnp.zeros((rows, LANES), F32)
    for kk_, idx in enumerate(idxs):
        rank = jnp.sum(jnp.where(lane_f == idx, prefix, 0.0), axis=-1, keepdims=True)
        ri = jnp.where(lane == kk_, idx, ri)
        ri = jnp.where(lane == TOP_K + kk_, rank, ri)
        rg = jnp.where(lane == kk_, gates[kk_], rg)
    ri_ref[...] = ri.astype(I32)
    rg_ref[...] = rg
    new_cnt = cnt_sc[0:1, :] + jnp.sum(hot, axis=0, keepdims=True)
    cnt_sc[...] = jnp.broadcast_to(new_cnt, cnt_sc.shape)
    cnt_ref[...] = jnp.broadcast_to(new_cnt, cnt_ref.shape)


def _premoe_consts(rows):
    i = np.arange(rows)[:, None]
    j = np.arange(rows)[None, :]
    same = (i // CHUNK) == (j // CHUNK)
    tri = (same & (j <= i)).astype(np.float32)
    tot = same.astype(np.float32)
    tri2 = np.concatenate([tri, tot], axis=0)
    ltri = (j < i).astype(np.float32)
    je = np.arange(rows + POOL_ROWS)[None, :]
    band = np.stack([((je <= i + POOL_ROWS) & (je > i + POOL_ROWS - w)).astype(np.float32) for w in POOL_WINDOWS])
    return jnp.asarray(tri2, BF16), jnp.asarray(ltri, BF16), jnp.asarray(band, BF16)


def _premoe(x2d, batch, seq, hist_len, s0t, pool0, kmem, vmem, w):
    rows = min(PREMOE_ROWS, seq)
    nl = seq // rows
    t = batch * seq
    hist = min(int(hist_len), POOL_BUF)
    tri2, ltri, band = _premoe_consts(rows)
    row_spec = pl.BlockSpec((rows, D_MODEL), lambda b, l: (b * nl + l, 0))
    lane_spec = pl.BlockSpec((rows, LANES), lambda b, l: (b * nl + l, 0))

    def per_batch(shape):
        return pl.BlockSpec((1,) + shape, lambda b, l: (b, 0, 0))

    consts = [w["norm_mix"], w["w_qkvg"], w["w_r"], w["w_u"], w["w_gate"], w["b_gate"], w["gla_gain"], w["pool_w"],
              w["pool_scale"], w["w_out"], w["norm_mem_q"], w["w_mq"], w["w_mo"], w["norm_moe"], w["w_router"],
              w["b_router"], tri2, ltri, band]
    in_specs = [row_spec, per_batch((GLA_WIDTH, GLA_KW)), per_batch((POOL_ROWS, POOL_WIDTH)),
                per_batch((N_MEM, D_MODEL)), per_batch((N_MEM, D_MODEL))] + [_const_spec(c.shape) for c in consts]
    out_specs = [row_spec, lane_spec, lane_spec, pl.BlockSpec((8, LANES), lambda b, l: (0, 0)),
                 per_batch((GLA_WIDTH, GLA_KW)), per_batch((POOL_ROWS, POOL_WIDTH))]
    out_shape = [jax.ShapeDtypeStruct((t, D_MODEL), F32), jax.ShapeDtypeStruct((t, LANES), I32),
                 jax.ShapeDtypeStruct((t, LANES), F32), jax.ShapeDtypeStruct((8, LANES), F32),
                 jax.ShapeDtypeStruct((batch, GLA_WIDTH, GLA_KW), F32),
                 jax.ShapeDtypeStruct((batch, POOL_ROWS, POOL_WIDTH), F32)]
    return pl.pallas_call(
        functools.partial(_premoe_body, hist, rows),
        grid=(batch, nl),
        in_specs=in_specs,
        out_specs=out_specs,
        out_shape=out_shape,
        scratch_shapes=[pltpu.VMEM((GLA_WIDTH, GLA_KW), F32), pltpu.VMEM((POOL_ROWS, POOL_WIDTH), F32),
                        pltpu.VMEM((8, LANES), F32)],
        compiler_params=pltpu.CompilerParams(dimension_semantics=("arbitrary", "arbitrary"),
                                             vmem_limit_bytes=VMEM_LIMIT),
        name="premoe",
    )(x2d, s0t, pool0, kmem, vmem, *consts)


def _route_tables(route_i, counts_f, tm, nb):
    e = route_i[:, :TOP_K]
    rank = route_i[:, TOP_K:2 * TOP_K]
    counts = counts_f[0, :N_EXPERTS].astype(I32)
    padded = ((counts + tm - 1) // tm) * tm
    pend = jnp.cumsum(padded)
    pstart = pend - padded
    eids = jnp.arange(N_EXPERTS, dtype=I32)
    dest = jnp.sum(jnp.where(e[:, :, None] == eids, pstart, 0), axis=-1) + rank
    blk_start = jnp.arange(nb, dtype=I32) * tm
    block_e = jnp.minimum(jnp.sum((pend[None, :] <= blk_start[:, None]).astype(I32), axis=1), N_EXPERTS - 1)
    n_used = (pend[-1] // tm).astype(I32).reshape(1)
    return dest.astype(I32), block_e.astype(I32), n_used, pend.astype(I32), counts


def _dispatch_body(rows, tm, nb, nu_ref, pend_ref, cnt_ref, dest_ref, x_ref, g_ref, xg_hbm, hbuf, zbuf, sem, zsem):
    i = pl.program_id(0)
    n_steps = pl.num_programs(0)
    slot = lax.rem(i, 2)
    nu = nu_ref[0]
    groups = rows // 8

    def zero_copy(start):
        return pltpu.make_async_copy(zbuf, xg_hbm.at[pl.ds(start // 8, tm // 8)], zsem)

    def for_each_fill(fn):
        def per_expert(e, c):
            @pl.when(cnt_ref[e] > 0)
            def _():
                fn(pend_ref[e] - tm)
            return c
        lax.fori_loop(0, N_EXPERTS, per_expert, 0)

        def per_tail(b, c):
            fn(b * tm)
            return c
        lax.fori_loop(nu, nb, per_tail, 0)

    @pl.when(i == 0)
    def _():
        zbuf[...] = jnp.zeros_like(zbuf)
        for_each_fill(lambda start: zero_copy(start).start())
        for_each_fill(lambda start: zero_copy(start).wait())

    def row_wait(s):
        for _ in range(TOP_K):
            pltpu.make_async_copy(hbuf.at[s], xg_hbm.at[pl.ds(0, groups)], sem.at[s]).wait()

    @pl.when(i >= 2)
    def _():
        row_wait(slot)

    hbuf[slot] = _rms(x_ref[...], g_ref[...]).reshape(groups, 8, D_MODEL)

    def send_group(gi, c):
        base = gi * (8 * TOP_K)
        for r in range(8):
            for kk in range(TOP_K):
                dst = dest_ref[0, 0, base + r * TOP_K + kk]
                pltpu.make_async_copy(hbuf.at[slot, gi, pl.ds(r, 1)], xg_hbm.at[dst >> 3, pl.ds(dst & 7, 1)],
                                      sem.at[slot]).start()
        return c
    lax.fori_loop(0, groups, send_group, 0)

    @pl.when(i == n_steps - 1)
    def _():
        row_wait(slot)

        @pl.when(i >= 1)
        def _():
            row_wait(1 - slot)


def _dispatch(x2, dest, n_used, pend, counts, g, tm, nb):
    t = x2.shape[0]
    rows = min(DISPATCH_ROWS, t)
    steps = t // rows
    dest3 = dest.reshape(steps, 1, rows * TOP_K)
    grid_spec = pltpu.PrefetchScalarGridSpec(
        num_scalar_prefetch=3,
        grid=(steps,),
        in_specs=[
            pl.BlockSpec((1, 1, rows * TOP_K), lambda i, *_: (i, 0, 0), memory_space=pltpu.SMEM),
            pl.BlockSpec((rows, D_MODEL), lambda i, *_: (i, 0)),
            pl.BlockSpec((1, D_MODEL), lambda i, *_: (0, 0)),
        ],
        out_specs=pl.BlockSpec(memory_space=pl.ANY),
        scratch_shapes=[pltpu.VMEM((2, rows // 8, 8, D_MODEL), F32), pltpu.VMEM((tm // 8, 8, D_MODEL), F32),
                        pltpu.SemaphoreType.DMA((2,)), pltpu.SemaphoreType.DMA(())],
    )
    return pl.pallas_call(
        functools.partial(_dispatch_body, rows, tm, nb),
        grid_spec=grid_spec,
        out_shape=jax.ShapeDtypeStruct((nb * tm // 8, 8, D_MODEL), F32),
        compiler_params=pltpu.CompilerParams(dimension_semantics=("arbitrary",), vmem_limit_bytes=VMEM_LIMIT),
        name="dispatch",
    )(n_used, pend, counts, dest3, x2, g)


def _moe_body(nu_ref, be_ref, xg_ref, wup_ref, bup_ref, wdn_ref, bdn_ref, ys_ref):
    del be_ref
    b = pl.program_id(0)

    @pl.when(b < nu_ref[0])
    def _():
        hh = xg_ref[...].astype(BF16)
        uu = jnp.dot(hh, wup_ref[0], preferred_element_type=F32) + bup_ref[0]
        glu = jnp.minimum(uu[:, :D_FF], SWIGLU_LIMIT)
        lin = jnp.clip(uu[:, D_FF:], -SWIGLU_LIMIT, SWIGLU_LIMIT)
        act = glu * jax.nn.sigmoid(SWIGLU_ALPHA * glu) * (lin + 1.0)
        ys_ref[...] = jnp.dot(act.astype(BF16), wdn_ref[0], preferred_element_type=F32) + bdn_ref[0]

    @pl.when(b >= nu_ref[0])
    def _():
        ys_ref[...] = jnp.zeros_like(ys_ref)


def _moe(xg, block_e, n_used, w, tm, nb):
    def used(b, nu):
        return jnp.minimum(b, nu[0] - 1)

    def per_expert(shape):
        return pl.BlockSpec((1,) + shape, lambda b, nu, be: (be[used(b, nu)], 0, 0))

    grid_spec = pltpu.PrefetchScalarGridSpec(
        num_scalar_prefetch=2,
        grid=(nb,),
        in_specs=[
            pl.BlockSpec((tm, D_MODEL), lambda b, nu, be: (used(b, nu), 0)),
            per_expert((D_MODEL, 2 * D_FF)), per_expert((1, 2 * D_FF)),
            per_expert((D_FF, D_MODEL)), per_expert((1, D_MODEL)),
        ],
        out_specs=pl.BlockSpec((tm, D_MODEL), lambda b, nu, be: (b, 0)),
    )
    return pl.pallas_call(
        _moe_body,
        grid_spec=grid_spec,
        out_shape=jax.ShapeDtypeStruct((nb * tm, D_MODEL), F32),
        compiler_params=pltpu.CompilerParams(dimension_semantics=("arbitrary",), vmem_limit_bytes=VMEM_LIMIT),
        name="moe",
    )(n_used, block_e, xg, w["w_up"], w["b_up"], w["w_down"], w["b_down"])


def _combine_body(rows, destc_ref, destn_ref, x_ref, gate_ref, g_ref, ys_hbm, o_ref, gbuf, sem):
    i = pl.program_id(0)
    n_steps = pl.num_programs(0)
    slot = lax.rem(i, 2)
    groups = rows // 8

    def fetch(dest_ref, s):
        def fetch_group(gi, c):
            base = gi * (8 * TOP_K)
            for r in range(8):
                for kk in range(TOP_K):
                    src = dest_ref[0, 0, base + r * TOP_K + kk]
                    pltpu.make_async_copy(ys_hbm.at[src >> 3, pl.ds(src & 7, 1)], gbuf.at[s, kk, gi, pl.ds(r, 1)],
                                          sem.at[s]).start()
            return c
        lax.fori_loop(0, groups, fetch_group, 0)

    @pl.when(i == 0)
    def _():
        fetch(destc_ref, 0)

    @pl.when(i + 1 < n_steps)
    def _():
        fetch(destn_ref, 1 - slot)

    for kk in range(TOP_K):
        pltpu.make_async_copy(ys_hbm.at[pl.ds(0, groups)], gbuf.at[slot, kk], sem.at[slot]).wait()

    gate = gate_ref[...]
    acc = x_ref[...]
    for kk in range(TOP_K):
        acc = acc + gbuf[slot, kk].reshape(rows, D_MODEL) * gate[:, kk:kk + 1]
    o_ref[...] = _rms(acc, g_ref[...])


def _combine(x2, ys, dest, route_g, g):
    t = x2.shape[0]
    rows = min(COMBINE_ROWS, t)
    steps = t // rows
    dest3 = dest.reshape(steps, 1, rows * TOP_K)
    smem_cur = pl.BlockSpec((1, 1, rows * TOP_K), lambda i: (i, 0, 0), memory_space=pltpu.SMEM)
    smem_next = pl.BlockSpec((1, 1, rows * TOP_K), lambda i: (jnp.minimum(i + 1, steps - 1), 0, 0),
                             memory_space=pltpu.SMEM)
    return pl.pallas_call(
        functools.partial(_combine_body, rows),
        grid=(steps,),
        in_specs=[smem_cur, smem_next,
                  pl.BlockSpec((rows, D_MODEL), lambda i: (i, 0)),
                  pl.BlockSpec((rows, LANES), lambda i: (i, 0)),
                  pl.BlockSpec((1, D_MODEL), lambda i: (0, 0)),
                  pl.BlockSpec(memory_space=pl.ANY)],
        out_specs=pl.BlockSpec((rows, D_MODEL), lambda i: (i, 0)),
        out_shape=jax.ShapeDtypeStruct((t, D_MODEL), F32),
        scratch_shapes=[pltpu.VMEM((2, TOP_K, rows // 8, 8, D_MODEL), F32), pltpu.SemaphoreType.DMA((2,))],
        compiler_params=pltpu.CompilerParams(dimension_semantics=("arbitrary",), vmem_limit_bytes=VMEM_LIMIT),
        name="combine",
    )(dest3, dest3, x2, route_g, g, ys)


def _moe_layer(x2, route_i, route_g, counts_f, w, norm_final):
    t = x2.shape[0]
    tm = MOE_ROWS
    nb = (t * TOP_K + N_EXPERTS * (tm - 1) + tm - 1) // tm
    dest, block_e, n_used, pend, counts = _route_tables(route_i, counts_f, tm, nb)
    xg = _dispatch(x2, dest, n_used, pend, counts, w["norm_moe"], tm, nb)
    ys = _moe(xg.reshape(nb * tm, D_MODEL), block_e, n_used, w, tm, nb)
    return _combine(x2, ys.reshape(nb * tm // 8, 8, D_MODEL), dest, route_g, norm_final.reshape(1, -1))


def _state_to_t(s):
    bsz = s.shape[0]
    st = jnp.zeros((bsz, GLA_HEADS, GLA_DV, GLA_HEADS, GLA_DK), F32)
    for hh in range(GLA_HEADS):
        st = st.at[:, hh, :, hh, :].set(jnp.swapaxes(s[:, hh], 1, 2).astype(F32))
    return st.reshape(bsz, GLA_WIDTH, GLA_KW)


def _state_from_t(st):
    bsz = st.shape[0]
    s5 = st.reshape(bsz, GLA_HEADS, GLA_DV, GLA_HEADS, GLA_DK)
    return jnp.stack([jnp.swapaxes(s5[:, hh, :, hh, :], 1, 2) for hh in range(GLA_HEADS)], axis=1)


def _prep_weights(norm_mix, w_in, w_gate, b_gate, gla_gain, pool_w, pool_scale, w_out, norm_mem_q, w_mq, w_mo,
                  norm_moe, w_router, b_router, w_up, b_up, w_down, b_down):
    n_qkvg = 2 * GLA_KW + 2 * GLA_WIDTH
    w_r = jnp.zeros((D_MODEL, LANES), BF16).at[:, :GATE_RANK].set(w_in[:, n_qkvg:n_qkvg + GATE_RANK].astype(BF16))
    w_g = jnp.zeros((LANES, GLA_KW), BF16).at[:GATE_RANK].set(w_gate.astype(BF16))
    w_rt = jnp.zeros((D_MODEL, LANES), BF16).at[:, :N_EXPERTS].set(w_router.astype(BF16))
    b_rt = jnp.full((1, LANES), NEG_BIG, F32).at[0, :N_EXPERTS].set(b_router)
    return {
        "norm_mix": norm_mix.reshape(1, -1),
        "w_qkvg": w_in[:, :n_qkvg].astype(BF16),
        "w_r": w_r,
        "w_u": w_in[:, n_qkvg + GATE_RANK:].astype(BF16),
        "w_gate": w_g,
        "b_gate": b_gate.reshape(1, -1),
        "gla_gain": gla_gain.reshape(1, -1),
        "pool_w": pool_w.astype(BF16),
        "pool_scale": pool_scale.reshape(1, -1),
        "w_out": w_out.astype(BF16),
        "norm_mem_q": norm_mem_q.reshape(1, -1),
        "w_mq": w_mq.astype(BF16),
        "w_mo": w_mo.astype(BF16),
        "norm_moe": norm_moe.reshape(1, -1),
        "w_router": w_rt,
        "b_router": b_rt,
        "w_up": w_up.astype(BF16),
        "b_up": b_up.reshape(N_EXPERTS, 1, -1),
        "w_down": w_down.astype(BF16),
        "b_down": b_down.reshape(N_EXPERTS, 1, -1),
    }


def _block(x, s_gla, pool_prev, hist_len, mk, mv, w, norm_final):
    batch, seq, _ = x.shape
    x2d = x.reshape(batch * seq, D_MODEL)
    pool0 = jnp.concatenate([jnp.zeros((batch, 1, POOL_WIDTH), F32), pool_prev.astype(F32)], axis=1)
    x2, ri, rg, cnt, st_t, pool_t = _premoe(x2d, batch, seq, hist_len, _state_to_t(s_gla), pool0,
                                            mk.reshape(batch, N_MEM, D_MODEL), mv.reshape(batch, N_MEM, D_MODEL), w)
    out = _moe_layer(x2, ri, rg, cnt, w, norm_final)
    return out.reshape(batch, seq, D_MODEL), _state_from_t(st_t), pool_t[:, 1:, :]


def kernel(x_prompt, x_sample, mem_prompt, state_gla, state_pool, cache_mem_k, cache_mem_v, norm_mix, w_in, w_gate, b_gate, gla_gain, pool_w, pool_scale, w_out, norm_mem_q, norm_mem_kv, w_mq, w_mk, w_mv, w_mo, norm_moe, w_router, b_router, w_up, b_up, w_down, b_down, norm_final):
    depth = w_in.shape[0]
    assert depth == 1
    xp, xs = x_prompt, x_sample
    bp = xp.shape[0]
    gla_p, pool_p, mk_p, mv_p, gla_s, pool_s = [], [], [], [], [], []
    for l in range(depth):
        w = _prep_weights(norm_mix[l], w_in[l], w_gate[l], b_gate[l], gla_gain[l], pool_w[l], pool_scale[l], w_out[l],
                          norm_mem_q[l], w_mq[l], w_mo[l], norm_moe[l], w_router[l], b_router[l],
                          w_up[l], b_up[l], w_down[l], b_down[l])
        mk2, mv2 = _mem_kv(mem_prompt.reshape(bp * N_MEM, D_MODEL), norm_mem_kv[l].reshape(1, -1),
                           w_mk[l].astype(BF16), w_mv[l].astype(BF16))
        mk = mk2.reshape(bp, N_MEM, MEM_HEADS, MEM_HD)
        mv = mv2.reshape(bp, N_MEM, MEM_HEADS, MEM_HD)
        xs, ss, ps = _block(xs, state_gla[l], state_pool[l], PAST_LEN, cache_mem_k[l], cache_mem_v[l], w, norm_final)
        gla_s.append(ss)
        pool_s.append(ps)
        s0 = jnp.zeros((bp, GLA_HEADS, GLA_DK, GLA_DV), F32)
        p0 = jnp.zeros((bp, POOL_BUF, POOL_WIDTH), F32)
        xp, sp, pp = _block(xp, s0, p0, 0, mk, mv, w, norm_final)
        gla_p.append(sp)
        pool_p.append(pp)
        mk_p.append(mk)
        mv_p.append(mv)
    return (xp, xs, jnp.stack(gla_p), jnp.stack(pool_p), jnp.stack(mk_p), jnp.stack(mv_p),
            jnp.stack(gla_s), jnp.stack(pool_s))
```

```python
import functools

import numpy as np
import jax
import jax.numpy as jnp
from jax import lax
from jax.experimental import pallas as pl
from jax.experimental.pallas import tpu as pltpu

F32 = jnp.float32
BF16 = jnp.bfloat16
I32 = jnp.int32

D_MODEL = 1024
CHUNK = 64
GLA_HEADS = 4
GLA_DK = 64
GLA_DV = 128
GLA_KW = GLA_HEADS * GLA_DK
GLA_WIDTH = GLA_HEADS * GLA_DV
GATE_RANK = 16
GATE_TAU = 16.0
POOL_WIDTH = 512
POOL_GC = 128
POOL_WINDOWS = (2, 4, 8, 16)
POOL_BUF = 15
PAST_LEN = 4096
POOL_ROWS = 16
N_MEM = 256
MEM_HEADS = 4
MEM_HD = 256
N_EXPERTS = 32
TOP_K = 4
D_FF = 1024
SWIGLU_ALPHA = 1.702
SWIGLU_LIMIT = 7.0
EPS = 1e-6
LANES = 128
NEG_BIG = -1e30

PREMOE_ROWS = 256
MOE_ROWS = 512
DISPATCH_ROWS = 256
COMBINE_ROWS = 256
VMEM_LIMIT = 56 * 1024 * 1024

NT_DIMS = (((1,), (1,)), ((), ()))
TN_DIMS = (((0,), (0,)), ((), ()))


def _rms(x, g):
    ms = jnp.mean(x * x, axis=-1, keepdims=True)
    return x * lax.rsqrt(ms + EPS) * g


def _const_spec(shape):
    nd = len(shape)
    return pl.BlockSpec(shape, lambda *_: (0,) * nd, pipeline_mode=pl.Buffered(1))


def _memkv_body(m_ref, g_ref, wk_ref, wv_ref, k_ref, v_ref):
    m = _rms(m_ref[...], g_ref[...]).astype(BF16)
    k_ref[...] = jnp.dot(m, wk_ref[...], preferred_element_type=F32)
    v_ref[...] = jnp.dot(m, wv_ref[...], preferred_element_type=F32)


def _mem_kv(mem2d, g, wk, wv):
    n = mem2d.shape[0]
    tm = 512
    row = pl.BlockSpec((tm, D_MODEL), lambda i: (i, 0))
    return pl.pallas_call(
        _memkv_body,
        grid=(n // tm,),
        in_specs=[row, _const_spec((1, D_MODEL)), _const_spec((D_MODEL, D_MODEL)), _const_spec((D_MODEL, D_MODEL))],
        out_specs=[row, row],
        out_shape=[jax.ShapeDtypeStruct((n, D_MODEL), F32)] * 2,
        compiler_params=pltpu.CompilerParams(dimension_semantics=("arbitrary",), vmem_limit_bytes=VMEM_LIMIT),
        name="mem_kv",
    )(mem2d, g, wk, wv)


def _premoe_body(hist, rows,
                 x_ref, s0_ref, p0_ref, k_ref, v_ref,
                 nmix_ref, wqkvg_ref, wr_ref, wu_ref, wgate_ref, bgate_ref, ggain_ref, poolw_ref, pscale_ref,
                 wout_ref, nq_ref, wmq_ref, wmo_ref, nmoe_ref, wrt_ref, brt_ref,
                 tri2_ref, ltri_ref, band_ref,
                 x2_ref, ri_ref, rg_ref, cnt_ref, sto_ref, po_ref,
                 st_sc, prev_sc, cnt_sc):
    b = pl.program_id(0)
    l = pl.program_id(1)
    n_chunks = rows // CHUNK

    @pl.when(l == 0)
    def _():
        st_sc[...] = s0_ref[0]
        prow = lax.broadcasted_iota(I32, (POOL_ROWS, POOL_WIDTH), 0)
        prev_sc[...] = jnp.where(prow >= POOL_ROWS - hist, p0_ref[0], 0.0)

    @pl.when((b == 0) & (l == 0))
    def _():
        cnt_sc[...] = jnp.zeros_like(cnt_sc)

    x = x_ref[...]
    h = _rms(x, nmix_ref[...]).astype(BF16)
    z = jnp.dot(h, wqkvg_ref[...], preferred_element_type=F32)
    q = z[:, 0:GLA_KW] * (GLA_DK ** -0.5)
    k = z[:, GLA_KW:2 * GLA_KW]
    v = z[:, 2 * GLA_KW:2 * GLA_KW + GLA_WIDTH]
    g = z[:, 2 * GLA_KW + GLA_WIDTH:]

    r = jnp.dot(h, wr_ref[...], preferred_element_type=F32)
    gp = jnp.dot(r.astype(BF16), wgate_ref[...], preferred_element_type=F32) + bgate_ref[...]
    la = jax.nn.log_sigmoid(gp) * (1.0 / GATE_TAU)

    hi = la.astype(BF16)
    r1 = la - hi.astype(F32)
    mid = r1.astype(BF16)
    lo = (r1 - mid.astype(F32)).astype(BF16)
    la3 = jnp.concatenate([hi, mid, lo], axis=1)
    bb = jnp.dot(tri2_ref[...], la3, preferred_element_type=F32)
    bsum = (bb[:, 0:GLA_KW] + bb[:, GLA_KW:2 * GLA_KW]) + bb[:, 2 * GLA_KW:]
    bcum = bsum[:rows]
    btot = bsum[rows:]

    qd = q * jnp.exp(bcum)
    kd = k * jnp.exp(-bcum)
    kl = k * jnp.exp(btot - bcum)

    lane_kw = lax.broadcasted_iota(I32, (rows, GLA_KW), 1)
    rowi = lax.broadcasted_iota(I32, (rows, rows), 0)
    coli = lax.broadcasted_iota(I32, (rows, rows), 1)
    amask = (coli <= rowi) & (coli >= (rowi & ~(CHUNK - 1)))
    vb = v.astype(BF16)

    o_parts = []
    for hh in range(GLA_HEADS):
        mh = (lane_kw >> 6) == hh
        qh = jnp.where(mh, qd, 0.0).astype(BF16)
        kh = jnp.where(mh, kd, 0.0).astype(BF16)
        a = lax.dot_general(qh, kh, NT_DIMS, preferred_element_type=F32)
        a = jnp.where(amask, a, 0.0).astype(BF16)
        o_parts.append(jnp.dot(a, vb[:, hh * GLA_DV:(hh + 1) * GLA_DV], preferred_element_type=F32))
    o_intra = jnp.concatenate(o_parts, axis=1)

    srow = lax.broadcasted_iota(I32, (GLA_WIDTH, GLA_KW), 0)
    scol = lax.broadcasted_iota(I32, (GLA_WIDTH, GLA_KW), 1)
    smask = (srow >> 7) == (scol >> 6)
    qdb = qd.astype(BF16)
    klb = kl.astype(BF16)
    st = st_sc[...]
    oi_parts = []
    for n in range(n_chunks):
        lo_r, hi_r = n * CHUNK, (n + 1) * CHUNK
        oi_parts.append(lax.dot_general(qdb[lo_r:hi_r], st.astype(BF16), NT_DIMS, preferred_element_type=F32))
        upd = lax.dot_general(vb[lo_r:hi_r], klb[lo_r:hi_r], TN_DIMS, preferred_element_type=F32)
        dec = jnp.exp(btot[lo_r:lo_r + 1, :])
        st = st * dec + jnp.where(smask, upd, 0.0)
    st_sc[...] = st
    sto_ref[0] = st
    o_inter = oi_parts[0] if n_chunks == 1 else jnp.concatenate(oi_parts, axis=0)
    o = o_intra + o_inter

    gains = ggain_ref[...]
    on_parts = []
    for hh in range(GLA_HEADS):
        oh = o[:, hh * GLA_DV:(hh + 1) * GLA_DV]
        ms = jnp.mean(oh * oh, axis=-1, keepdims=True)
        on_parts.append(oh * lax.rsqrt(ms + EPS) * gains[:, hh * GLA_DV:(hh + 1) * GLA_DV])
    on = jnp.concatenate(on_parts, axis=1) * (g * jax.nn.sigmoid(g))

    u = jnp.dot(h, wu_ref[...], preferred_element_type=F32)
    ext = jnp.concatenate([prev_sc[...], u], axis=0).astype(BF16)
    pos = l * rows + lax.broadcasted_iota(I32, (rows, POOL_GC), 0)
    p_parts = []
    for gi, w in enumerate(POOL_WINDOWS):
        cols = slice(gi * POOL_GC, (gi + 1) * POOL_GC)
        s = jnp.dot(band_ref[gi], ext[:, cols], preferred_element_type=F32)
        cnt_w = jnp.minimum(w, pos + 1 + hist).astype(F32)
        dd = s / cnt_w - u[:, cols]
        p_parts.append(jnp.dot(dd.astype(BF16), poolw_ref[gi], preferred_element_type=F32))
    p = jnp.concatenate(p_parts, axis=1) * pscale_ref[...]
    tail = u[rows - POOL_ROWS:rows, :]
    prev_sc[...] = tail
    po_ref[0] = tail

    cat = jnp.concatenate([on, p], axis=1).astype(BF16)
    x1 = x + jnp.dot(cat, wout_ref[...], preferred_element_type=F32)

    h2 = _rms(x1, nq_ref[...]).astype(BF16)
    qm = jnp.dot(h2, wmq_ref[...], preferred_element_type=F32).astype(BF16)
    kk = k_ref[0].astype(BF16)
    vv = v_ref[0].astype(BF16)
    a_parts = []
    for hh in range(MEM_HEADS):
        cols = slice(hh * MEM_HD, (hh + 1) * MEM_HD)
        s = lax.dot_general(qm[:, cols], kk[:, cols], NT_DIMS, preferred_element_type=F32) * (MEM_HD ** -0.5)
        e = jnp.exp(s - jnp.max(s, axis=-1, keepdims=True))
        pr = e / jnp.sum(e, axis=-1, keepdims=True)
        a_parts.append(jnp.dot(pr.astype(BF16), vv[:, cols], preferred_element_type=F32))
    att = jnp.concatenate(a_parts, axis=1).astype(BF16)
    x2 = x1 + jnp.dot(att, wmo_ref[...], preferred_element_type=F32)
    x2_ref[...] = x2

    h3 = _rms(x2, nmoe_ref[...]).astype(BF16)
    logits = jnp.dot(h3, wrt_ref[...], preferred_element_type=F32) + brt_ref[...]
    lane = lax.broadcasted_iota(I32, (rows, LANES), 1)
    lane_f = lane.astype(F32)
    work = logits
    vals, idxs = [], []
    for _ in range(TOP_K):
        m = jnp.max(work, axis=-1, keepdims=True)
        idx = jnp.min(jnp.where(work == m, lane_f, float(LANES)), axis=-1, keepdims=True)
        vals.append(m)
        idxs.append(idx)
        work = jnp.where(lane_f == idx, -jnp.inf, work)
    exps = [jnp.exp(vk - vals[0]) for vk in vals]
    den = (exps[0] + exps[1]) + (exps[2] + exps[3])
    gates = [ek / den for ek in exps]

    hot = jnp.zeros((rows, LANES), F32)
    for idx in idxs:
        hot = hot + jnp.where(lane_f == idx, 1.0, 0.0)
    prefix = jnp.dot(ltri_ref[...], hot.astype(BF16), preferred_element_type=F32) + cnt_sc[0:1, :]
    ri = jnp.zeros((rows, LANES), F32)
    rg = jnp.zeros((rows, LANES), F32)
    for kk_, idx in enumerate(idxs):
        rank = jnp.sum(jnp.where(lane_f == idx, prefix, 0.0), axis=-1, keepdims=True)
        ri = jnp.where(lane == kk_, idx, ri)
        ri = jnp.where(lane == TOP_K + kk_, rank, ri)
        rg = jnp.where(lane == kk_, gates[kk_], rg)
    ri_ref[...] = ri.astype(I32)
    rg_ref[...] = rg
    new_cnt = cnt_sc[0:1, :] + jnp.sum(hot, axis=0, keepdims=True)
    cnt_sc[...] = jnp.broadcast_to(new_cnt, cnt_sc.shape)
    cnt_ref[...] = jnp.broadcast_to(new_cnt, cnt_ref.shape)


def _premoe_consts(rows):
    i = np.arange(rows)[:, None]
    j = np.arange(rows)[None, :]
    same = (i // CHUNK) == (j // CHUNK)
    tri = (same & (j <= i)).astype(np.float32)
    tot = same.astype(np.float32)
    tri2 = np.concatenate([tri, tot], axis=0)
    ltri = (j < i).astype(np.float32)
    je = np.arange(rows + POOL_ROWS)[None, :]
    band = np.stack([((je <= i + POOL_ROWS) & (je > i + POOL_ROWS - w)).astype(np.float32) for w in POOL_WINDOWS])
    return jnp.asarray(tri2, BF16), jnp.asarray(ltri, BF16), jnp.asarray(band, BF16)


def _premoe(x2d, batch, seq, hist_len, s0t, pool0, kmem, vmem, w):
    rows = min(PREMOE_ROWS, seq)
    nl = seq // rows
    t = batch * seq
    hist = min(int(hist_len), POOL_BUF)
    tri2, ltri, band = _premoe_consts(rows)
    row_spec = pl.BlockSpec((rows, D_MODEL), lambda b, l: (b * nl + l, 0))
    lane_spec = pl.BlockSpec((rows, LANES), lambda b, l: (b * nl + l, 0))

    def per_batch(shape):
        return pl.BlockSpec((1,) + shape, lambda b, l: (b, 0, 0))

    consts = [w["norm_mix"], w["w_qkvg"], w["w_r"], w["w_u"], w["w_gate"], w["b_gate"], w["gla_gain"], w["pool_w"],
              w["pool_scale"], w["w_out"], w["norm_mem_q"], w["w_mq"], w["w_mo"], w["norm_moe"], w["w_router"],
              w["b_router"], tri2, ltri, band]
    in_specs = [row_spec, per_batch((GLA_WIDTH, GLA_KW)), per_batch((POOL_ROWS, POOL_WIDTH)),
                per_batch((N_MEM, D_MODEL)), per_batch((N_MEM, D_MODEL))] + [_const_spec(c.shape) for c in consts]
    out_specs = [row_spec, lane_spec, lane_spec, pl.BlockSpec((8, LANES), lambda b, l: (0, 0)),
                 per_batch((GLA_WIDTH, GLA_KW)), per_batch((POOL_ROWS, POOL_WIDTH))]
    out_shape = [jax.ShapeDtypeStruct((t, D_MODEL), F32), jax.ShapeDtypeStruct((t, LANES), I32),
                 jax.ShapeDtypeStruct((t, LANES), F32), jax.ShapeDtypeStruct((8, LANES), F32),
                 jax.ShapeDtypeStruct((batch, GLA_WIDTH, GLA_KW), F32),
                 jax.ShapeDtypeStruct((batch, POOL_ROWS, POOL_WIDTH), F32)]
    return pl.pallas_call(
        functools.partial(_premoe_body, hist, rows),
        grid=(batch, nl),
        in_specs=in_specs,
        out_specs=out_specs,
        out_shape=out_shape,
        scratch_shapes=[pltpu.VMEM((GLA_WIDTH, GLA_KW), F32), pltpu.VMEM((POOL_ROWS, POOL_WIDTH), F32),
                        pltpu.VMEM((8, LANES), F32)],
        compiler_params=pltpu.CompilerParams(dimension_semantics=("arbitrary", "arbitrary"),
                                             vmem_limit_bytes=VMEM_LIMIT),
        name="premoe",
    )(x2d, s0t, pool0, kmem, vmem, *consts)


def _route_tables(route_i, counts_f, tm, nb):
    e = route_i[:, :TOP_K]
    rank = route_i[:, TOP_K:2 * TOP_K]
    counts = counts_f[0, :N_EXPERTS].astype(I32)
    padded = ((counts + tm - 1) // tm) * tm
    pend = jnp.cumsum(padded)
    pstart = pend - padded
    eids = jnp.arange(N_EXPERTS, dtype=I32)
    dest = jnp.sum(jnp.where(e[:, :, None] == eids, pstart, 0), axis=-1) + rank
    blk_start = jnp.arange(nb, dtype=I32) * tm
    block_e = jnp.minimum(jnp.sum((pend[None, :] <= blk_start[:, None]).astype(I32), axis=1), N_EXPERTS - 1)
    n_used = (pend[-1] // tm).astype(I32).reshape(1)
    return dest.astype(I32), block_e.astype(I32), n_used, pend.astype(I32), counts


def _dispatch_body(rows, tm, nb, nu_ref, pend_ref, cnt_ref, dest_ref, x_ref, g_ref, xg_hbm, hbuf, zbuf, sem, zsem):
    i = pl.program_id(0)
    n_steps = pl.num_programs(0)
    slot = lax.rem(i, 2)
    nu = nu_ref[0]
    groups = rows // 8

    def zero_copy(start):
        return pltpu.make_async_copy(zbuf, xg_hbm.at[pl.ds(start // 8, tm // 8)], zsem)

    def for_each_fill(fn):
        def per_expert(e, c):
            @pl.when(cnt_ref[e] > 0)
            def _():
                fn(pend_ref[e] - tm)
            return c
        lax.fori_loop(0, N_EXPERTS, per_expert, 0)

        def per_tail(b, c):
            fn(b * tm)
            return c
        lax.fori_loop(nu, nb, per_tail, 0)

    @pl.when(i == 0)
    def _():
        zbuf[...] = jnp.zeros_like(zbuf)
        for_each_fill(lambda start: zero_copy(start).start())
        for_each_fill(lambda start: zero_copy(start).wait())

    def row_wait(s):
        for _ in range(TOP_K):
            pltpu.make_async_copy(hbuf.at[s], xg_hbm.at[pl.ds(0, groups)], sem.at[s]).wait()

    @pl.when(i >= 2)
    def _():
        row_wait(slot)

    hbuf[slot] = _rms(x_ref[...], g_ref[...]).reshape(groups, 8, D_MODEL)

    def send_group(gi, c):
        base = gi * (8 * TOP_K)
        for r in range(8):
            for kk in range(TOP_K):
                dst = dest_ref[0, 0, base + r * TOP_K + kk]
                pltpu.make_async_copy(hbuf.at[slot, gi, pl.ds(r, 1)], xg_hbm.at[dst >> 3, pl.ds(dst & 7, 1)],
                                      sem.at[slot]).start(priority=kk % 2)
        return c
    lax.fori_loop(0, groups, send_group, 0)

    @pl.when(i == n_steps - 1)
    def _():
        row_wait(slot)

        @pl.when(i >= 1)
        def _():
            row_wait(1 - slot)


def _dispatch(x2, dest, n_used, pend, counts, g, tm, nb):
    t = x2.shape[0]
    rows = min(DISPATCH_ROWS, t)
    steps = t // rows
    dest3 = dest.reshape(steps, 1, rows * TOP_K)
    grid_spec = pltpu.PrefetchScalarGridSpec(
        num_scalar_prefetch=3,
        grid=(steps,),
        in_specs=[
            pl.BlockSpec((1, 1, rows * TOP_K), lambda i, *_: (i, 0, 0), memory_space=pltpu.SMEM),
            pl.BlockSpec((rows, D_MODEL), lambda i, *_: (i, 0)),
            pl.BlockSpec((1, D_MODEL), lambda i, *_: (0, 0)),
        ],
        out_specs=pl.BlockSpec(memory_space=pl.ANY),
        scratch_shapes=[pltpu.VMEM((2, rows // 8, 8, D_MODEL), F32), pltpu.VMEM((tm // 8, 8, D_MODEL), F32),
                        pltpu.SemaphoreType.DMA((2,)), pltpu.SemaphoreType.DMA(())],
    )
    return pl.pallas_call(
        functools.partial(_dispatch_body, rows, tm, nb),
        grid_spec=grid_spec,
        out_shape=jax.ShapeDtypeStruct((nb * tm // 8, 8, D_MODEL), F32),
        compiler_params=pltpu.CompilerParams(dimension_semantics=("arbitrary",), vmem_limit_bytes=VMEM_LIMIT),
        name="dispatch",
    )(n_used, pend, counts, dest3, x2, g)


def _moe_body(nu_ref, be_ref, xg_ref, wup_ref, bup_ref, wdn_ref, bdn_ref, ys_ref):
    del be_ref
    b = pl.program_id(0)

    @pl.when(b < nu_ref[0])
    def _():
        hh = xg_ref[...].astype(BF16)
        uu = jnp.dot(hh, wup_ref[0], preferred_element_type=F32) + bup_ref[0]
        glu = jnp.minimum(uu[:, :D_FF], SWIGLU_LIMIT)
        lin = jnp.clip(uu[:, D_FF:], -SWIGLU_LIMIT, SWIGLU_LIMIT)
        act = glu * jax.nn.sigmoid(SWIGLU_ALPHA * glu) * (lin + 1.0)
        ys_ref[...] = jnp.dot(act.astype(BF16), wdn_ref[0], preferred_element_type=F32) + bdn_ref[0]

    @pl.when(b >= nu_ref[0])
    def _():
        ys_ref[...] = jnp.zeros_like(ys_ref)


def _moe(xg, block_e, n_used, w, tm, nb):
    def used(b, nu):
        return jnp.minimum(b, nu[0] - 1)

    def per_expert(shape):
        return pl.BlockSpec((1,) + shape, lambda b, nu, be: (be[used(b, nu)], 0, 0))

    grid_spec = pltpu.PrefetchScalarGridSpec(
        num_scalar_prefetch=2,
        grid=(nb,),
        in_specs=[
            pl.BlockSpec((tm, D_MODEL), lambda b, nu, be: (used(b, nu), 0)),
            per_expert((D_MODEL, 2 * D_FF)), per_expert((1, 2 * D_FF)),
            per_expert((D_FF, D_MODEL)), per_expert((1, D_MODEL)),
        ],
        out_specs=pl.BlockSpec((tm, D_MODEL), lambda b, nu, be: (b, 0)),
    )
    return pl.pallas_call(
        _moe_body,
        grid_spec=grid_spec,
        out_shape=jax.ShapeDtypeStruct((nb * tm, D_MODEL), F32),
        compiler_params=pltpu.CompilerParams(dimension_semantics=("arbitrary",), vmem_limit_bytes=VMEM_LIMIT),
        name="moe",
    )(n_used, block_e, xg, w["w_up"], w["b_up"], w["w_down"], w["b_down"])


def _combine_body(rows, destc_ref, destn_ref, x_ref, gate_ref, g_ref, ys_hbm, o_ref, gbuf, sem):
    i = pl.program_id(0)
    n_steps = pl.num_programs(0)
    slot = lax.rem(i, 2)
    groups = rows // 8

    def fetch(dest_ref, s):
        def fetch_group(gi, c):
            base = gi * (8 * TOP_K)
            for r in range(8):
                for kk in range(TOP_K):
                    src = dest_ref[0, 0, base + r * TOP_K + kk]
                    pltpu.make_async_copy(ys_hbm.at[src >> 3, pl.ds(src & 7, 1)], gbuf.at[s, kk, gi, pl.ds(r, 1)],
                                          sem.at[s]).start(priority=kk % 2)
            return c
        lax.fori_loop(0, groups, fetch_group, 0)

    @pl.when(i == 0)
    def _():
        fetch(destc_ref, 0)

    @pl.when(i + 1 < n_steps)
    def _():
        fetch(destn_ref, 1 - slot)

    for kk in range(TOP_K):
        pltpu.make_async_copy(ys_hbm.at[pl.ds(0, groups)], gbuf.at[slot, kk], sem.at[slot]).wait()

    gate = gate_ref[...]
    acc = x_ref[...]
    for kk in range(TOP_K):
        acc = acc + gbuf[slot, kk].reshape(rows, D_MODEL) * gate[:, kk:kk + 1]
    o_ref[...] = _rms(acc, g_ref[...])


def _combine(x2, ys, dest, route_g, g):
    t = x2.shape[0]
    rows = min(COMBINE_ROWS, t)
    steps = t // rows
    dest3 = dest.reshape(steps, 1, rows * TOP_K)
    smem_cur = pl.BlockSpec((1, 1, rows * TOP_K), lambda i: (i, 0, 0), memory_space=pltpu.SMEM)
    smem_next = pl.BlockSpec((1, 1, rows * TOP_K), lambda i: (jnp.minimum(i + 1, steps - 1), 0, 0),
                             memory_space=pltpu.SMEM)
    return pl.pallas_call(
        functools.partial(_combine_body, rows),
        grid=(steps,),
        in_specs=[smem_cur, smem_next,
                  pl.BlockSpec((rows, D_MODEL), lambda i: (i, 0)),
                  pl.BlockSpec((rows, LANES), lambda i: (i, 0)),
                  pl.BlockSpec((1, D_MODEL), lambda i: (0, 0)),
                  pl.BlockSpec(memory_space=pl.ANY)],
        out_specs=pl.BlockSpec((rows, D_MODEL), lambda i: (i, 0)),
        out_shape=jax.ShapeDtypeStruct((t, D_MODEL), F32),
        scratch_shapes=[pltpu.VMEM((2, TOP_K, rows // 8, 8, D_MODEL), F32), pltpu.SemaphoreType.DMA((2,))],
        compiler_params=pltpu.CompilerParams(dimension_semantics=("arbitrary",), vmem_limit_bytes=VMEM_LIMIT),
        name="combine",
    )(dest3, dest3, x2, route_g, g, ys)


def _moe_layer(x2, route_i, route_g, counts_f, w, norm_final):
    t = x2.shape[0]
    tm = MOE_ROWS
    nb = (t * TOP_K + N_EXPERTS * (tm - 1) + tm - 1) // tm
    dest, block_e, n_used, pend, counts = _route_tables(route_i, counts_f, tm, nb)
    xg = _dispatch(x2, dest, n_used, pend, counts, w["norm_moe"], tm, nb)
    ys = _moe(xg.reshape(nb * tm, D_MODEL), block_e, n_used, w, tm, nb)
    return _combine(x2, ys.reshape(nb * tm // 8, 8, D_MODEL), dest, route_g, norm_final.reshape(1, -1))


def _state_to_t(s):
    bsz = s.shape[0]
    st = jnp.zeros((bsz, GLA_HEADS, GLA_DV, GLA_HEADS, GLA_DK), F32)
    for hh in range(GLA_HEADS):
        st = st.at[:, hh, :, hh, :].set(jnp.swapaxes(s[:, hh], 1, 2).astype(F32))
    return st.reshape(bsz, GLA_WIDTH, GLA_KW)


def _state_from_t(st):
    bsz = st.shape[0]
    s5 = st.reshape(bsz, GLA_HEADS, GLA_DV, GLA_HEADS, GLA_DK)
    return jnp.stack([jnp.swapaxes(s5[:, hh, :, hh, :], 1, 2) for hh in range(GLA_HEADS)], axis=1)


def _prep_weights(norm_mix, w_in, w_gate, b_gate, gla_gain, pool_w, pool_scale, w_out, norm_mem_q, w_mq, w_mo,
                  norm_moe, w_router, b_router, w_up, b_up, w_down, b_down):
    n_qkvg = 2 * GLA_KW + 2 * GLA_WIDTH
    w_r = jnp.zeros((D_MODEL, LANES), BF16).at[:, :GATE_RANK].set(w_in[:, n_qkvg:n_qkvg + GATE_RANK].astype(BF16))
    w_g = jnp.zeros((LANES, GLA_KW), BF16).at[:GATE_RANK].set(w_gate.astype(BF16))
    w_rt = jnp.zeros((D_MODEL, LANES), BF16).at[:, :N_EXPERTS].set(w_router.astype(BF16))
    b_rt = jnp.full((1, LANES), NEG_BIG, F32).at[0, :N_EXPERTS].set(b_router)
    return {
        "norm_mix": norm_mix.reshape(1, -1),
        "w_qkvg": w_in[:, :n_qkvg].astype(BF16),
        "w_r": w_r,
        "w_u": w_in[:, n_qkvg + GATE_RANK:].astype(BF16),
        "w_gate": w_g,
        "b_gate": b_gate.reshape(1, -1),
        "gla_gain": gla_gain.reshape(1, -1),
        "pool_w": pool_w.astype(BF16),
        "pool_scale": pool_scale.reshape(1, -1),
        "w_out": w_out.astype(BF16),
        "norm_mem_q": norm_mem_q.reshape(1, -1),
        "w_mq": w_mq.astype(BF16),
        "w_mo": w_mo.astype(BF16),
        "norm_moe": norm_moe.reshape(1, -1),
        "w_router": w_rt,
        "b_router": b_rt,
        "w_up": w_up.astype(BF16),
        "b_up": b_up.reshape(N_EXPERTS, 1, -1),
        "w_down": w_down.astype(BF16),
        "b_down": b_down.reshape(N_EXPERTS, 1, -1),
    }


def _block(x, s_gla, pool_prev, hist_len, mk, mv, w, norm_final):
    batch, seq, _ = x.shape
    x2d = x.reshape(batch * seq, D_MODEL)
    pool0 = jnp.concatenate([jnp.zeros((batch, 1, POOL_WIDTH), F32), pool_prev.astype(F32)], axis=1)
    x2, ri, rg, cnt, st_t, pool_t = _premoe(x2d, batch, seq, hist_len, _state_to_t(s_gla), pool0,
                                            mk.reshape(batch, N_MEM, D_MODEL), mv.reshape(batch, N_MEM, D_MODEL), w)
    out = _moe_layer(x2, ri, rg, cnt, w, norm_final)
    return out.reshape(batch, seq, D_MODEL), _state_from_t(st_t), pool_t[:, 1:, :]


def kernel(x_prompt, x_sample, mem_prompt, state_gla, state_pool, cache_mem_k, cache_mem_v, norm_mix, w_in, w_gate, b_gate, gla_gain, pool_w, pool_scale, w_out, norm_mem_q, norm_mem_kv, w_mq, w_mk, w_mv, w_mo, norm_moe, w_router, b_router, w_up, b_up, w_down, b_down, norm_final):
    depth = w_in.shape[0]
    assert depth == 1
    xp, xs = x_prompt, x_sample
    bp = xp.shape[0]
    gla_p, pool_p, mk_p, mv_p, gla_s, pool_s = [], [], [], [], [], []
    for l in range(depth):
        w = _prep_weights(norm_mix[l], w_in[l], w_gate[l], b_gate[l], gla_gain[l], pool_w[l], pool_scale[l], w_out[l],
                          norm_mem_q[l], w_mq[l], w_mo[l], norm_moe[l], w_router[l], b_router[l],
                          w_up[l], b_up[l], w_down[l], b_down[l])
        mk2, mv2 = _mem_kv(mem_prompt.reshape(bp * N_MEM, D_MODEL), norm_mem_kv[l].reshape(1, -1),
                           w_mk[l].astype(BF16), w_mv[l].astype(BF16))
        mk = mk2.reshape(bp, N_MEM, MEM_HEADS, MEM_HD)
        mv = mv2.reshape(bp, N_MEM, MEM_HEADS, MEM_HD)
        xs, ss, ps = _block(xs, state_gla[l], state_pool[l], PAST_LEN, cache_mem_k[l], cache_mem_v[l], w, norm_final)
        gla_s.append(ss)
        pool_s.append(ps)
        s0 = jnp.zeros((bp, GLA_HEADS, GLA_DK, GLA_DV), F32)
        p0 = jnp.zeros((bp, POOL_BUF, POOL_WIDTH), F32)
        xp, sp, pp = _block(xp, s0, p0, 0, mk, mv, w, norm_final)
        gla_p.append(sp)
        pool_p.append(pp)
        mk_p.append(mk)
        mv_p.append(mv)
    return (xp, xs, jnp.stack(gla_p), jnp.stack(pool_p), jnp.stack(mk_p), jnp.stack(mv_p),
            jnp.stack(gla_s), jnp.stack(pool_s))
```

```python
import functools

import numpy as np
import jax
import jax.numpy as jnp
from jax import lax
from jax.experimental import pallas as pl
from jax.experimental.pallas import tpu as pltpu

F32 = jnp.float32
BF16 = jnp.bfloat16
I32 = jnp.int32

D_MODEL = 1024
CHUNK = 64
GLA_HEADS = 4
GLA_DK = 64
GLA_DV = 128
GLA_KW = GLA_HEADS * GLA_DK
GLA_WIDTH = GLA_HEADS * GLA_DV
GATE_RANK = 16
GATE_TAU = 16.0
POOL_WIDTH = 512
POOL_GC = 128
POOL_WINDOWS = (2, 4, 8, 16)
POOL_BUF = 15
PAST_LEN = 4096
POOL_ROWS = 16
N_MEM = 256
MEM_HEADS = 4
MEM_HD = 256
N_EXPERTS = 32
TOP_K = 4
D_FF = 1024
SWIGLU_ALPHA = 1.702
SWIGLU_LIMIT = 7.0
EPS = 1e-6
LANES = 128
TILE_ROWS = D_MODEL // LANES
NEG_BIG = -1e30

PREMOE_ROWS = 256
MOE_ROWS = 512
DISPATCH_ROWS = 256
COMBINE_ROWS = 256
VMEM_LIMIT = 56 * 1024 * 1024

NT_DIMS = (((1,), (1,)), ((), ()))
TN_DIMS = (((0,), (0,)), ((), ()))


def _rms(x, g):
    ms = jnp.mean(x * x, axis=-1, keepdims=True)
    return x * lax.rsqrt(ms + EPS) * g


def _load_token_tiles(ref, rows):
    return jnp.concatenate([ref[pl.ds(s, rows, stride=TILE_ROWS), :] for s in range(TILE_ROWS)], axis=1)


def _plain_rows(v):
    g = v.shape[0]
    return jnp.concatenate([v[:, c].reshape(g * 8, LANES) for c in range(TILE_ROWS)], axis=1)


def _store_token_tiles(ref, val):
    rows = val.shape[0]
    for s in range(TILE_ROWS):
        ref[pl.ds(s, rows, stride=TILE_ROWS), :] = val[:, s * LANES:(s + 1) * LANES]


def _const_spec(shape):
    nd = len(shape)
    return pl.BlockSpec(shape, lambda *_: (0,) * nd, pipeline_mode=pl.Buffered(1))


def _memkv_body(m_ref, g_ref, wk_ref, wv_ref, k_ref, v_ref):
    m = _rms(m_ref[...], g_ref[...]).astype(BF16)
    k_ref[...] = jnp.dot(m, wk_ref[...], preferred_element_type=F32)
    v_ref[...] = jnp.dot(m, wv_ref[...], preferred_element_type=F32)


def _mem_kv(mem2d, g, wk, wv):
    n = mem2d.shape[0]
    tm = 512
    row = pl.BlockSpec((tm, D_MODEL), lambda i: (i, 0))
    return pl.pallas_call(
        _memkv_body,
        grid=(n // tm,),
        in_specs=[row, _const_spec((1, D_MODEL)), _const_spec((D_MODEL, D_MODEL)), _const_spec((D_MODEL, D_MODEL))],
        out_specs=[row, row],
        out_shape=[jax.ShapeDtypeStruct((n, D_MODEL), F32)] * 2,
        compiler_params=pltpu.CompilerParams(dimension_semantics=("arbitrary",), vmem_limit_bytes=VMEM_LIMIT),
        name="mem_kv",
    )(mem2d, g, wk, wv)


def _premoe_body(hist, rows,
                 x_ref, s0_ref, p0_ref, k_ref, v_ref,
                 nmix_ref, wqkvg_ref, wr_ref, wu_ref, wgate_ref, bgate_ref, ggain_ref, poolw_ref, pscale_ref,
                 wout_ref, nq_ref, wmq_ref, wmo_ref, nmoe_ref, wrt_ref, brt_ref,
                 tri2_ref, ltri_ref, band_ref,
                 x2_ref, h3t_ref, ri_ref, rg_ref, cnt_ref, sto_ref, po_ref,
                 st_sc, prev_sc, cnt_sc):
    b = pl.program_id(0)
    l = pl.program_id(1)
    n_chunks = rows // CHUNK

    @pl.when(l == 0)
    def _():
        st_sc[...] = s0_ref[0]
        prow = lax.broadcasted_iota(I32, (POOL_ROWS, POOL_WIDTH), 0)
        prev_sc[...] = jnp.where(prow >= POOL_ROWS - hist, p0_ref[0], 0.0)

    @pl.when((b == 0) & (l == 0))
    def _():
        cnt_sc[...] = jnp.zeros_like(cnt_sc)

    x = x_ref[...]
    h = _rms(x, nmix_ref[...]).astype(BF16)
    z = jnp.dot(h, wqkvg_ref[...], preferred_element_type=F32)
    q = z[:, 0:GLA_KW] * (GLA_DK ** -0.5)
    k = z[:, GLA_KW:2 * GLA_KW]
    v = z[:, 2 * GLA_KW:2 * GLA_KW + GLA_WIDTH]
    g = z[:, 2 * GLA_KW + GLA_WIDTH:]

    r = jnp.dot(h, wr_ref[...], preferred_element_type=F32)
    gp = jnp.dot(r.astype(BF16), wgate_ref[...], preferred_element_type=F32) + bgate_ref[...]
    la = jax.nn.log_sigmoid(gp) * (1.0 / GATE_TAU)

    hi = la.astype(BF16)
    r1 = la - hi.astype(F32)
    mid = r1.astype(BF16)
    lo = (r1 - mid.astype(F32)).astype(BF16)
    la3 = jnp.concatenate([hi, mid, lo], axis=1)
    bb = jnp.dot(tri2_ref[...], la3, preferred_element_type=F32)
    bsum = (bb[:, 0:GLA_KW] + bb[:, GLA_KW:2 * GLA_KW]) + bb[:, 2 * GLA_KW:]
    bcum = bsum[:rows]
    btot = bsum[rows:]

    qd = q * jnp.exp(bcum)
    kd = k * jnp.exp(-bcum)
    kl = k * jnp.exp(btot - bcum)

    lane_kw = lax.broadcasted_iota(I32, (rows, GLA_KW), 1)
    rowi = lax.broadcasted_iota(I32, (rows, rows), 0)
    coli = lax.broadcasted_iota(I32, (rows, rows), 1)
    amask = (coli <= rowi) & (coli >= (rowi & ~(CHUNK - 1)))
    vb = v.astype(BF16)

    o_parts = []
    for hh in range(GLA_HEADS):
        mh = (lane_kw >> 6) == hh
        qh = jnp.where(mh, qd, 0.0).astype(BF16)
        kh = jnp.where(mh, kd, 0.0).astype(BF16)
        a = lax.dot_general(qh, kh, NT_DIMS, preferred_element_type=F32)
        a = jnp.where(amask, a, 0.0).astype(BF16)
        o_parts.append(jnp.dot(a, vb[:, hh * GLA_DV:(hh + 1) * GLA_DV], preferred_element_type=F32))
    o_intra = jnp.concatenate(o_parts, axis=1)

    srow = lax.broadcasted_iota(I32, (GLA_WIDTH, GLA_KW), 0)
    scol = lax.broadcasted_iota(I32, (GLA_WIDTH, GLA_KW), 1)
    smask = (srow >> 7) == (scol >> 6)
    qdb = qd.astype(BF16)
    klb = kl.astype(BF16)
    st = st_sc[...]
    oi_parts = []
    for n in range(n_chunks):
        lo_r, hi_r = n * CHUNK, (n + 1) * CHUNK
        oi_parts.append(lax.dot_general(qdb[lo_r:hi_r], st.astype(BF16), NT_DIMS, preferred_element_type=F32))
        upd = lax.dot_general(vb[lo_r:hi_r], klb[lo_r:hi_r], TN_DIMS, preferred_element_type=F32)
        dec = jnp.exp(btot[lo_r:lo_r + 1, :])
        st = st * dec + jnp.where(smask, upd, 0.0)
    st_sc[...] = st
    sto_ref[0] = st
    o_inter = oi_parts[0] if n_chunks == 1 else jnp.concatenate(oi_parts, axis=0)
    o = o_intra + o_inter

    gains = ggain_ref[...]
    on_parts = []
    for hh in range(GLA_HEADS):
        oh = o[:, hh * GLA_DV:(hh + 1) * GLA_DV]
        ms = jnp.mean(oh * oh, axis=-1, keepdims=True)
        on_parts.append(oh * lax.rsqrt(ms + EPS) * gains[:, hh * GLA_DV:(hh + 1) * GLA_DV])
    on = jnp.concatenate(on_parts, axis=1) * (g * jax.nn.sigmoid(g))

    u = jnp.dot(h, wu_ref[...], preferred_element_type=F32)
    ext = jnp.concatenate([prev_sc[...], u], axis=0).astype(BF16)
    pos = l * rows + lax.broadcasted_iota(I32, (rows, POOL_GC), 0)
    p_parts = []
    for gi, w in enumerate(POOL_WINDOWS):
        cols = slice(gi * POOL_GC, (gi + 1) * POOL_GC)
        s = jnp.dot(band_ref[gi], ext[:, cols], preferred_element_type=F32)
        cnt_w = jnp.minimum(w, pos + 1 + hist).astype(F32)
        dd = s / cnt_w - u[:, cols]
        p_parts.append(jnp.dot(dd.astype(BF16), poolw_ref[gi], preferred_element_type=F32))
    p = jnp.concatenate(p_parts, axis=1) * pscale_ref[...]
    tail = u[rows - POOL_ROWS:rows, :]
    prev_sc[...] = tail
    po_ref[0] = tail

    cat = jnp.concatenate([on, p], axis=1).astype(BF16)
    x1 = x + jnp.dot(cat, wout_ref[...], preferred_element_type=F32)

    h2 = _rms(x1, nq_ref[...]).astype(BF16)
    qm = jnp.dot(h2, wmq_ref[...], preferred_element_type=F32).astype(BF16)
    kk = k_ref[0].astype(BF16)
    vv = v_ref[0].astype(BF16)
    a_parts = []
    for hh in range(MEM_HEADS):
        cols = slice(hh * MEM_HD, (hh + 1) * MEM_HD)
        s = lax.dot_general(qm[:, cols], kk[:, cols], NT_DIMS, preferred_element_type=F32) * (MEM_HD ** -0.5)
        e = jnp.exp(s - jnp.max(s, axis=-1, keepdims=True))
        pr = e / jnp.sum(e, axis=-1, keepdims=True)
        a_parts.append(jnp.dot(pr.astype(BF16), vv[:, cols], preferred_element_type=F32))
    att = jnp.concatenate(a_parts, axis=1).astype(BF16)
    x2 = x1 + jnp.dot(att, wmo_ref[...], preferred_element_type=F32)
    x2_ref[...] = x2

    h3f = _rms(x2, nmoe_ref[...])
    _store_token_tiles(h3t_ref, h3f)
    logits = jnp.dot(h3f.astype(BF16), wrt_ref[...], preferred_element_type=F32) + brt_ref[...]
    lane = lax.broadcasted_iota(I32, (rows, LANES), 1)
    lane_f = lane.astype(F32)
    work = logits
    vals, idxs = [], []
    for _ in range(TOP_K):
        m = jnp.max(work, axis=-1, keepdims=True)
        idx = jnp.min(jnp.where(work == m, lane_f, float(LANES)), axis=-1, keepdims=True)
        vals.append(m)
        idxs.append(idx)
        work = jnp.where(lane_f == idx, -jnp.inf, work)
    exps = [jnp.exp(vk - vals[0]) for vk in vals]
    den = (exps[0] + exps[1]) + (exps[2] + exps[3])
    gates = [ek / den for ek in exps]

    hot = jnp.zeros((rows, LANES), F32)
    for idx in idxs:
        hot = hot + jnp.where(lane_f == idx, 1.0, 0.0)
    prefix = jnp.dot(ltri_ref[...], hot.astype(BF16), preferred_element_type=F32) + cnt_sc[0:1, :]
    ri = jnp.zeros((rows, LANES), F32)
    rg = jnp.zeros((rows, LANES), F32)
    for kk_, idx in enumerate(idxs):
        rank = jnp.sum(jnp.where(lane_f == idx, prefix, 0.0), axis=-1, keepdims=True)
        ri = jnp.where(lane == kk_, idx, ri)
        ri = jnp.where(lane == TOP_K + kk_, rank, ri)
        rg = jnp.where(lane == kk_, gates[kk_], rg)
    ri_ref[...] = ri.astype(I32)
    rg_ref[...] = rg
    new_cnt = cnt_sc[0:1, :] + jnp.sum(hot, axis=0, keepdims=True)
    cnt_sc[...] = jnp.broadcast_to(new_cnt, cnt_sc.shape)
    cnt_ref[...] = jnp.broadcast_to(new_cnt, cnt_ref.shape)


def _premoe_consts(rows):
    i = np.arange(rows)[:, None]
    j = np.arange(rows)[None, :]
    same = (i // CHUNK) == (j // CHUNK)
    tri = (same & (j <= i)).astype(np.float32)
    tot = same.astype(np.float32)
    tri2 = np.concatenate([tri, tot], axis=0)
    ltri = (j < i).astype(np.float32)
    je = np.arange(rows + POOL_ROWS)[None, :]
    band = np.stack([((je <= i + POOL_ROWS) & (je > i + POOL_ROWS - w)).astype(np.float32) for w in POOL_WINDOWS])
    return jnp.asarray(tri2, BF16), jnp.asarray(ltri, BF16), jnp.asarray(band, BF16)


def _premoe(x2d, batch, seq, hist_len, s0t, pool0, kmem, vmem, w):
    rows = min(PREMOE_ROWS, seq)
    nl = seq // rows
    t = batch * seq
    hist = min(int(hist_len), POOL_BUF)
    tri2, ltri, band = _premoe_consts(rows)
    row_spec = pl.BlockSpec((rows, D_MODEL), lambda b, l: (b * nl + l, 0))
    lane_spec = pl.BlockSpec((rows, LANES), lambda b, l: (b * nl + l, 0))
    tile_spec = pl.BlockSpec((rows * TILE_ROWS, LANES), lambda b, l: (b * nl + l, 0))

    def per_batch(shape):
        return pl.BlockSpec((1,) + shape, lambda b, l: (b, 0, 0))

    consts = [w["norm_mix"], w["w_qkvg"], w["w_r"], w["w_u"], w["w_gate"], w["b_gate"], w["gla_gain"], w["pool_w"],
              w["pool_scale"], w["w_out"], w["norm_mem_q"], w["w_mq"], w["w_mo"], w["norm_moe"], w["w_router"],
              w["b_router"], tri2, ltri, band]
    in_specs = [row_spec, per_batch((GLA_WIDTH, GLA_KW)), per_batch((POOL_ROWS, POOL_WIDTH)),
                per_batch((N_MEM, D_MODEL)), per_batch((N_MEM, D_MODEL))] + [_const_spec(c.shape) for c in consts]
    out_specs = [row_spec, tile_spec, lane_spec, lane_spec, pl.BlockSpec((8, LANES), lambda b, l: (0, 0)),
                 per_batch((GLA_WIDTH, GLA_KW)), per_batch((POOL_ROWS, POOL_WIDTH))]
    out_shape = [jax.ShapeDtypeStruct((t, D_MODEL), F32), jax.ShapeDtypeStruct((t * TILE_ROWS, LANES), F32),
                 jax.ShapeDtypeStruct((t, LANES), I32),
                 jax.ShapeDtypeStruct((t, LANES), F32), jax.ShapeDtypeStruct((8, LANES), F32),
                 jax.ShapeDtypeStruct((batch, GLA_WIDTH, GLA_KW), F32),
                 jax.ShapeDtypeStruct((batch, POOL_ROWS, POOL_WIDTH), F32)]
    return pl.pallas_call(
        functools.partial(_premoe_body, hist, rows),
        grid=(batch, nl),
        in_specs=in_specs,
        out_specs=out_specs,
        out_shape=out_shape,
        scratch_shapes=[pltpu.VMEM((GLA_WIDTH, GLA_KW), F32), pltpu.VMEM((POOL_ROWS, POOL_WIDTH), F32),
                        pltpu.VMEM((8, LANES), F32)],
        compiler_params=pltpu.CompilerParams(dimension_semantics=("arbitrary", "arbitrary"),
                                             vmem_limit_bytes=VMEM_LIMIT),
        name="premoe",
    )(x2d, s0t, pool0, kmem, vmem, *consts)


def _route_tables(route_i, counts_f, tm, nb):
    e = route_i[:, :TOP_K]
    rank = route_i[:, TOP_K:2 * TOP_K]
    counts = counts_f[0, :N_EXPERTS].astype(I32)
    padded = ((counts + tm - 1) // tm) * tm
    pend = jnp.cumsum(padded)
    pstart = pend - padded
    eids = jnp.arange(N_EXPERTS, dtype=I32)
    dest = jnp.sum(jnp.where(e[:, :, None] == eids, pstart, 0), axis=-1) + rank
    blk_start = jnp.arange(nb, dtype=I32) * tm
    block_e = jnp.minimum(jnp.sum((pend[None, :] <= blk_start[:, None]).astype(I32), axis=1), N_EXPERTS - 1)
    n_used = (pend[-1] // tm).astype(I32).reshape(1)
    return dest.astype(I32), block_e.astype(I32), n_used, pend.astype(I32), counts


def _dispatch_body(rows, tm, nb, nu_ref, pend_ref, cnt_ref, dest_ref, h3t_hbm, xg_hbm, zbuf, sem, zsem):
    i = pl.program_id(0)
    n_steps = pl.num_programs(0)
    nu = nu_ref[0]
    groups = rows // 8

    def tile(ref, row):
        return ref.at[pl.ds(pl.multiple_of(row * TILE_ROWS, TILE_ROWS), TILE_ROWS)]

    def zero_copy(start):
        return pltpu.make_async_copy(zbuf, xg_hbm.at[pl.ds(pl.multiple_of(start * TILE_ROWS, TILE_ROWS), tm * TILE_ROWS)],
                                     zsem)

    def for_each_fill(fn):
        def per_expert(e, c):
            @pl.when(cnt_ref[e] > 0)
            def _():
                fn(pend_ref[e] - tm)
            return c
        lax.fori_loop(0, N_EXPERTS, per_expert, 0)

        def per_tail(b, c):
            fn(b * tm)
            return c
        lax.fori_loop(nu, nb, per_tail, 0)

    @pl.when(i == 0)
    def _():
        zbuf[...] = jnp.zeros_like(zbuf)
        for_each_fill(lambda start: zero_copy(start).start())
        for_each_fill(lambda start: zero_copy(start).wait())

    def step_wait():
        for _ in range(TOP_K):
            pltpu.make_async_copy(h3t_hbm.at[pl.ds(0, rows * TILE_ROWS)], xg_hbm.at[pl.ds(0, rows * TILE_ROWS)],
                                  sem).wait()

    @pl.when(i >= 1)
    def _():
        step_wait()

    def send_group(gi, c):
        base = gi * (8 * TOP_K)
        tok0 = i * rows + gi * 8
        for r0 in range(0, 8, 2):
            dsts = [dest_ref[0, 0, base + r0 * TOP_K + j] for j in range(2 * TOP_K)]
            for j, dst in enumerate(dsts):
                pltpu.make_async_copy(tile(h3t_hbm, tok0 + r0 + j // TOP_K), tile(xg_hbm, dst),
                                      sem).start(priority=j % 2)
        return c
    lax.fori_loop(0, groups, send_group, 0)

    @pl.when(i == n_steps - 1)
    def _():
        step_wait()


def _dispatch(h3t, dest, n_used, pend, counts, tm, nb):
    t = h3t.shape[0] // TILE_ROWS
    rows = min(DISPATCH_ROWS, t)
    steps = t // rows
    dest3 = dest.reshape(steps, 1, rows * TOP_K)
    grid_spec = pltpu.PrefetchScalarGridSpec(
        num_scalar_prefetch=3,
        grid=(steps,),
        in_specs=[
            pl.BlockSpec((1, 1, rows * TOP_K), lambda i, *_: (i, 0, 0), memory_space=pltpu.SMEM),
            pl.BlockSpec(memory_space=pl.ANY),
        ],
        out_specs=pl.BlockSpec(memory_space=pl.ANY),
        scratch_shapes=[pltpu.VMEM((tm * TILE_ROWS, LANES), F32),
                        pltpu.SemaphoreType.DMA(()), pltpu.SemaphoreType.DMA(())],
    )
    return pl.pallas_call(
        functools.partial(_dispatch_body, rows, tm, nb),
        grid_spec=grid_spec,
        out_shape=jax.ShapeDtypeStruct((nb * tm * TILE_ROWS, LANES), F32),
        compiler_params=pltpu.CompilerParams(dimension_semantics=("arbitrary",), vmem_limit_bytes=VMEM_LIMIT),
        name="dispatch",
    )(n_used, pend, counts, dest3, h3t)


def _moe_body(tm, nu_ref, be_ref, xg_ref, wup_ref, bup_ref, wdn_ref, bdn_ref, ys_ref):
    del be_ref
    b = pl.program_id(0)

    @pl.when(b < nu_ref[0])
    def _():
        hh = _load_token_tiles(xg_ref, tm).astype(BF16)
        uu = jnp.dot(hh, wup_ref[0], preferred_element_type=F32) + bup_ref[0]
        glu = jnp.minimum(uu[:, :D_FF], SWIGLU_LIMIT)
        lin = jnp.clip(uu[:, D_FF:], -SWIGLU_LIMIT, SWIGLU_LIMIT)
        act = glu * jax.nn.sigmoid(SWIGLU_ALPHA * glu) * (lin + 1.0)
        _store_token_tiles(ys_ref, jnp.dot(act.astype(BF16), wdn_ref[0], preferred_element_type=F32) + bdn_ref[0])

    @pl.when(b >= nu_ref[0])
    def _():
        ys_ref[...] = jnp.zeros_like(ys_ref)


def _moe(xg, block_e, n_used, w, tm, nb):
    def used(b, nu):
        return jnp.minimum(b, nu[0] - 1)

    def per_expert(shape):
        return pl.BlockSpec((1,) + shape, lambda b, nu, be: (be[used(b, nu)], 0, 0))

    grid_spec = pltpu.PrefetchScalarGridSpec(
        num_scalar_prefetch=2,
        grid=(nb,),
        in_specs=[
            pl.BlockSpec((tm * TILE_ROWS, LANES), lambda b, nu, be: (used(b, nu), 0)),
            per_expert((D_MODEL, 2 * D_FF)), per_expert((1, 2 * D_FF)),
            per_expert((D_FF, D_MODEL)), per_expert((1, D_MODEL)),
        ],
        out_specs=pl.BlockSpec((tm * TILE_ROWS, LANES), lambda b, nu, be: (b, 0)),
    )
    return pl.pallas_call(
        functools.partial(_moe_body, tm),
        grid_spec=grid_spec,
        out_shape=jax.ShapeDtypeStruct((nb * tm * TILE_ROWS, LANES), F32),
        compiler_params=pltpu.CompilerParams(dimension_semantics=("arbitrary",), vmem_limit_bytes=VMEM_LIMIT),
        name="moe",
    )(n_used, block_e, xg, w["w_up"], w["b_up"], w["w_down"], w["b_down"])


def _combine_body(rows, destc_ref, destn_ref, x_ref, gate_ref, g_ref, ys_hbm, o_ref, gbuf, sem):
    i = pl.program_id(0)
    n_steps = pl.num_programs(0)
    slot = lax.rem(i, 2)
    groups = rows // 8

    def fetch(dest_ref, s):
        def fetch_group(gi, c):
            base = gi * (8 * TOP_K)
            srcs = [dest_ref[0, 0, base + j] for j in range(8 * TOP_K)]
            for r in range(8):
                for kk in range(TOP_K):
                    src = srcs[r * TOP_K + kk]
                    pltpu.make_async_copy(
                        ys_hbm.at[pl.ds(pl.multiple_of(src * TILE_ROWS, TILE_ROWS), TILE_ROWS)],
                        gbuf.at[s, kk, gi, :, r, :],
                        sem.at[s]).start(priority=kk % 2)
            return c
        lax.fori_loop(0, groups, fetch_group, 0)

    @pl.when(i == 0)
    def _():
        fetch(destc_ref, 0)

    @pl.when(i + 1 < n_steps)
    def _():
        fetch(destn_ref, 1 - slot)

    for kk in range(TOP_K):
        for r in range(8):
            pltpu.make_async_copy(ys_hbm.at[pl.ds(0, groups * TILE_ROWS)].reshape(groups, TILE_ROWS, LANES),
                                  gbuf.at[slot, kk, :, :, r, :], sem.at[slot]).wait()

    gate = gate_ref[...]
    acc = x_ref[...]
    for kk in range(TOP_K):
        acc = acc + _plain_rows(gbuf[slot, kk]) * gate[:, kk:kk + 1]
    o_ref[...] = _rms(acc, g_ref[...])


def _combine(x2, ys, dest, route_g, g):
    t = x2.shape[0]
    rows = min(COMBINE_ROWS, t)
    steps = t // rows
    dest3 = dest.reshape(steps, 1, rows * TOP_K)
    smem_cur = pl.BlockSpec((1, 1, rows * TOP_K), lambda i: (i, 0, 0), memory_space=pltpu.SMEM)
    smem_next = pl.BlockSpec((1, 1, rows * TOP_K), lambda i: (jnp.minimum(i + 1, steps - 1), 0, 0),
                             memory_space=pltpu.SMEM)
    return pl.pallas_call(
        functools.partial(_combine_body, rows),
        grid=(steps,),
        in_specs=[smem_cur, smem_next,
                  pl.BlockSpec((rows, D_MODEL), lambda i: (i, 0)),
                  pl.BlockSpec((rows, LANES), lambda i: (i, 0)),
                  pl.BlockSpec((1, D_MODEL), lambda i: (0, 0)),
                  pl.BlockSpec(memory_space=pl.ANY)],
        out_specs=pl.BlockSpec((rows, D_MODEL), lambda i: (i, 0)),
        out_shape=jax.ShapeDtypeStruct((t, D_MODEL), F32),
        scratch_shapes=[pltpu.VMEM((2, TOP_K, rows // 8, TILE_ROWS, 8, LANES), F32), pltpu.SemaphoreType.DMA((2,))],
        compiler_params=pltpu.CompilerParams(dimension_semantics=("arbitrary",), vmem_limit_bytes=VMEM_LIMIT),
        name="combine",
    )(dest3, dest3, x2, route_g, g, ys)


def _moe_layer(x2, h3t, route_i, route_g, counts_f, w, norm_final):
    t = x2.shape[0]
    tm = MOE_ROWS
    nb = (t * TOP_K + N_EXPERTS * (tm - 1) + tm - 1) // tm
    dest, block_e, n_used, pend, counts = _route_tables(route_i, counts_f, tm, nb)
    xg = _dispatch(h3t, dest, n_used, pend, counts, tm, nb)
    ys = _moe(xg, block_e, n_used, w, tm, nb)
    return _combine(x2, ys, dest, route_g, norm_final.reshape(1, -1))


def _state_to_t(s):
    bsz = s.shape[0]
    st = jnp.zeros((bsz, GLA_HEADS, GLA_DV, GLA_HEADS, GLA_DK), F32)
    for hh in range(GLA_HEADS):
        st = st.at[:, hh, :, hh, :].set(jnp.swapaxes(s[:, hh], 1, 2).astype(F32))
    return st.reshape(bsz, GLA_WIDTH, GLA_KW)


def _state_from_t(st):
    bsz = st.shape[0]
    s5 = st.reshape(bsz, GLA_HEADS, GLA_DV, GLA_HEADS, GLA_DK)
    return jnp.stack([jnp.swapaxes(s5[:, hh, :, hh, :], 1, 2) for hh in range(GLA_HEADS)], axis=1)


def _prep_weights(norm_mix, w_in, w_gate, b_gate, gla_gain, pool_w, pool_scale, w_out, norm_mem_q, w_mq, w_mo,
                  norm_moe, w_router, b_router, w_up, b_up, w_down, b_down):
    n_qkvg = 2 * GLA_KW + 2 * GLA_WIDTH
    w_r = jnp.zeros((D_MODEL, LANES), BF16).at[:, :GATE_RANK].set(w_in[:, n_qkvg:n_qkvg + GATE_RANK].astype(BF16))
    w_g = jnp.zeros((LANES, GLA_KW), BF16).at[:GATE_RANK].set(w_gate.astype(BF16))
    w_rt = jnp.zeros((D_MODEL, LANES), BF16).at[:, :N_EXPERTS].set(w_router.astype(BF16))
    b_rt = jnp.full((1, LANES), NEG_BIG, F32).at[0, :N_EXPERTS].set(b_router)
    return {
        "norm_mix": norm_mix.reshape(1, -1),
        "w_qkvg": w_in[:, :n_qkvg].astype(BF16),
        "w_r": w_r,
        "w_u": w_in[:, n_qkvg + GATE_RANK:].astype(BF16),
        "w_gate": w_g,
        "b_gate": b_gate.reshape(1, -1),
        "gla_gain": gla_gain.reshape(1, -1),
        "pool_w": pool_w.astype(BF16),
        "pool_scale": pool_scale.reshape(1, -1),
        "w_out": w_out.astype(BF16),
        "norm_mem_q": norm_mem_q.reshape(1, -1),
        "w_mq": w_mq.astype(BF16),
        "w_mo": w_mo.astype(BF16),
        "norm_moe": norm_moe.reshape(1, -1),
        "w_router": w_rt,
        "b_router": b_rt,
        "w_up": w_up.astype(BF16),
        "b_up": b_up.reshape(N_EXPERTS, 1, -1),
        "w_down": w_down.astype(BF16),
        "b_down": b_down.reshape(N_EXPERTS, 1, -1),
    }


def _block(x, s_gla, pool_prev, hist_len, mk, mv, w, norm_final):
    batch, seq, _ = x.shape
    x2d = x.reshape(batch * seq, D_MODEL)
    pool0 = jnp.concatenate([jnp.zeros((batch, 1, POOL_WIDTH), F32), pool_prev.astype(F32)], axis=1)
    x2, h3t, ri, rg, cnt, st_t, pool_t = _premoe(x2d, batch, seq, hist_len, _state_to_t(s_gla), pool0,
                                            mk.reshape(batch, N_MEM, D_MODEL), mv.reshape(batch, N_MEM, D_MODEL), w)
    out = _moe_layer(x2, h3t, ri, rg, cnt, w, norm_final)
    return out.reshape(batch, seq, D_MODEL), _state_from_t(st_t), pool_t[:, 1:, :]


def kernel(x_prompt, x_sample, mem_prompt, state_gla, state_pool, cache_mem_k, cache_mem_v, norm_mix, w_in, w_gate, b_gate, gla_gain, pool_w, pool_scale, w_out, norm_mem_q, norm_mem_kv, w_mq, w_mk, w_mv, w_mo, norm_moe, w_router, b_router, w_up, b_up, w_down, b_down, norm_final):
    depth = w_in.shape[0]
    assert depth == 1
    xp, xs = x_prompt, x_sample
    bp = xp.shape[0]
    gla_p, pool_p, mk_p, mv_p, gla_s, pool_s = [], [], [], [], [], []
    for l in range(depth):
        w = _prep_weights(norm_mix[l], w_in[l], w_gate[l], b_gate[l], gla_gain[l], pool_w[l], pool_scale[l], w_out[l],
                          norm_mem_q[l], w_mq[l], w_mo[l], norm_moe[l], w_router[l], b_router[l],
                          w_up[l], b_up[l], w_down[l], b_down[l])
        mk2, mv2 = _mem_kv(mem_prompt.reshape(bp * N_MEM, D_MODEL), norm_mem_kv[l].reshape(1, -1),
                           w_mk[l].astype(BF16), w_mv[l].astype(BF16))
        mk = mk2.reshape(bp, N_MEM, MEM_HEADS, MEM_HD)
        mv = mv2.reshape(bp, N_MEM, MEM_HEADS, MEM_HD)
        xs, ss, ps = _block(xs, state_gla[l], state_pool[l], PAST_LEN, cache_mem_k[l], cache_mem_v[l], w, norm_final)
        gla_s.append(ss)
        pool_s.append(ps)
        s0 = jnp.zeros((bp, GLA_HEADS, GLA_DK, GLA_DV), F32)
        p0 = jnp.zeros((bp, POOL_BUF, POOL_WIDTH), F32)
        xp, sp, pp = _block(xp, s0, p0, 0, mk, mv, w, norm_final)
        gla_p.append(sp)
        pool_p.append(pp)
        mk_p.append(mk)
        mv_p.append(mv)
    return (xp, xs, jnp.stack(gla_p), jnp.stack(pool_p), jnp.stack(mk_p), jnp.stack(mv_p),
            jnp.stack(gla_s), jnp.stack(pool_s))
```

```python
import functools

import numpy as np
import jax
import jax.numpy as jnp
from jax import lax
from jax.experimental import pallas as pl
from jax.experimental.pallas import tpu as pltpu

F32 = jnp.float32
BF16 = jnp.bfloat16
I32 = jnp.int32

D_MODEL = 1024
CHUNK = 64
GLA_HEADS = 4
GLA_DK = 64
GLA_DV = 128
GLA_KW = GLA_HEADS * GLA_DK
GLA_WIDTH = GLA_HEADS * GLA_DV
GATE_RANK = 16
GATE_TAU = 16.0
POOL_WIDTH = 512
POOL_GC = 128
POOL_WINDOWS = (2, 4, 8, 16)
POOL_BUF = 15
PAST_LEN = 4096
POOL_ROWS = 16
N_MEM = 256
MEM_HEADS = 4
MEM_HD = 256
N_EXPERTS = 32
TOP_K = 4
D_FF = 1024
SWIGLU_ALPHA = 1.702
SWIGLU_LIMIT = 7.0
EPS = 1e-6
LANES = 128
TILE_ROWS = D_MODEL // LANES
NEG_BIG = -1e30

PREMOE_ROWS = 256
MOE_ROWS = 512
DISPATCH_ROWS = 256
COMBINE_ROWS = 256
VMEM_LIMIT = 56 * 1024 * 1024

NT_DIMS = (((1,), (1,)), ((), ()))
TN_DIMS = (((0,), (0,)), ((), ()))


def _rms(x, g):
    ms = jnp.mean(x * x, axis=-1, keepdims=True)
    return x * lax.rsqrt(ms + EPS) * g


def _load_token_tiles(ref, rows):
    return jnp.concatenate([ref[pl.ds(s, rows, stride=TILE_ROWS), :] for s in range(TILE_ROWS)], axis=1)


def _plain_rows(v):
    g = v.shape[0]
    return jnp.concatenate([v[:, c].reshape(g * 8, LANES) for c in range(TILE_ROWS)], axis=1)


def _store_token_tiles(ref, val):
    rows = val.shape[0]
    for s in range(TILE_ROWS):
        ref[pl.ds(s, rows, stride=TILE_ROWS), :] = val[:, s * LANES:(s + 1) * LANES]


def _const_spec(shape):
    nd = len(shape)
    return pl.BlockSpec(shape, lambda *_: (0,) * nd, pipeline_mode=pl.Buffered(1))


def _memkv_body(m_ref, g_ref, wk_ref, wv_ref, k_ref, v_ref):
    m = _rms(m_ref[...], g_ref[...]).astype(BF16)
    k_ref[...] = jnp.dot(m, wk_ref[...], preferred_element_type=F32)
    v_ref[...] = jnp.dot(m, wv_ref[...], preferred_element_type=F32)


def _mem_kv(mem2d, g, wk, wv):
    n = mem2d.shape[0]
    tm = 512
    row = pl.BlockSpec((tm, D_MODEL), lambda i: (i, 0))
    return pl.pallas_call(
        _memkv_body,
        grid=(n // tm,),
        in_specs=[row, _const_spec((1, D_MODEL)), _const_spec((D_MODEL, D_MODEL)), _const_spec((D_MODEL, D_MODEL))],
        out_specs=[row, row],
        out_shape=[jax.ShapeDtypeStruct((n, D_MODEL), F32)] * 2,
        compiler_params=pltpu.CompilerParams(dimension_semantics=("arbitrary",), vmem_limit_bytes=VMEM_LIMIT),
        name="mem_kv",
    )(mem2d, g, wk, wv)


def _premoe_body(hist, rows,
                 x_ref, s0_ref, p0_ref, k_ref, v_ref,
                 nmix_ref, wqkvg_ref, wr_ref, wu_ref, wgate_ref, bgate_ref, ggain_ref, poolw_ref, pscale_ref,
                 wout_ref, nq_ref, wmq_ref, wmo_ref, nmoe_ref, wrt_ref, brt_ref,
                 tri2_ref, ltri_ref, band_ref,
                 x2_ref, ri_ref, rg_ref, cnt_ref, sto_ref, po_ref,
                 st_sc, prev_sc, cnt_sc):
    b = pl.program_id(0)
    l = pl.program_id(1)
    n_chunks = rows // CHUNK

    @pl.when(l == 0)
    def _():
        st_sc[...] = s0_ref[0]
        prow = lax.broadcasted_iota(I32, (POOL_ROWS, POOL_WIDTH), 0)
        prev_sc[...] = jnp.where(prow >= POOL_ROWS - hist, p0_ref[0], 0.0)

    @pl.when((b == 0) & (l == 0))
    def _():
        cnt_sc[...] = jnp.zeros_like(cnt_sc)

    x = x_ref[...]
    h = _rms(x, nmix_ref[...]).astype(BF16)
    z = jnp.dot(h, wqkvg_ref[...], preferred_element_type=F32)
    q = z[:, 0:GLA_KW] * (GLA_DK ** -0.5)
    k = z[:, GLA_KW:2 * GLA_KW]
    v = z[:, 2 * GLA_KW:2 * GLA_KW + GLA_WIDTH]
    g = z[:, 2 * GLA_KW + GLA_WIDTH:]

    r = jnp.dot(h, wr_ref[...], preferred_element_type=F32)
    gp = jnp.dot(r.astype(BF16), wgate_ref[...], preferred_element_type=F32) + bgate_ref[...]
    la = jax.nn.log_sigmoid(gp) * (1.0 / GATE_TAU)

    hi = la.astype(BF16)
    r1 = la - hi.astype(F32)
    mid = r1.astype(BF16)
    lo = (r1 - mid.astype(F32)).astype(BF16)
    la3 = jnp.concatenate([hi, mid, lo], axis=1)
    bb = jnp.dot(tri2_ref[...], la3, preferred_element_type=F32)
    bsum = (bb[:, 0:GLA_KW] + bb[:, GLA_KW:2 * GLA_KW]) + bb[:, 2 * GLA_KW:]
    bcum = bsum[:rows]
    btot = bsum[rows:]

    qd = q * jnp.exp(bcum)
    kd = k * jnp.exp(-bcum)
    kl = k * jnp.exp(btot - bcum)

    lane_kw = lax.broadcasted_iota(I32, (rows, GLA_KW), 1)
    rowi = lax.broadcasted_iota(I32, (rows, rows), 0)
    coli = lax.broadcasted_iota(I32, (rows, rows), 1)
    amask = (coli <= rowi) & (coli >= (rowi & ~(CHUNK - 1)))
    vb = v.astype(BF16)

    o_parts = []
    for hh in range(GLA_HEADS):
        mh = (lane_kw >> 6) == hh
        qh = jnp.where(mh, qd, 0.0).astype(BF16)
        kh = jnp.where(mh, kd, 0.0).astype(BF16)
        a = lax.dot_general(qh, kh, NT_DIMS, preferred_element_type=F32)
        a = jnp.where(amask, a, 0.0).astype(BF16)
        o_parts.append(jnp.dot(a, vb[:, hh * GLA_DV:(hh + 1) * GLA_DV], preferred_element_type=F32))
    o_intra = jnp.concatenate(o_parts, axis=1)

    srow = lax.broadcasted_iota(I32, (GLA_WIDTH, GLA_KW), 0)
    scol = lax.broadcasted_iota(I32, (GLA_WIDTH, GLA_KW), 1)
    smask = (srow >> 7) == (scol >> 6)
    qdb = qd.astype(BF16)
    klb = kl.astype(BF16)
    st = st_sc[...]
    oi_parts = []
    for n in range(n_chunks):
        lo_r, hi_r = n * CHUNK, (n + 1) * CHUNK
        oi_parts.append(lax.dot_general(qdb[lo_r:hi_r], st.astype(BF16), NT_DIMS, preferred_element_type=F32))
        upd = lax.dot_general(vb[lo_r:hi_r], klb[lo_r:hi_r], TN_DIMS, preferred_element_type=F32)
        dec = jnp.exp(btot[lo_r:lo_r + 1, :])
        st = st * dec + jnp.where(smask, upd, 0.0)
    st_sc[...] = st
    sto_ref[0] = st
    o_inter = oi_parts[0] if n_chunks == 1 else jnp.concatenate(oi_parts, axis=0)
    o = o_intra + o_inter

    gains = ggain_ref[...]
    on_parts = []
    for hh in range(GLA_HEADS):
        oh = o[:, hh * GLA_DV:(hh + 1) * GLA_DV]
        ms = jnp.mean(oh * oh, axis=-1, keepdims=True)
        on_parts.append(oh * lax.rsqrt(ms + EPS) * gains[:, hh * GLA_DV:(hh + 1) * GLA_DV])
    on = jnp.concatenate(on_parts, axis=1) * (g * jax.nn.sigmoid(g))

    u = jnp.dot(h, wu_ref[...], preferred_element_type=F32)
    ext = jnp.concatenate([prev_sc[...], u], axis=0).astype(BF16)
    pos = l * rows + lax.broadcasted_iota(I32, (rows, POOL_GC), 0)
    p_parts = []
    for gi, w in enumerate(POOL_WINDOWS):
        cols = slice(gi * POOL_GC, (gi + 1) * POOL_GC)
        s = jnp.dot(band_ref[gi], ext[:, cols], preferred_element_type=F32)
        cnt_w = jnp.minimum(w, pos + 1 + hist).astype(F32)
        dd = s / cnt_w - u[:, cols]
        p_parts.append(jnp.dot(dd.astype(BF16), poolw_ref[gi], preferred_element_type=F32))
    p = jnp.concatenate(p_parts, axis=1) * pscale_ref[...]
    tail = u[rows - POOL_ROWS:rows, :]
    prev_sc[...] = tail
    po_ref[0] = tail

    cat = jnp.concatenate([on, p], axis=1).astype(BF16)
    x1 = x + jnp.dot(cat, wout_ref[...], preferred_element_type=F32)

    h2 = _rms(x1, nq_ref[...]).astype(BF16)
    qm = jnp.dot(h2, wmq_ref[...], preferred_element_type=F32).astype(BF16)
    kk = k_ref[0].astype(BF16)
    vv = v_ref[0].astype(BF16)
    a_parts = []
    for hh in range(MEM_HEADS):
        cols = slice(hh * MEM_HD, (hh + 1) * MEM_HD)
        s = lax.dot_general(qm[:, cols], kk[:, cols], NT_DIMS, preferred_element_type=F32) * (MEM_HD ** -0.5)
        e = jnp.exp(s - jnp.max(s, axis=-1, keepdims=True))
        pr = e / jnp.sum(e, axis=-1, keepdims=True)
        a_parts.append(jnp.dot(pr.astype(BF16), vv[:, cols], preferred_element_type=F32))
    att = jnp.concatenate(a_parts, axis=1).astype(BF16)
    x2 = x1 + jnp.dot(att, wmo_ref[...], preferred_element_type=F32)
    x2_ref[...] = x2

    h3 = _rms(x2, nmoe_ref[...]).astype(BF16)
    logits = jnp.dot(h3, wrt_ref[...], preferred_element_type=F32) + brt_ref[...]
    lane = lax.broadcasted_iota(I32, (rows, LANES), 1)
    lane_f = lane.astype(F32)
    work = logits
    vals, idxs = [], []
    for _ in range(TOP_K):
        m = jnp.max(work, axis=-1, keepdims=True)
        idx = jnp.min(jnp.where(work == m, lane_f, float(LANES)), axis=-1, keepdims=True)
        vals.append(m)
        idxs.append(idx)
        work = jnp.where(lane_f == idx, -jnp.inf, work)
    exps = [jnp.exp(vk - vals[0]) for vk in vals]
    den = (exps[0] + exps[1]) + (exps[2] + exps[3])
    gates = [ek / den for ek in exps]

    hot = jnp.zeros((rows, LANES), F32)
    for idx in idxs:
        hot = hot + jnp.where(lane_f == idx, 1.0, 0.0)
    prefix = jnp.dot(ltri_ref[...], hot.astype(BF16), preferred_element_type=F32) + cnt_sc[0:1, :]
    ri = jnp.zeros((rows, LANES), F32)
    rg = jnp.zeros((rows, LANES), F32)
    for kk_, idx in enumerate(idxs):
        rank = jnp.sum(jnp.where(lane_f == idx, prefix, 0.0), axis=-1, keepdims=True)
        ri = jnp.where(lane == kk_, idx, ri)
        ri = jnp.where(lane == TOP_K + kk_, rank, ri)
        rg = jnp.where(lane == kk_, gates[kk_], rg)
    ri_ref[...] = ri.astype(I32)
    rg_ref[...] = rg
    new_cnt = cnt_sc[0:1, :] + jnp.sum(hot, axis=0, keepdims=True)
    cnt_sc[...] = jnp.broadcast_to(new_cnt, cnt_sc.shape)
    cnt_ref[...] = jnp.broadcast_to(new_cnt, cnt_ref.shape)


def _premoe_consts(rows):
    i = np.arange(rows)[:, None]
    j = np.arange(rows)[None, :]
    same = (i // CHUNK) == (j // CHUNK)
    tri = (same & (j <= i)).astype(np.float32)
    tot = same.astype(np.float32)
    tri2 = np.concatenate([tri, tot], axis=0)
    ltri = (j < i).astype(np.float32)
    je = np.arange(rows + POOL_ROWS)[None, :]
    band = np.stack([((je <= i + POOL_ROWS) & (je > i + POOL_ROWS - w)).astype(np.float32) for w in POOL_WINDOWS])
    return jnp.asarray(tri2, BF16), jnp.asarray(ltri, BF16), jnp.asarray(band, BF16)


def _premoe(x2d, batch, seq, hist_len, s0t, pool0, kmem, vmem, w):
    rows = min(PREMOE_ROWS, seq)
    nl = seq // rows
    t = batch * seq
    hist = min(int(hist_len), POOL_BUF)
    tri2, ltri, band = _premoe_consts(rows)
    row_spec = pl.BlockSpec((rows, D_MODEL), lambda b, l: (b * nl + l, 0))
    lane_spec = pl.BlockSpec((rows, LANES), lambda b, l: (b * nl + l, 0))

    def per_batch(shape):
        return pl.BlockSpec((1,) + shape, lambda b, l: (b, 0, 0))

    consts = [w["norm_mix"], w["w_qkvg"], w["w_r"], w["w_u"], w["w_gate"], w["b_gate"], w["gla_gain"], w["pool_w"],
              w["pool_scale"], w["w_out"], w["norm_mem_q"], w["w_mq"], w["w_mo"], w["norm_moe"], w["w_router"],
              w["b_router"], tri2, ltri, band]
    in_specs = [row_spec, per_batch((GLA_WIDTH, GLA_KW)), per_batch((POOL_ROWS, POOL_WIDTH)),
                per_batch((N_MEM, D_MODEL)), per_batch((N_MEM, D_MODEL))] + [_const_spec(c.shape) for c in consts]
    out_specs = [row_spec, lane_spec, lane_spec, pl.BlockSpec((8, LANES), lambda b, l: (0, 0)),
                 per_batch((GLA_WIDTH, GLA_KW)), per_batch((POOL_ROWS, POOL_WIDTH))]
    out_shape = [jax.ShapeDtypeStruct((t, D_MODEL), F32), jax.ShapeDtypeStruct((t, LANES), I32),
                 jax.ShapeDtypeStruct((t, LANES), F32), jax.ShapeDtypeStruct((8, LANES), F32),
                 jax.ShapeDtypeStruct((batch, GLA_WIDTH, GLA_KW), F32),
                 jax.ShapeDtypeStruct((batch, POOL_ROWS, POOL_WIDTH), F32)]
    return pl.pallas_call(
        functools.partial(_premoe_body, hist, rows),
        grid=(batch, nl),
        in_specs=in_specs,
        out_specs=out_specs,
        out_shape=out_shape,
        scratch_shapes=[pltpu.VMEM((GLA_WIDTH, GLA_KW), F32), pltpu.VMEM((POOL_ROWS, POOL_WIDTH), F32),
                        pltpu.VMEM((8, LANES), F32)],
        compiler_params=pltpu.CompilerParams(dimension_semantics=("arbitrary", "arbitrary"),
                                             vmem_limit_bytes=VMEM_LIMIT),
        name="premoe",
    )(x2d, s0t, pool0, kmem, vmem, *consts)


def _route_tables(route_i, counts_f, tm, nb):
    e = route_i[:, :TOP_K]
    rank = route_i[:, TOP_K:2 * TOP_K]
    counts = counts_f[0, :N_EXPERTS].astype(I32)
    padded = ((counts + tm - 1) // tm) * tm
    pend = jnp.cumsum(padded)
    pstart = pend - padded
    eids = jnp.arange(N_EXPERTS, dtype=I32)
    dest = jnp.sum(jnp.where(e[:, :, None] == eids, pstart, 0), axis=-1) + rank
    blk_start = jnp.arange(nb, dtype=I32) * tm
    block_e = jnp.minimum(jnp.sum((pend[None, :] <= blk_start[:, None]).astype(I32), axis=1), N_EXPERTS - 1)
    n_used = (pend[-1] // tm).astype(I32).reshape(1)
    return dest.astype(I32), block_e.astype(I32), n_used, pend.astype(I32), counts


def _dispatch_body(rows, tm, nb, nu_ref, pend_ref, cnt_ref, dest_ref, x_ref, g_ref, xg_hbm, hbuf, zbuf, sem, zsem):
    i = pl.program_id(0)
    n_steps = pl.num_programs(0)
    slot = lax.rem(i, 2)
    nu = nu_ref[0]
    groups = rows // 8

    def tile(ref, row):
        return ref.at[pl.ds(pl.multiple_of(row * TILE_ROWS, TILE_ROWS), TILE_ROWS)]

    def zero_copy(start):
        return pltpu.make_async_copy(zbuf, xg_hbm.at[pl.ds(pl.multiple_of(start * TILE_ROWS, TILE_ROWS), tm * TILE_ROWS)],
                                     zsem)

    def for_each_fill(fn):
        def per_expert(e, c):
            @pl.when(cnt_ref[e] > 0)
            def _():
                fn(pend_ref[e] - tm)
            return c
        lax.fori_loop(0, N_EXPERTS, per_expert, 0)

        def per_tail(b, c):
            fn(b * tm)
            return c
        lax.fori_loop(nu, nb, per_tail, 0)

    @pl.when(i == 0)
    def _():
        zbuf[...] = jnp.zeros_like(zbuf)
        for_each_fill(lambda start: zero_copy(start).start())
        for_each_fill(lambda start: zero_copy(start).wait())

    def row_wait(s):
        for _ in range(TOP_K):
            pltpu.make_async_copy(hbuf.at[s], xg_hbm.at[pl.ds(0, rows * TILE_ROWS)], sem.at[s]).wait()

    @pl.when(i >= 2)
    def _():
        row_wait(slot)

    _store_token_tiles(hbuf.at[slot], _rms(x_ref[...], g_ref[...]))

    def send_group(gi, c):
        base = gi * (8 * TOP_K)
        for r in range(8):
            for kk in range(TOP_K):
                dst = dest_ref[0, 0, base + r * TOP_K + kk]
                pltpu.make_async_copy(tile(hbuf.at[slot], gi * 8 + r), tile(xg_hbm, dst),
                                      sem.at[slot]).start(priority=kk % 2)
        return c
    lax.fori_loop(0, groups, send_group, 0)

    @pl.when(i == n_steps - 1)
    def _():
        row_wait(slot)

        @pl.when(i >= 1)
        def _():
            row_wait(1 - slot)


def _dispatch(x2, dest, n_used, pend, counts, g, tm, nb):
    t = x2.shape[0]
    rows = min(DISPATCH_ROWS, t)
    steps = t // rows
    dest3 = dest.reshape(steps, 1, rows * TOP_K)
    grid_spec = pltpu.PrefetchScalarGridSpec(
        num_scalar_prefetch=3,
        grid=(steps,),
        in_specs=[
            pl.BlockSpec((1, 1, rows * TOP_K), lambda i, *_: (i, 0, 0), memory_space=pltpu.SMEM),
            pl.BlockSpec((rows, D_MODEL), lambda i, *_: (i, 0)),
            pl.BlockSpec((1, D_MODEL), lambda i, *_: (0, 0)),
        ],
        out_specs=pl.BlockSpec(memory_space=pl.ANY),
        scratch_shapes=[pltpu.VMEM((2, rows * TILE_ROWS, LANES), F32), pltpu.VMEM((tm * TILE_ROWS, LANES), F32),
                        pltpu.SemaphoreType.DMA((2,)), pltpu.SemaphoreType.DMA(())],
    )
    return pl.pallas_call(
        functools.partial(_dispatch_body, rows, tm, nb),
        grid_spec=grid_spec,
        out_shape=jax.ShapeDtypeStruct((nb * tm * TILE_ROWS, LANES), F32),
        compiler_params=pltpu.CompilerParams(dimension_semantics=("arbitrary",), vmem_limit_bytes=VMEM_LIMIT),
        name="dispatch",
    )(n_used, pend, counts, dest3, x2, g)


def _moe_body(tm, nu_ref, be_ref, xg_ref, wup_ref, bup_ref, wdn_ref, bdn_ref, ys_ref):
    del be_ref
    b = pl.program_id(0)

    @pl.when(b < nu_ref[0])
    def _():
        hh = _load_token_tiles(xg_ref, tm).astype(BF16)
        uu = jnp.dot(hh, wup_ref[0], preferred_element_type=F32) + bup_ref[0]
        glu = jnp.minimum(uu[:, :D_FF], SWIGLU_LIMIT)
        lin = jnp.clip(uu[:, D_FF:], -SWIGLU_LIMIT, SWIGLU_LIMIT)
        act = glu * jax.nn.sigmoid(SWIGLU_ALPHA * glu) * (lin + 1.0)
        _store_token_tiles(ys_ref, jnp.dot(act.astype(BF16), wdn_ref[0], preferred_element_type=F32) + bdn_ref[0])

    @pl.when(b >= nu_ref[0])
    def _():
        ys_ref[...] = jnp.zeros_like(ys_ref)


def _moe(xg, block_e, n_used, w, tm, nb):
    def used(b, nu):
        return jnp.minimum(b, nu[0] - 1)

    def per_expert(shape):
        return pl.BlockSpec((1,) + shape, lambda b, nu, be: (be[used(b, nu)], 0, 0))

    grid_spec = pltpu.PrefetchScalarGridSpec(
        num_scalar_prefetch=2,
        grid=(nb,),
        in_specs=[
            pl.BlockSpec((tm * TILE_ROWS, LANES), lambda b, nu, be: (used(b, nu), 0)),
            per_expert((D_MODEL, 2 * D_FF)), per_expert((1, 2 * D_FF)),
            per_expert((D_FF, D_MODEL)), per_expert((1, D_MODEL)),
        ],
        out_specs=pl.BlockSpec((tm * TILE_ROWS, LANES), lambda b, nu, be: (b, 0)),
    )
    return pl.pallas_call(
        functools.partial(_moe_body, tm),
        grid_spec=grid_spec,
        out_shape=jax.ShapeDtypeStruct((nb * tm * TILE_ROWS, LANES), F32),
        compiler_params=pltpu.CompilerParams(dimension_semantics=("arbitrary",), vmem_limit_bytes=VMEM_LIMIT),
        name="moe",
    )(n_used, block_e, xg, w["w_up"], w["b_up"], w["w_down"], w["b_down"])


def _combine_body(rows, destc_ref, destn_ref, x_ref, gate_ref, g_ref, ys_hbm, o_ref, gbuf, sem):
    i = pl.program_id(0)
    n_steps = pl.num_programs(0)
    slot = lax.rem(i, 2)
    groups = rows // 8

    def fetch(dest_ref, s):
        def fetch_group(gi, c):
            base = gi * (8 * TOP_K)
            srcs = [dest_ref[0, 0, base + j] for j in range(8 * TOP_K)]
            for r in range(8):
                for kk in range(TOP_K):
                    src = srcs[r * TOP_K + kk]
                    pltpu.make_async_copy(
                        ys_hbm.at[pl.ds(pl.multiple_of(src * TILE_ROWS, TILE_ROWS), TILE_ROWS)],
                        gbuf.at[s, kk, gi, :, r, :],
                        sem.at[s]).start(priority=kk % 2)
            return c
        lax.fori_loop(0, groups, fetch_group, 0)

    @pl.when(i == 0)
    def _():
        fetch(destc_ref, 0)

    @pl.when(i + 1 < n_steps)
    def _():
        fetch(destn_ref, 1 - slot)

    for kk in range(TOP_K):
        for r in range(8):
            pltpu.make_async_copy(ys_hbm.at[pl.ds(0, groups * TILE_ROWS)].reshape(groups, TILE_ROWS, LANES),
                                  gbuf.at[slot, kk, :, :, r, :], sem.at[slot]).wait()

    gate = gate_ref[...]
    acc = x_ref[...]
    for kk in range(TOP_K):
        acc = acc + _plain_rows(gbuf[slot, kk]) * gate[:, kk:kk + 1]
    o_ref[...] = _rms(acc, g_ref[...])


def _combine(x2, ys, dest, route_g, g):
    t = x2.shape[0]
    rows = min(COMBINE_ROWS, t)
    steps = t // rows
    dest3 = dest.reshape(steps, 1, rows * TOP_K)
    smem_cur = pl.BlockSpec((1, 1, rows * TOP_K), lambda i: (i, 0, 0), memory_space=pltpu.SMEM)
    smem_next = pl.BlockSpec((1, 1, rows * TOP_K), lambda i: (jnp.minimum(i + 1, steps - 1), 0, 0),
                             memory_space=pltpu.SMEM)
    return pl.pallas_call(
        functools.partial(_combine_body, rows),
        grid=(steps,),
        in_specs=[smem_cur, smem_next,
                  pl.BlockSpec((rows, D_MODEL), lambda i: (i, 0)),
                  pl.BlockSpec((rows, LANES), lambda i: (i, 0)),
                  pl.BlockSpec((1, D_MODEL), lambda i: (0, 0)),
                  pl.BlockSpec(memory_space=pl.ANY)],
        out_specs=pl.BlockSpec((rows, D_MODEL), lambda i: (i, 0)),
        out_shape=jax.ShapeDtypeStruct((t, D_MODEL), F32),
        scratch_shapes=[pltpu.VMEM((2, TOP_K, rows // 8, TILE_ROWS, 8, LANES), F32), pltpu.SemaphoreType.DMA((2,))],
        compiler_params=pltpu.CompilerParams(dimension_semantics=("arbitrary",), vmem_limit_bytes=VMEM_LIMIT),
        name="combine",
    )(dest3, dest3, x2, route_g, g, ys)


def _moe_layer(x2, route_i, route_g, counts_f, w, norm_final):
    t = x2.shape[0]
    tm = MOE_ROWS
    nb = (t * TOP_K + N_EXPERTS * (tm - 1) + tm - 1) // tm
    dest, block_e, n_used, pend, counts = _route_tables(route_i, counts_f, tm, nb)
    xg = _dispatch(x2, dest, n_used, pend, counts, w["norm_moe"], tm, nb)
    ys = _moe(xg, block_e, n_used, w, tm, nb)
    return _combine(x2, ys, dest, route_g, norm_final.reshape(1, -1))


def _state_to_t(s):
    bsz = s.shape[0]
    st = jnp.zeros((bsz, GLA_HEADS, GLA_DV, GLA_HEADS, GLA_DK), F32)
    for hh in range(GLA_HEADS):
        st = st.at[:, hh, :, hh, :].set(jnp.swapaxes(s[:, hh], 1, 2).astype(F32))
    return st.reshape(bsz, GLA_WIDTH, GLA_KW)


def _state_from_t(st):
    bsz = st.shape[0]
    s5 = st.reshape(bsz, GLA_HEADS, GLA_DV, GLA_HEADS, GLA_DK)
    return jnp.stack([jnp.swapaxes(s5[:, hh, :, hh, :], 1, 2) for hh in range(GLA_HEADS)], axis=1)


def _prep_weights(norm_mix, w_in, w_gate, b_gate, gla_gain, pool_w, pool_scale, w_out, norm_mem_q, w_mq, w_mo,
                  norm_moe, w_router, b_router, w_up, b_up, w_down, b_down):
    n_qkvg = 2 * GLA_KW + 2 * GLA_WIDTH
    w_r = jnp.zeros((D_MODEL, LANES), BF16).at[:, :GATE_RANK].set(w_in[:, n_qkvg:n_qkvg + GATE_RANK].astype(BF16))
    w_g = jnp.zeros((LANES, GLA_KW), BF16).at[:GATE_RANK].set(w_gate.astype(BF16))
    w_rt = jnp.zeros((D_MODEL, LANES), BF16).at[:, :N_EXPERTS].set(w_router.astype(BF16))
    b_rt = jnp.full((1, LANES), NEG_BIG, F32).at[0, :N_EXPERTS].set(b_router)
    return {
        "norm_mix": norm_mix.reshape(1, -1),
        "w_qkvg": w_in[:, :n_qkvg].astype(BF16),
        "w_r": w_r,
        "w_u": w_in[:, n_qkvg + GATE_RANK:].astype(BF16),
        "w_gate": w_g,
        "b_gate": b_gate.reshape(1, -1),
        "gla_gain": gla_gain.reshape(1, -1),
        "pool_w": pool_w.astype(BF16),
        "pool_scale": pool_scale.reshape(1, -1),
        "w_out": w_out.astype(BF16),
        "norm_mem_q": norm_mem_q.reshape(1, -1),
        "w_mq": w_mq.astype(BF16),
        "w_mo": w_mo.astype(BF16),
        "norm_moe": norm_moe.reshape(1, -1),
        "w_router": w_rt,
        "b_router": b_rt,
        "w_up": w_up.astype(BF16),
        "b_up": b_up.reshape(N_EXPERTS, 1, -1),
        "w_down": w_down.astype(BF16),
        "b_down": b_down.reshape(N_EXPERTS, 1, -1),
    }


def _block(x, s_gla, pool_prev, hist_len, mk, mv, w, norm_final):
    batch, seq, _ = x.shape
    x2d = x.reshape(batch * seq, D_MODEL)
    pool0 = jnp.concatenate([jnp.zeros((batch, 1, POOL_WIDTH), F32), pool_prev.astype(F32)], axis=1)
    x2, ri, rg, cnt, st_t, pool_t = _premoe(x2d, batch, seq, hist_len, _state_to_t(s_gla), pool0,
                                            mk.reshape(batch, N_MEM, D_MODEL), mv.reshape(batch, N_MEM, D_MODEL), w)
    out = _moe_layer(x2, ri, rg, cnt, w, norm_final)
    return out.reshape(batch, seq, D_MODEL), _state_from_t(st_t), pool_t[:, 1:, :]


def kernel(x_prompt, x_sample, mem_prompt, state_gla, state_pool, cache_mem_k, cache_mem_v, norm_mix, w_in, w_gate, b_gate, gla_gain, pool_w, pool_scale, w_out, norm_mem_q, norm_mem_kv, w_mq, w_mk, w_mv, w_mo, norm_moe, w_router, b_router, w_up, b_up, w_down, b_down, norm_final):
    depth = w_in.shape[0]
    assert depth == 1
    xp, xs = x_prompt, x_sample
    bp = xp.shape[0]
    gla_p, pool_p, mk_p, mv_p, gla_s, pool_s = [], [], [], [], [], []
    for l in range(depth):
        w = _prep_weights(norm_mix[l], w_in[l], w_gate[l], b_gate[l], gla_gain[l], pool_w[l], pool_scale[l], w_out[l],
                          norm_mem_q[l], w_mq[l], w_mo[l], norm_moe[l], w_router[l], b_router[l],
                          w_up[l], b_up[l], w_down[l], b_down[l])
        mk2, mv2 = _mem_kv(mem_prompt.reshape(bp * N_MEM, D_MODEL), norm_mem_kv[l].reshape(1, -1),
                           w_mk[l].astype(BF16), w_mv[l].astype(BF16))
        mk = mk2.reshape(bp, N_MEM, MEM_HEADS, MEM_HD)
        mv = mv2.reshape(bp, N_MEM, MEM_HEADS, MEM_HD)
        xs, ss, ps = _block(xs, state_gla[l], state_pool[l], PAST_LEN, cache_mem_k[l], cache_mem_v[l], w, norm_final)
        gla_s.append(ss)
        pool_s.append(ps)
        s0 = jnp.zeros((bp, GLA_HEADS, GLA_DK, GLA_DV), F32)
        p0 = jnp.zeros((bp, POOL_BUF, POOL_WIDTH), F32)
        xp, sp, pp = _block(xp, s0, p0, 0, mk, mv, w, norm_final)
        gla_p.append(sp)
        pool_p.append(pp)
        mk_p.append(mk)
        mv_p.append(mv)
    return (xp, xs, jnp.stack(gla_p), jnp.stack(pool_p), jnp.stack(mk_p), jnp.stack(mv_p),
            jnp.stack(gla_s), jnp.stack(pool_s))
```

```python
import functools

import numpy as np
import jax
import jax.numpy as jnp
from jax import lax
from jax.experimental import pallas as pl
from jax.experimental.pallas import tpu as pltpu

F32 = jnp.float32
BF16 = jnp.bfloat16
I32 = jnp.int32

D_MODEL = 1024
CHUNK = 64
GLA_HEADS = 4
GLA_DK = 64
GLA_DV = 128
GLA_KW = GLA_HEADS * GLA_DK
GLA_WIDTH = GLA_HEADS * GLA_DV
GATE_RANK = 16
GATE_TAU = 16.0
POOL_WIDTH = 512
POOL_GC = 128
POOL_WINDOWS = (2, 4, 8, 16)
POOL_BUF = 15
PAST_LEN = 4096
POOL_ROWS = 16
N_MEM = 256
MEM_HEADS = 4
MEM_HD = 256
N_EXPERTS = 32
TOP_K = 4
D_FF = 1024
SWIGLU_ALPHA = 1.702
SWIGLU_LIMIT = 7.0
EPS = 1e-6
LANES = 128
TILE_ROWS = D_MODEL // LANES
NEG_BIG = -1e30

PREMOE_ROWS = 512
PREMOE_SUB = 256
MOE_ROWS = 512
DISPATCH_ROWS = 256
COMBINE_ROWS = 256
VMEM_LIMIT = 56 * 1024 * 1024

NT_DIMS = (((1,), (1,)), ((), ()))
TN_DIMS = (((0,), (0,)), ((), ()))


def _rms(x, g):
    ms = jnp.mean(x * x, axis=-1, keepdims=True)
    return x * lax.rsqrt(ms + EPS) * g


def _load_token_tiles(ref, rows):
    return jnp.concatenate([ref[pl.ds(s, rows, stride=TILE_ROWS), :] for s in range(TILE_ROWS)], axis=1)


def _plain_rows(v):
    g = v.shape[0]
    return jnp.concatenate([v[:, c].reshape(g * 8, LANES) for c in range(TILE_ROWS)], axis=1)


def _store_token_tiles(ref, val):
    rows = val.shape[0]
    for s in range(TILE_ROWS):
        ref[pl.ds(s, rows, stride=TILE_ROWS), :] = val[:, s * LANES:(s + 1) * LANES]


def _const_spec(shape):
    nd = len(shape)
    return pl.BlockSpec(shape, lambda *_: (0,) * nd, pipeline_mode=pl.Buffered(1))


def _memkv_body(m_ref, g_ref, wk_ref, wv_ref, k_ref, v_ref):
    m = _rms(m_ref[...], g_ref[...]).astype(BF16)
    k_ref[...] = jnp.dot(m, wk_ref[...], preferred_element_type=F32)
    v_ref[...] = jnp.dot(m, wv_ref[...], preferred_element_type=F32)


def _mem_kv(mem2d, g, wk, wv):
    n = mem2d.shape[0]
    tm = 512
    row = pl.BlockSpec((tm, D_MODEL), lambda i: (i, 0))
    return pl.pallas_call(
        _memkv_body,
        grid=(n // tm,),
        in_specs=[row, _const_spec((1, D_MODEL)), _const_spec((D_MODEL, D_MODEL)), _const_spec((D_MODEL, D_MODEL))],
        out_specs=[row, row],
        out_shape=[jax.ShapeDtypeStruct((n, D_MODEL), F32)] * 2,
        compiler_params=pltpu.CompilerParams(dimension_semantics=("arbitrary",), vmem_limit_bytes=VMEM_LIMIT),
        name="mem_kv",
    )(mem2d, g, wk, wv)


def _premoe_body(hist, rows,
                 x_ref, s0_ref, p0_ref, k_ref, v_ref,
                 nmix_ref, wqkvg_ref, wu_ref, wgate_ref, bgate_ref, ggain_ref, poolw_ref, pscale_ref,
                 wout_ref, nq_ref, wmq_ref, wmo_ref, nmoe_ref, wrt_ref, brt_ref,
                 tri2_ref, ltri_ref, band_ref,
                 x2_ref, ri_ref, rg_ref, cnt_ref, sto_ref, po_ref,
                 st_sc, prev_sc, cnt_sc):
    b = pl.program_id(0)
    l = pl.program_id(1)
    n_chunks = rows // CHUNK
    sub = min(rows, PREMOE_SUB)
    subs = [slice(s0, s0 + sub) for s0 in range(0, rows, sub)]

    @pl.when(l == 0)
    def _():
        st_sc[...] = s0_ref[0]
        prow = lax.broadcasted_iota(I32, (POOL_ROWS, POOL_WIDTH), 0)
        prev_sc[...] = jnp.where(prow >= POOL_ROWS - hist, p0_ref[0], 0.0)

    @pl.when((b == 0) & (l == 0))
    def _():
        cnt_sc[...] = jnp.zeros_like(cnt_sc)

    x = x_ref[...]
    h = _rms(x, nmix_ref[...]).astype(BF16)
    z = jnp.dot(h, wqkvg_ref[...], preferred_element_type=F32)
    q = z[:, 0:GLA_KW] * (GLA_DK ** -0.5)
    k = z[:, GLA_KW:2 * GLA_KW]
    v = z[:, 2 * GLA_KW:2 * GLA_KW + GLA_WIDTH]
    g = z[:, 2 * GLA_KW + GLA_WIDTH:2 * GLA_KW + 2 * GLA_WIDTH]
    r = z[:, 2 * GLA_KW + 2 * GLA_WIDTH:]

    gp = jnp.dot(r.astype(BF16), wgate_ref[...], preferred_element_type=F32) + bgate_ref[...]
    la = jax.nn.log_sigmoid(gp) * (1.0 / GATE_TAU)

    hi = la.astype(BF16)
    r1 = la - hi.astype(F32)
    mid = r1.astype(BF16)
    lo = (r1 - mid.astype(F32)).astype(BF16)
    la3 = jnp.concatenate([hi, mid, lo], axis=1)
    bcum_parts, btot_parts = [], []
    for sl in subs:
        bb = jnp.dot(tri2_ref[...], la3[sl], preferred_element_type=F32)
        bsum = (bb[:, 0:GLA_KW] + bb[:, GLA_KW:2 * GLA_KW]) + bb[:, 2 * GLA_KW:]
        bcum_parts.append(bsum[:sub])
        btot_parts.append(bsum[sub:])
    bcum = jnp.concatenate(bcum_parts, axis=0)
    btot = jnp.concatenate(btot_parts, axis=0)

    qd = q * jnp.exp(bcum)
    kd = k * jnp.exp(-bcum)
    kl = k * jnp.exp(btot - bcum)

    lane_kw = lax.broadcasted_iota(I32, (rows, GLA_KW), 1)
    rowi = lax.broadcasted_iota(I32, (sub, sub), 0)
    coli = lax.broadcasted_iota(I32, (sub, sub), 1)
    amask = (coli <= rowi) & (coli >= (rowi & ~(CHUNK - 1)))
    vb = v.astype(BF16)

    o_heads = []
    for hh in range(GLA_HEADS):
        mh = (lane_kw >> 6) == hh
        qh = jnp.where(mh, qd, 0.0).astype(BF16)
        kh = jnp.where(mh, kd, 0.0).astype(BF16)
        o_sub = []
        for sl in subs:
            a = lax.dot_general(qh[sl], kh[sl], NT_DIMS, preferred_element_type=F32)
            a = jnp.where(amask, a, 0.0).astype(BF16)
            o_sub.append(jnp.dot(a, vb[sl, hh * GLA_DV:(hh + 1) * GLA_DV], preferred_element_type=F32))
        o_heads.append(jnp.concatenate(o_sub, axis=0))
    o_intra = jnp.concatenate(o_heads, axis=1)

    srow = lax.broadcasted_iota(I32, (GLA_WIDTH, GLA_KW), 0)
    scol = lax.broadcasted_iota(I32, (GLA_WIDTH, GLA_KW), 1)
    smask = (srow >> 7) == (scol >> 6)
    qdb = qd.astype(BF16)
    klb = kl.astype(BF16)
    st = st_sc[...]
    oi_parts = []
    for n in range(n_chunks):
        lo_r, hi_r = n * CHUNK, (n + 1) * CHUNK
        oi_parts.append(lax.dot_general(qdb[lo_r:hi_r], st.astype(BF16), NT_DIMS, preferred_element_type=F32))
        upd = lax.dot_general(vb[lo_r:hi_r], klb[lo_r:hi_r], TN_DIMS, preferred_element_type=F32)
        dec = jnp.exp(btot[lo_r:lo_r + 1, :])
        st = st * dec + jnp.where(smask, upd, 0.0)
    st_sc[...] = st
    sto_ref[0] = st
    o_inter = oi_parts[0] if n_chunks == 1 else jnp.concatenate(oi_parts, axis=0)
    o = o_intra + o_inter

    gains = ggain_ref[...]
    on_parts = []
    for hh in range(GLA_HEADS):
        oh = o[:, hh * GLA_DV:(hh + 1) * GLA_DV]
        ms = jnp.mean(oh * oh, axis=-1, keepdims=True)
        on_parts.append(oh * lax.rsqrt(ms + EPS) * gains[:, hh * GLA_DV:(hh + 1) * GLA_DV])
    on = jnp.concatenate(on_parts, axis=1) * (g * jax.nn.sigmoid(g))

    u = jnp.dot(h, wu_ref[...], preferred_element_type=F32)
    prevs = [prev_sc[...]] + [u[sl.stop - POOL_ROWS:sl.stop] for sl in subs[:-1]]
    exts = [jnp.concatenate([pv, u[sl]], axis=0).astype(BF16) for pv, sl in zip(prevs, subs)]
    pos = l * rows + lax.broadcasted_iota(I32, (rows, POOL_GC), 0)
    p_parts = []
    for gi, w in enumerate(POOL_WINDOWS):
        cols = slice(gi * POOL_GC, (gi + 1) * POOL_GC)
        s = jnp.concatenate([jnp.dot(band_ref[gi], ext[:, cols], preferred_element_type=F32) for ext in exts], axis=0)
        cnt_w = jnp.minimum(w, pos + 1 + hist).astype(F32)
        dd = s / cnt_w - u[:, cols]
        p_parts.append(jnp.dot(dd.astype(BF16), poolw_ref[gi], preferred_element_type=F32))
    p = jnp.concatenate(p_parts, axis=1) * pscale_ref[...]
    tail = u[rows - POOL_ROWS:rows, :]
    prev_sc[...] = tail
    po_ref[0] = tail

    cat = jnp.concatenate([on, p], axis=1).astype(BF16)
    x1 = x + jnp.dot(cat, wout_ref[...], preferred_element_type=F32)

    h2 = _rms(x1, nq_ref[...]).astype(BF16)
    qm = jnp.dot(h2, wmq_ref[...], preferred_element_type=F32).astype(BF16)
    kk = k_ref[0].astype(BF16)
    vv = v_ref[0].astype(BF16)
    a_parts = []
    for hh in range(MEM_HEADS):
        cols = slice(hh * MEM_HD, (hh + 1) * MEM_HD)
        s = lax.dot_general(qm[:, cols], kk[:, cols], NT_DIMS, preferred_element_type=F32) * (MEM_HD ** -0.5)
        e = jnp.exp(s - jnp.max(s, axis=-1, keepdims=True))
        pr = e / jnp.sum(e, axis=-1, keepdims=True)
        a_parts.append(jnp.dot(pr.astype(BF16), vv[:, cols], preferred_element_type=F32))
    att = jnp.concatenate(a_parts, axis=1).astype(BF16)
    x2 = x1 + jnp.dot(att, wmo_ref[...], preferred_element_type=F32)
    x2_ref[...] = x2

    h3 = _rms(x2, nmoe_ref[...]).astype(BF16)
    logits = jnp.dot(h3, wrt_ref[...], preferred_element_type=F32) + brt_ref[...]
    lane = lax.broadcasted_iota(I32, (rows, LANES), 1)
    lane_f = lane.astype(F32)
    work = logits
    vals, idxs = [], []
    for _ in range(TOP_K):
        m = jnp.max(work, axis=-1, keepdims=True)
        idx = jnp.min(jnp.where(work == m, lane_f, float(LANES)), axis=-1, keepdims=True)
        vals.append(m)
        idxs.append(idx)
        work = jnp.where(lane_f == idx, -jnp.inf, work)
    exps = [jnp.exp(vk - vals[0]) for vk in vals]
    den = (exps[0] + exps[1]) + (exps[2] + exps[3])
    gates = [ek / den for ek in exps]

    hot = jnp.zeros((rows, LANES), F32)
    for idx in idxs:
        hot = hot + jnp.where(lane_f == idx, 1.0, 0.0)
    prefix = jnp.dot(ltri_ref[...], hot.astype(BF16), preferred_element_type=F32) + cnt_sc[0:1, :]
    ri = jnp.zeros((rows, LANES), F32)
    rg = jnp.zeros((rows, LANES), F32)
    for kk_, idx in enumerate(idxs):
        rank = jnp.sum(jnp.where(lane_f == idx, prefix, 0.0), axis=-1, keepdims=True)
        ri = jnp.where(lane == kk_, idx, ri)
        ri = jnp.where(lane == TOP_K + kk_, rank, ri)
        rg = jnp.where(lane == kk_, gates[kk_], rg)
    ri_ref[...] = ri.astype(I32)
    rg_ref[...] = rg
    new_cnt = cnt_sc[0:1, :] + jnp.sum(hot, axis=0, keepdims=True)
    cnt_sc[...] = jnp.broadcast_to(new_cnt, cnt_sc.shape)
    cnt_ref[...] = jnp.broadcast_to(new_cnt, cnt_ref.shape)


def _premoe_consts(rows):
    sub = min(rows, PREMOE_SUB)
    i = np.arange(sub)[:, None]
    j = np.arange(sub)[None, :]
    same = (i // CHUNK) == (j // CHUNK)
    tri = (same & (j <= i)).astype(np.float32)
    tot = same.astype(np.float32)
    tri2 = np.concatenate([tri, tot], axis=0)
    je = np.arange(sub + POOL_ROWS)[None, :]
    band = np.stack([((je <= i + POOL_ROWS) & (je > i + POOL_ROWS - w)).astype(np.float32) for w in POOL_WINDOWS])
    ltri = (np.arange(rows)[None, :] < np.arange(rows)[:, None]).astype(np.float32)
    return jnp.asarray(tri2, BF16), jnp.asarray(ltri, BF16), jnp.asarray(band, BF16)


def _premoe(x2d, batch, seq, hist_len, s0t, pool0, kmem, vmem, w):
    rows = min(PREMOE_ROWS, seq)
    nl = seq // rows
    t = batch * seq
    hist = min(int(hist_len), POOL_BUF)
    tri2, ltri, band = _premoe_consts(rows)
    row_spec = pl.BlockSpec((rows, D_MODEL), lambda b, l: (b * nl + l, 0))
    lane_spec = pl.BlockSpec((rows, LANES), lambda b, l: (b * nl + l, 0))

    def per_batch(shape):
        return pl.BlockSpec((1,) + shape, lambda b, l: (b, 0, 0))

    consts = [w["norm_mix"], w["w_qkvg"], w["w_u"], w["w_gate"], w["b_gate"], w["gla_gain"], w["pool_w"],
              w["pool_scale"], w["w_out"], w["norm_mem_q"], w["w_mq"], w["w_mo"], w["norm_moe"], w["w_router"],
              w["b_router"], tri2, ltri, band]
    in_specs = [row_spec, per_batch((GLA_WIDTH, GLA_KW)), per_batch((POOL_ROWS, POOL_WIDTH)),
                per_batch((N_MEM, D_MODEL)), per_batch((N_MEM, D_MODEL))] + [_const_spec(c.shape) for c in consts]
    out_specs = [row_spec, lane_spec, lane_spec, pl.BlockSpec((8, LANES), lambda b, l: (0, 0)),
                 per_batch((GLA_WIDTH, GLA_KW)), per_batch((POOL_ROWS, POOL_WIDTH))]
    out_shape = [jax.ShapeDtypeStruct((t, D_MODEL), F32), jax.ShapeDtypeStruct((t, LANES), I32),
                 jax.ShapeDtypeStruct((t, LANES), F32), jax.ShapeDtypeStruct((8, LANES), F32),
                 jax.ShapeDtypeStruct((batch, GLA_WIDTH, GLA_KW), F32),
                 jax.ShapeDtypeStruct((batch, POOL_ROWS, POOL_WIDTH), F32)]
    return pl.pallas_call(
        functools.partial(_premoe_body, hist, rows),
        grid=(batch, nl),
        in_specs=in_specs,
        out_specs=out_specs,
        out_shape=out_shape,
        scratch_shapes=[pltpu.VMEM((GLA_WIDTH, GLA_KW), F32), pltpu.VMEM((POOL_ROWS, POOL_WIDTH), F32),
                        pltpu.VMEM((8, LANES), F32)],
        compiler_params=pltpu.CompilerParams(dimension_semantics=("arbitrary", "arbitrary"),
                                             vmem_limit_bytes=VMEM_LIMIT),
        name="premoe",
    )(x2d, s0t, pool0, kmem, vmem, *consts)


def _route_tables(route_i, counts_f, tm, nb):
    e = route_i[:, :TOP_K]
    rank = route_i[:, TOP_K:2 * TOP_K]
    counts = counts_f[0, :N_EXPERTS].astype(I32)
    padded = ((counts + tm - 1) // tm) * tm
    pend = jnp.cumsum(padded)
    pstart = pend - padded
    eids = jnp.arange(N_EXPERTS, dtype=I32)
    dest = jnp.sum(jnp.where(e[:, :, None] == eids, pstart, 0), axis=-1) + rank
    blk_start = jnp.arange(nb, dtype=I32) * tm
    block_e = jnp.minimum(jnp.sum((pend[None, :] <= blk_start[:, None]).astype(I32), axis=1), N_EXPERTS - 1)
    n_used = (pend[-1] // tm).astype(I32).reshape(1)
    return dest.astype(I32), block_e.astype(I32), n_used, pend.astype(I32), counts


def _dispatch_body(rows, tm, nb, nu_ref, pend_ref, cnt_ref, dest_ref, x_ref, g_ref, xg_hbm, hbuf, zbuf, sem, zsem):
    i = pl.program_id(0)
    n_steps = pl.num_programs(0)
    slot = lax.rem(i, 2)
    nu = nu_ref[0]
    groups = rows // 8

    def tile(ref, row):
        return ref.at[pl.ds(pl.multiple_of(row * TILE_ROWS, TILE_ROWS), TILE_ROWS)]

    def zero_copy(start):
        return pltpu.make_async_copy(zbuf, xg_hbm.at[pl.ds(pl.multiple_of(start * TILE_ROWS, TILE_ROWS), tm * TILE_ROWS)],
                                     zsem)

    def for_each_fill(fn):
        def per_expert(e, c):
            @pl.when(cnt_ref[e] > 0)
            def _():
                fn(pend_ref[e] - tm)
            return c
        lax.fori_loop(0, N_EXPERTS, per_expert, 0)

        def per_tail(b, c):
            fn(b * tm)
            return c
        lax.fori_loop(nu, nb, per_tail, 0)

    @pl.when(i == 0)
    def _():
        zbuf[...] = jnp.zeros_like(zbuf)
        for_each_fill(lambda start: zero_copy(start).start())
        for_each_fill(lambda start: zero_copy(start).wait())

    def row_wait(s):
        for _ in range(TOP_K):
            pltpu.make_async_copy(hbuf.at[s], xg_hbm.at[pl.ds(0, rows * TILE_ROWS)], sem.at[s]).wait()

    @pl.when(i >= 2)
    def _():
        row_wait(slot)

    _store_token_tiles(hbuf.at[slot], _rms(x_ref[...], g_ref[...]))

    def send_group(gi, c):
        base = gi * (8 * TOP_K)
        for r in range(8):
            for kk in range(TOP_K):
                dst = dest_ref[0, 0, base + r * TOP_K + kk]
                pltpu.make_async_copy(tile(hbuf.at[slot], gi * 8 + r), tile(xg_hbm, dst),
                                      sem.at[slot]).start(priority=kk % 2)
        return c
    lax.fori_loop(0, groups, send_group, 0)

    @pl.when(i == n_steps - 1)
    def _():
        row_wait(slot)

        @pl.when(i >= 1)
        def _():
            row_wait(1 - slot)


def _dispatch(x2, dest, n_used, pend, counts, g, tm, nb):
    t = x2.shape[0]
    rows = min(DISPATCH_ROWS, t)
    steps = t // rows
    dest3 = dest.reshape(steps, 1, rows * TOP_K)
    grid_spec = pltpu.PrefetchScalarGridSpec(
        num_scalar_prefetch=3,
        grid=(steps,),
        in_specs=[
            pl.BlockSpec((1, 1, rows * TOP_K), lambda i, *_: (i, 0, 0), memory_space=pltpu.SMEM),
            pl.BlockSpec((rows, D_MODEL), lambda i, *_: (i, 0)),
            pl.BlockSpec((1, D_MODEL), lambda i, *_: (0, 0)),
        ],
        out_specs=pl.BlockSpec(memory_space=pl.ANY),
        scratch_shapes=[pltpu.VMEM((2, rows * TILE_ROWS, LANES), F32), pltpu.VMEM((tm * TILE_ROWS, LANES), F32),
                        pltpu.SemaphoreType.DMA((2,)), pltpu.SemaphoreType.DMA(())],
    )
    return pl.pallas_call(
        functools.partial(_dispatch_body, rows, tm, nb),
        grid_spec=grid_spec,
        out_shape=jax.ShapeDtypeStruct((nb * tm * TILE_ROWS, LANES), F32),
        compiler_params=pltpu.CompilerParams(dimension_semantics=("arbitrary",), vmem_limit_bytes=VMEM_LIMIT),
        name="dispatch",
    )(n_used, pend, counts, dest3, x2, g)


def _moe_body(tm, nu_ref, be_ref, xg_ref, wup_ref, bup_ref, wdn_ref, bdn_ref, ys_ref):
    del be_ref
    b = pl.program_id(0)

    @pl.when(b < nu_ref[0])
    def _():
        hh = _load_token_tiles(xg_ref, tm).astype(BF16)
        uu = jnp.dot(hh, wup_ref[0], preferred_element_type=F32) + bup_ref[0]
        glu = jnp.minimum(uu[:, :D_FF], SWIGLU_LIMIT)
        lin = jnp.clip(uu[:, D_FF:], -SWIGLU_LIMIT, SWIGLU_LIMIT)
        act = glu * jax.nn.sigmoid(SWIGLU_ALPHA * glu) * (lin + 1.0)
        _store_token_tiles(ys_ref, jnp.dot(act.astype(BF16), wdn_ref[0], preferred_element_type=F32) + bdn_ref[0])

    @pl.when(b >= nu_ref[0])
    def _():
        ys_ref[...] = jnp.zeros_like(ys_ref)


def _moe(xg, block_e, n_used, w, tm, nb):
    def used(b, nu):
        return jnp.minimum(b, nu[0] - 1)

    def per_expert(shape):
        return pl.BlockSpec((1,) + shape, lambda b, nu, be: (be[used(b, nu)], 0, 0))

    grid_spec = pltpu.PrefetchScalarGridSpec(
        num_scalar_prefetch=2,
        grid=(nb,),
        in_specs=[
            pl.BlockSpec((tm * TILE_ROWS, LANES), lambda b, nu, be: (used(b, nu), 0)),
            per_expert((D_MODEL, 2 * D_FF)), per_expert((1, 2 * D_FF)),
            per_expert((D_FF, D_MODEL)), per_expert((1, D_MODEL)),
        ],
        out_specs=pl.BlockSpec((tm * TILE_ROWS, LANES), lambda b, nu, be: (b, 0)),
    )
    return pl.pallas_call(
        functools.partial(_moe_body, tm),
        grid_spec=grid_spec,
        out_shape=jax.ShapeDtypeStruct((nb * tm * TILE_ROWS, LANES), F32),
        compiler_params=pltpu.CompilerParams(dimension_semantics=("arbitrary",), vmem_limit_bytes=VMEM_LIMIT),
        name="moe",
    )(n_used, block_e, xg, w["w_up"], w["b_up"], w["w_down"], w["b_down"])


def _combine_body(rows, destc_ref, destn_ref, x_ref, gate_ref, g_ref, ys_hbm, o_ref, gbuf, sem):
    i = pl.program_id(0)
    n_steps = pl.num_programs(0)
    slot = lax.rem(i, 2)
    groups = rows // 8

    def fetch(dest_ref, s):
        def fetch_group(gi, c):
            base = gi * (8 * TOP_K)
            srcs = [dest_ref[0, 0, base + j] for j in range(8 * TOP_K)]
            for r in range(8):
                for kk in range(TOP_K):
                    src = srcs[r * TOP_K + kk]
                    pltpu.make_async_copy(
                        ys_hbm.at[pl.ds(pl.multiple_of(src * TILE_ROWS, TILE_ROWS), TILE_ROWS)],
                        gbuf.at[s, kk, gi, :, r, :],
                        sem.at[s]).start(priority=kk % 2)
            return c
        lax.fori_loop(0, groups, fetch_group, 0)

    @pl.when(i == 0)
    def _():
        fetch(destc_ref, 0)

    @pl.when(i + 1 < n_steps)
    def _():
        fetch(destn_ref, 1 - slot)

    for kk in range(TOP_K):
        for r in range(8):
            pltpu.make_async_copy(ys_hbm.at[pl.ds(0, groups * TILE_ROWS)].reshape(groups, TILE_ROWS, LANES),
                                  gbuf.at[slot, kk, :, :, r, :], sem.at[slot]).wait()

    gate = gate_ref[...]
    acc = x_ref[...]
    for kk in range(TOP_K):
        acc = acc + _plain_rows(gbuf[slot, kk]) * gate[:, kk:kk + 1]
    o_ref[...] = _rms(acc, g_ref[...])


def _combine(x2, ys, dest, route_g, g):
    t = x2.shape[0]
    rows = min(COMBINE_ROWS, t)
    steps = t // rows
    dest3 = dest.reshape(steps, 1, rows * TOP_K)
    smem_cur = pl.BlockSpec((1, 1, rows * TOP_K), lambda i: (i, 0, 0), memory_space=pltpu.SMEM)
    smem_next = pl.BlockSpec((1, 1, rows * TOP_K), lambda i: (jnp.minimum(i + 1, steps - 1), 0, 0),
                             memory_space=pltpu.SMEM)
    return pl.pallas_call(
        functools.partial(_combine_body, rows),
        grid=(steps,),
        in_specs=[smem_cur, smem_next,
                  pl.BlockSpec((rows, D_MODEL), lambda i: (i, 0)),
                  pl.BlockSpec((rows, LANES), lambda i: (i, 0)),
                  pl.BlockSpec((1, D_MODEL), lambda i: (0, 0)),
                  pl.BlockSpec(memory_space=pl.ANY)],
        out_specs=pl.BlockSpec((rows, D_MODEL), lambda i: (i, 0)),
        out_shape=jax.ShapeDtypeStruct((t, D_MODEL), F32),
        scratch_shapes=[pltpu.VMEM((2, TOP_K, rows // 8, TILE_ROWS, 8, LANES), F32), pltpu.SemaphoreType.DMA((2,))],
        compiler_params=pltpu.CompilerParams(dimension_semantics=("arbitrary",), vmem_limit_bytes=VMEM_LIMIT),
        name="combine",
    )(dest3, dest3, x2, route_g, g, ys)


def _moe_layer(x2, route_i, route_g, counts_f, w, norm_final):
    t = x2.shape[0]
    tm = MOE_ROWS
    nb = (t * TOP_K + N_EXPERTS * (tm - 1) + tm - 1) // tm
    dest, block_e, n_used, pend, counts = _route_tables(route_i, counts_f, tm, nb)
    xg = _dispatch(x2, dest, n_used, pend, counts, w["norm_moe"], tm, nb)
    ys = _moe(xg, block_e, n_used, w, tm, nb)
    return _combine(x2, ys, dest, route_g, norm_final.reshape(1, -1))


def _state_to_t(s):
    bsz = s.shape[0]
    st = jnp.zeros((bsz, GLA_HEADS, GLA_DV, GLA_HEADS, GLA_DK), F32)
    for hh in range(GLA_HEADS):
        st = st.at[:, hh, :, hh, :].set(jnp.swapaxes(s[:, hh], 1, 2).astype(F32))
    return st.reshape(bsz, GLA_WIDTH, GLA_KW)


def _state_from_t(st):
    bsz = st.shape[0]
    s5 = st.reshape(bsz, GLA_HEADS, GLA_DV, GLA_HEADS, GLA_DK)
    return jnp.stack([jnp.swapaxes(s5[:, hh, :, hh, :], 1, 2) for hh in range(GLA_HEADS)], axis=1)


def _prep_weights(norm_mix, w_in, w_gate, b_gate, gla_gain, pool_w, pool_scale, w_out, norm_mem_q, w_mq, w_mo,
                  norm_moe, w_router, b_router, w_up, b_up, w_down, b_down):
    n_qkvg = 2 * GLA_KW + 2 * GLA_WIDTH
    w_qkvg = jnp.zeros((D_MODEL, n_qkvg + LANES), BF16).at[:, :n_qkvg + GATE_RANK].set(
        w_in[:, :n_qkvg + GATE_RANK].astype(BF16))
    w_g = jnp.zeros((LANES, GLA_KW), BF16).at[:GATE_RANK].set(w_gate.astype(BF16))
    w_rt = jnp.zeros((D_MODEL, LANES), BF16).at[:, :N_EXPERTS].set(w_router.astype(BF16))
    b_rt = jnp.full((1, LANES), NEG_BIG, F32).at[0, :N_EXPERTS].set(b_router)
    return {
        "norm_mix": norm_mix.reshape(1, -1),
        "w_qkvg": w_qkvg,
        "w_u": w_in[:, n_qkvg + GATE_RANK:].astype(BF16),
        "w_gate": w_g,
        "b_gate": b_gate.reshape(1, -1),
        "gla_gain": gla_gain.reshape(1, -1),
        "pool_w": pool_w.astype(BF16),
        "pool_scale": pool_scale.reshape(1, -1),
        "w_out": w_out.astype(BF16),
        "norm_mem_q": norm_mem_q.reshape(1, -1),
        "w_mq": w_mq.astype(BF16),
        "w_mo": w_mo.astype(BF16),
        "norm_moe": norm_moe.reshape(1, -1),
        "w_router": w_rt,
        "b_router": b_rt,
        "w_up": w_up.astype(BF16),
        "b_up": b_up.reshape(N_EXPERTS, 1, -1),
        "w_down": w_down.astype(BF16),
        "b_down": b_down.reshape(N_EXPERTS, 1, -1),
    }


def _block(x, s_gla, pool_prev, hist_len, mk, mv, w, norm_final):
    batch, seq, _ = x.shape
    x2d = x.reshape(batch * seq, D_MODEL)
    pool0 = jnp.concatenate([jnp.zeros((batch, 1, POOL_WIDTH), F32), pool_prev.astype(F32)], axis=1)
    x2, ri, rg, cnt, st_t, pool_t = _premoe(x2d, batch, seq, hist_len, _state_to_t(s_gla), pool0,
                                            mk.reshape(batch, N_MEM, D_MODEL), mv.reshape(batch, N_MEM, D_MODEL), w)
    out = _moe_layer(x2, ri, rg, cnt, w, norm_final)
    return out.reshape(batch, seq, D_MODEL), _state_from_t(st_t), pool_t[:, 1:, :]


def kernel(x_prompt, x_sample, mem_prompt, state_gla, state_pool, cache_mem_k, cache_mem_v, norm_mix, w_in, w_gate, b_gate, gla_gain, pool_w, pool_scale, w_out, norm_mem_q, norm_mem_kv, w_mq, w_mk, w_mv, w_mo, norm_moe, w_router, b_router, w_up, b_up, w_down, b_down, norm_final):
    depth = w_in.shape[0]
    assert depth == 1
    xp, xs = x_prompt, x_sample
    bp = xp.shape[0]
    gla_p, pool_p, mk_p, mv_p, gla_s, pool_s = [], [], [], [], [], []
    for l in range(depth):
        w = _prep_weights(norm_mix[l], w_in[l], w_gate[l], b_gate[l], gla_gain[l], pool_w[l], pool_scale[l], w_out[l],
                          norm_mem_q[l], w_mq[l], w_mo[l], norm_moe[l], w_router[l], b_router[l],
                          w_up[l], b_up[l], w_down[l], b_down[l])
        mk2, mv2 = _mem_kv(mem_prompt.reshape(bp * N_MEM, D_MODEL), norm_mem_kv[l].reshape(1, -1),
                           w_mk[l].astype(BF16), w_mv[l].astype(BF16))
        mk = mk2.reshape(bp, N_MEM, MEM_HEADS, MEM_HD)
        mv = mv2.reshape(bp, N_MEM, MEM_HEADS, MEM_HD)
        xs, ss, ps = _block(xs, state_gla[l], state_pool[l], PAST_LEN, cache_mem_k[l], cache_mem_v[l], w, norm_final)
        gla_s.append(ss)
        pool_s.append(ps)
        s0 = jnp.zeros((bp, GLA_HEADS, GLA_DK, GLA_DV), F32)
        p0 = jnp.zeros((bp, POOL_BUF, POOL_WIDTH), F32)
        xp, sp, pp = _block(xp, s0, p0, 0, mk, mv, w, norm_final)
        gla_p.append(sp)
        pool_p.append(pp)
        mk_p.append(mk)
        mv_p.append(mv)
    return (xp, xs, jnp.stack(gla_p), jnp.stack(pool_p), jnp.stack(mk_p), jnp.stack(mv_p),
            jnp.stack(gla_s), jnp.stack(pool_s))
```

```python
import functools

import numpy as np
import jax
import jax.numpy as jnp
from jax import lax
from jax.experimental import pallas as pl
from jax.experimental.pallas import tpu as pltpu

F32 = jnp.float32
BF16 = jnp.bfloat16
I32 = jnp.int32

D_MODEL = 1024
CHUNK = 64
GLA_HEADS = 4
GLA_DK = 64
GLA_DV = 128
GLA_KW = GLA_HEADS * GLA_DK
GLA_WIDTH = GLA_HEADS * GLA_DV
GATE_RANK = 16
GATE_TAU = 16.0
POOL_WIDTH = 512
POOL_GC = 128
POOL_WINDOWS = (2, 4, 8, 16)
POOL_BUF = 15
PAST_LEN = 4096
POOL_ROWS = 16
N_MEM = 256
MEM_HEADS = 4
MEM_HD = 256
N_EXPERTS = 32
TOP_K = 4
D_FF = 1024
SWIGLU_ALPHA = 1.702
SWIGLU_LIMIT = 7.0
EPS = 1e-6
LANES = 128
TILE_ROWS = D_MODEL // LANES
ROUTE_EXPERT, ROUTE_RANK, ROUTE_GATE = 0, 4, 8

PREMOE_ROWS = 512
PREMOE_SUB = 256
MOE_ROWS = 512
DISPATCH_ROWS = 256
COMBINE_ROWS = 256
VMEM_LIMIT = 56 * 1024 * 1024

NT_DIMS = (((1,), (1,)), ((), ()))
TN_DIMS = (((0,), (0,)), ((), ()))


def _rms(x, g):
    ms = jnp.mean(x * x, axis=-1, keepdims=True)
    return x * lax.rsqrt(ms + EPS) * g


def _load_token_tiles(ref, rows):
    return jnp.concatenate([ref[pl.ds(s, rows, stride=TILE_ROWS), :] for s in range(TILE_ROWS)], axis=1)


def _plain_rows(v):
    g = v.shape[0]
    return jnp.concatenate([v[:, c].reshape(g * 8, LANES) for c in range(TILE_ROWS)], axis=1)


def _store_token_tiles(ref, val):
    rows = val.shape[0]
    for s in range(TILE_ROWS):
        ref[pl.ds(s, rows, stride=TILE_ROWS), :] = val[:, s * LANES:(s + 1) * LANES]


def _const_spec(shape):
    nd = len(shape)
    return pl.BlockSpec(shape, lambda *_: (0,) * nd, pipeline_mode=pl.Buffered(1))


def _memkv_body(m_ref, g_ref, wk_ref, wv_ref, k_ref, v_ref):
    m = _rms(m_ref[...], g_ref[...]).astype(BF16)
    k_ref[...] = jnp.dot(m, wk_ref[...], preferred_element_type=F32)
    v_ref[...] = jnp.dot(m, wv_ref[...], preferred_element_type=F32)


def _mem_kv(mem2d, g, wk, wv):
    n = mem2d.shape[0]
    tm = 512
    row = pl.BlockSpec((tm, D_MODEL), lambda i: (i, 0))
    return pl.pallas_call(
        _memkv_body,
        grid=(n // tm,),
        in_specs=[row, _const_spec((1, D_MODEL)), _const_spec((D_MODEL, D_MODEL)), _const_spec((D_MODEL, D_MODEL))],
        out_specs=[row, row],
        out_shape=[jax.ShapeDtypeStruct((n, D_MODEL), F32)] * 2,
        compiler_params=pltpu.CompilerParams(dimension_semantics=("arbitrary",), vmem_limit_bytes=VMEM_LIMIT),
        name="mem_kv",
    )(mem2d, g, wk, wv)


def _premoe_body(hist, rows,
                 x_ref, s0_ref, p0_ref, k_ref, v_ref,
                 nmix_ref, wqkvg_ref, wu_ref, wgate_ref, bgate_ref, ggain_ref, poolw_ref, pscale_ref,
                 wout_ref, nq_ref, wmq_ref, wmo_ref, nmoe_ref, wrt_ref, brt_ref,
                 tri2_ref, utri_ref, band_ref,
                 x2_ref, route_ref, cnt_ref, sto_ref, po_ref,
                 st_sc, prev_sc, cnt_sc):
    b = pl.program_id(0)
    l = pl.program_id(1)
    n_chunks = rows // CHUNK
    sub = min(rows, PREMOE_SUB)
    subs = [slice(s0, s0 + sub) for s0 in range(0, rows, sub)]

    @pl.when(l == 0)
    def _():
        st_sc[...] = s0_ref[0]
        prow = lax.broadcasted_iota(I32, (POOL_ROWS, POOL_WIDTH), 0)
        prev_sc[...] = jnp.where(prow >= POOL_ROWS - hist, p0_ref[0], 0.0)

    @pl.when((b == 0) & (l == 0))
    def _():
        cnt_sc[...] = jnp.zeros_like(cnt_sc)

    x = x_ref[...]
    h = _rms(x, nmix_ref[...]).astype(BF16)
    z = jnp.dot(h, wqkvg_ref[...], preferred_element_type=F32)
    q = z[:, 0:GLA_KW] * (GLA_DK ** -0.5)
    k = z[:, GLA_KW:2 * GLA_KW]
    v = z[:, 2 * GLA_KW:2 * GLA_KW + GLA_WIDTH]
    g = z[:, 2 * GLA_KW + GLA_WIDTH:2 * GLA_KW + 2 * GLA_WIDTH]
    r = z[:, 2 * GLA_KW + 2 * GLA_WIDTH:]

    gp = jnp.dot(r.astype(BF16), wgate_ref[...], preferred_element_type=F32) + bgate_ref[...]
    la = jax.nn.log_sigmoid(gp) * (1.0 / GATE_TAU)

    hi = la.astype(BF16)
    r1 = la - hi.astype(F32)
    mid = r1.astype(BF16)
    lo = (r1 - mid.astype(F32)).astype(BF16)
    la3 = jnp.concatenate([hi, mid, lo], axis=1)
    bcum_parts, btot_parts = [], []
    for sl in subs:
        bb = jnp.dot(tri2_ref[...], la3[sl], preferred_element_type=F32)
        bsum = (bb[:, 0:GLA_KW] + bb[:, GLA_KW:2 * GLA_KW]) + bb[:, 2 * GLA_KW:]
        bcum_parts.append(bsum[:sub])
        btot_parts.append(bsum[sub:])
    bcum = jnp.concatenate(bcum_parts, axis=0)
    btot = jnp.concatenate(btot_parts, axis=0)

    qd = q * jnp.exp(bcum)
    kd = k * jnp.exp(-bcum)
    kl = k * jnp.exp(btot - bcum)

    lane_kw = lax.broadcasted_iota(I32, (rows, GLA_KW), 1)
    rowi = lax.broadcasted_iota(I32, (sub, sub), 0)
    coli = lax.broadcasted_iota(I32, (sub, sub), 1)
    amask = (coli <= rowi) & (coli >= (rowi & ~(CHUNK - 1)))
    vb = v.astype(BF16)

    o_heads = []
    for hh in range(GLA_HEADS):
        mh = (lane_kw >> 6) == hh
        qh = jnp.where(mh, qd, 0.0).astype(BF16)
        kh = jnp.where(mh, kd, 0.0).astype(BF16)
        o_sub = []
        for sl in subs:
            a = lax.dot_general(qh[sl], kh[sl], NT_DIMS, preferred_element_type=F32)
            a = jnp.where(amask, a, 0.0).astype(BF16)
            o_sub.append(jnp.dot(a, vb[sl, hh * GLA_DV:(hh + 1) * GLA_DV], preferred_element_type=F32))
        o_heads.append(jnp.concatenate(o_sub, axis=0))
    o_intra = jnp.concatenate(o_heads, axis=1)

    srow = lax.broadcasted_iota(I32, (GLA_WIDTH, GLA_KW), 0)
    scol = lax.broadcasted_iota(I32, (GLA_WIDTH, GLA_KW), 1)
    smask = (srow >> 7) == (scol >> 6)
    qdb = qd.astype(BF16)
    klb = kl.astype(BF16)
    st = st_sc[...]
    oi_parts = []
    for n in range(n_chunks):
        lo_r, hi_r = n * CHUNK, (n + 1) * CHUNK
        oi_parts.append(lax.dot_general(qdb[lo_r:hi_r], st.astype(BF16), NT_DIMS, preferred_element_type=F32))
        upd = lax.dot_general(vb[lo_r:hi_r], klb[lo_r:hi_r], TN_DIMS, preferred_element_type=F32)
        dec = jnp.exp(btot[lo_r:lo_r + 1, :])
        st = st * dec + jnp.where(smask, upd, 0.0)
    st_sc[...] = st
    sto_ref[0] = st
    o_inter = oi_parts[0] if n_chunks == 1 else jnp.concatenate(oi_parts, axis=0)
    o = o_intra + o_inter

    gains = ggain_ref[...]
    on_parts = []
    for hh in range(GLA_HEADS):
        oh = o[:, hh * GLA_DV:(hh + 1) * GLA_DV]
        ms = jnp.mean(oh * oh, axis=-1, keepdims=True)
        on_parts.append(oh * lax.rsqrt(ms + EPS) * gains[:, hh * GLA_DV:(hh + 1) * GLA_DV])
    on = jnp.concatenate(on_parts, axis=1) * (g * jax.nn.sigmoid(g))

    u = jnp.dot(h, wu_ref[...], preferred_element_type=F32)
    prevs = [prev_sc[...]] + [u[sl.stop - POOL_ROWS:sl.stop] for sl in subs[:-1]]
    exts = [jnp.concatenate([pv, u[sl]], axis=0).astype(BF16) for pv, sl in zip(prevs, subs)]
    pos = l * rows + lax.broadcasted_iota(I32, (rows, POOL_GC), 0)
    p_parts = []
    for gi, w in enumerate(POOL_WINDOWS):
        cols = slice(gi * POOL_GC, (gi + 1) * POOL_GC)
        s = jnp.concatenate([jnp.dot(band_ref[gi], ext[:, cols], preferred_element_type=F32) for ext in exts], axis=0)
        cnt_w = jnp.minimum(w, pos + 1 + hist).astype(F32)
        dd = s / cnt_w - u[:, cols]
        p_parts.append(jnp.dot(dd.astype(BF16), poolw_ref[gi], preferred_element_type=F32))
    p = jnp.concatenate(p_parts, axis=1) * pscale_ref[...]
    tail = u[rows - POOL_ROWS:rows, :]
    prev_sc[...] = tail
    po_ref[0] = tail

    cat = jnp.concatenate([on, p], axis=1).astype(BF16)
    x1 = x + jnp.dot(cat, wout_ref[...], preferred_element_type=F32)

    h2 = _rms(x1, nq_ref[...]).astype(BF16)
    qm = jnp.dot(h2, wmq_ref[...], preferred_element_type=F32).astype(BF16)
    kk = k_ref[0].astype(BF16)
    vv = v_ref[0].astype(BF16)
    a_parts = []
    for hh in range(MEM_HEADS):
        cols = slice(hh * MEM_HD, (hh + 1) * MEM_HD)
        s = lax.dot_general(qm[:, cols], kk[:, cols], NT_DIMS, preferred_element_type=F32) * (MEM_HD ** -0.5)
        e = jnp.exp(s - jnp.max(s, axis=-1, keepdims=True))
        pr = e / jnp.sum(e, axis=-1, keepdims=True)
        a_parts.append(jnp.dot(pr.astype(BF16), vv[:, cols], preferred_element_type=F32))
    att = jnp.concatenate(a_parts, axis=1).astype(BF16)
    x2 = x1 + jnp.dot(att, wmo_ref[...], preferred_element_type=F32)
    x2_ref[...] = x2

    rp = max(rows, LANES)
    h3 = _rms(x2, nmoe_ref[...]).astype(BF16)
    if rp > rows:
        h3 = jnp.concatenate([h3, jnp.zeros((rp - rows, D_MODEL), BF16)], axis=0)
    logits = lax.dot_general(wrt_ref[...], h3, NT_DIMS, preferred_element_type=F32) + brt_ref[...]
    eid = lax.broadcasted_iota(I32, (N_EXPERTS, rp), 0).astype(F32)
    work = logits
    vals, idxs = [], []
    for _ in range(TOP_K):
        m = jnp.max(work, axis=0, keepdims=True)
        idx = jnp.min(jnp.where(work == m, eid, float(N_EXPERTS)), axis=0, keepdims=True)
        vals.append(m)
        idxs.append(idx)
        work = jnp.where(eid == idx, -jnp.inf, work)
    exps = [jnp.exp(vk - vals[0]) for vk in vals]
    den = (exps[0] + exps[1]) + (exps[2] + exps[3])
    gates = [ek / den for ek in exps]

    valid = lax.broadcasted_iota(I32, (N_EXPERTS, rp), 1) < rows
    hot = jnp.zeros((N_EXPERTS, rp), F32)
    for idx in idxs:
        hot = hot + jnp.where((eid == idx) & valid, 1.0, 0.0)
    prefix = jnp.dot(hot.astype(BF16), utri_ref[...], preferred_element_type=F32) + cnt_sc[:, 0:1]
    ranks = [jnp.sum(jnp.where(eid == idx, prefix, 0.0), axis=0, keepdims=True) for idx in idxs]
    cnt_sc[...] = cnt_sc[...] + jnp.sum(hot, axis=1, keepdims=True)
    cnt_ref[...] = cnt_sc[...]

    pieces = idxs + ranks + gates
    rid = lax.broadcasted_iota(I32, (2 * 8, rp), 0)
    rec = jnp.zeros((2 * 8, rp), F32)
    for j, piece in enumerate(pieces):
        rec = jnp.where(rid == j, piece, rec)
    rec = jnp.concatenate([rec, jnp.zeros((LANES - 2 * 8, rp), F32)], axis=0)
    route_ref[...] = rec.T[:rows]


def _premoe_consts(rows):
    sub = min(rows, PREMOE_SUB)
    i = np.arange(sub)[:, None]
    j = np.arange(sub)[None, :]
    same = (i // CHUNK) == (j // CHUNK)
    tri = (same & (j <= i)).astype(np.float32)
    tot = same.astype(np.float32)
    tri2 = np.concatenate([tri, tot], axis=0)
    je = np.arange(sub + POOL_ROWS)[None, :]
    band = np.stack([((je <= i + POOL_ROWS) & (je > i + POOL_ROWS - w)).astype(np.float32) for w in POOL_WINDOWS])
    rp = max(rows, LANES)
    utri = (np.arange(rp)[:, None] < np.arange(rp)[None, :]).astype(np.float32)
    return jnp.asarray(tri2, BF16), jnp.asarray(utri, BF16), jnp.asarray(band, BF16)


def _premoe(x2d, batch, seq, hist_len, s0t, pool0, kmem, vmem, w):
    rows = min(PREMOE_ROWS, seq)
    nl = seq // rows
    t = batch * seq
    hist = min(int(hist_len), POOL_BUF)
    tri2, utri, band = _premoe_consts(rows)
    row_spec = pl.BlockSpec((rows, D_MODEL), lambda b, l: (b * nl + l, 0))
    lane_spec = pl.BlockSpec((rows, LANES), lambda b, l: (b * nl + l, 0))

    def per_batch(shape):
        return pl.BlockSpec((1,) + shape, lambda b, l: (b, 0, 0))

    consts = [w["norm_mix"], w["w_qkvg"], w["w_u"], w["w_gate"], w["b_gate"], w["gla_gain"], w["pool_w"],
              w["pool_scale"], w["w_out"], w["norm_mem_q"], w["w_mq"], w["w_mo"], w["norm_moe"], w["w_router"],
              w["b_router"], tri2, utri, band]
    in_specs = [row_spec, per_batch((GLA_WIDTH, GLA_KW)), per_batch((POOL_ROWS, POOL_WIDTH)),
                per_batch((N_MEM, D_MODEL)), per_batch((N_MEM, D_MODEL))] + [_const_spec(c.shape) for c in consts]
    out_specs = [row_spec, lane_spec, pl.BlockSpec((N_EXPERTS, LANES), lambda b, l: (0, 0)),
                 per_batch((GLA_WIDTH, GLA_KW)), per_batch((POOL_ROWS, POOL_WIDTH))]
    out_shape = [jax.ShapeDtypeStruct((t, D_MODEL), F32), jax.ShapeDtypeStruct((t, LANES), F32),
                 jax.ShapeDtypeStruct((N_EXPERTS, LANES), F32),
                 jax.ShapeDtypeStruct((batch, GLA_WIDTH, GLA_KW), F32),
                 jax.ShapeDtypeStruct((batch, POOL_ROWS, POOL_WIDTH), F32)]
    return pl.pallas_call(
        functools.partial(_premoe_body, hist, rows),
        grid=(batch, nl),
        in_specs=in_specs,
        out_specs=out_specs,
        out_shape=out_shape,
        scratch_shapes=[pltpu.VMEM((GLA_WIDTH, GLA_KW), F32), pltpu.VMEM((POOL_ROWS, POOL_WIDTH), F32),
                        pltpu.VMEM((N_EXPERTS, LANES), F32)],
        compiler_params=pltpu.CompilerParams(dimension_semantics=("arbitrary", "arbitrary"),
                                             vmem_limit_bytes=VMEM_LIMIT),
        name="premoe",
    )(x2d, s0t, pool0, kmem, vmem, *consts)


def _route_tables(route, counts_f, tm, nb):
    e = route[:, ROUTE_EXPERT:ROUTE_EXPERT + TOP_K].astype(I32)
    rank = route[:, ROUTE_RANK:ROUTE_RANK + TOP_K].astype(I32)
    counts = counts_f[:, 0].astype(I32)
    padded = ((counts + tm - 1) // tm) * tm
    pend = jnp.cumsum(padded)
    pstart = pend - padded
    eids = jnp.arange(N_EXPERTS, dtype=I32)
    dest = jnp.sum(jnp.where(e[:, :, None] == eids, pstart, 0), axis=-1) + rank
    blk_start = jnp.arange(nb, dtype=I32) * tm
    block_e = jnp.minimum(jnp.sum((pend[None, :] <= blk_start[:, None]).astype(I32), axis=1), N_EXPERTS - 1)
    n_used = (pend[-1] // tm).astype(I32).reshape(1)
    return dest.astype(I32), block_e.astype(I32), n_used, pend.astype(I32), counts


def _dispatch_body(rows, tm, nb, nu_ref, pend_ref, cnt_ref, dest_ref, x_ref, g_ref, xg_hbm, hbuf, zbuf, sem, zsem):
    i = pl.program_id(0)
    n_steps = pl.num_programs(0)
    slot = lax.rem(i, 2)
    nu = nu_ref[0]
    groups = rows // 8

    def tile(ref, row):
        return ref.at[pl.ds(pl.multiple_of(row * TILE_ROWS, TILE_ROWS), TILE_ROWS)]

    def zero_copy(start):
        return pltpu.make_async_copy(zbuf, xg_hbm.at[pl.ds(pl.multiple_of(start * TILE_ROWS, TILE_ROWS), tm * TILE_ROWS)],
                                     zsem)

    def for_each_fill(fn):
        def per_expert(e, c):
            @pl.when(cnt_ref[e] > 0)
            def _():
                fn(pend_ref[e] - tm)
            return c
        lax.fori_loop(0, N_EXPERTS, per_expert, 0)

        def per_tail(b, c):
            fn(b * tm)
            return c
        lax.fori_loop(nu, nb, per_tail, 0)

    @pl.when(i == 0)
    def _():
        zbuf[...] = jnp.zeros_like(zbuf)
        for_each_fill(lambda start: zero_copy(start).start())
        for_each_fill(lambda start: zero_copy(start).wait())

    def row_wait(s):
        for _ in range(TOP_K):
            pltpu.make_async_copy(hbuf.at[s], xg_hbm.at[pl.ds(0, rows * TILE_ROWS)], sem.at[s]).wait()

    @pl.when(i >= 2)
    def _():
        row_wait(slot)

    _store_token_tiles(hbuf.at[slot], _rms(x_ref[...], g_ref[...]))

    def send_group(gi, c):
        base = gi * (8 * TOP_K)
        for r in range(8):
            for kk in range(TOP_K):
                dst = dest_ref[0, 0, base + r * TOP_K + kk]
                pltpu.make_async_copy(tile(hbuf.at[slot], gi * 8 + r), tile(xg_hbm, dst),
                                      sem.at[slot]).start(priority=kk % 2)
        return c
    lax.fori_loop(0, groups, send_group, 0)

    @pl.when(i == n_steps - 1)
    def _():
        row_wait(slot)

        @pl.when(i >= 1)
        def _():
            row_wait(1 - slot)


def _dispatch(x2, dest, n_used, pend, counts, g, tm, nb):
    t = x2.shape[0]
    rows = min(DISPATCH_ROWS, t)
    steps = t // rows
    dest3 = dest.reshape(steps, 1, rows * TOP_K)
    grid_spec = pltpu.PrefetchScalarGridSpec(
        num_scalar_prefetch=3,
        grid=(steps,),
        in_specs=[
            pl.BlockSpec((1, 1, rows * TOP_K), lambda i, *_: (i, 0, 0), memory_space=pltpu.SMEM),
            pl.BlockSpec((rows, D_MODEL), lambda i, *_: (i, 0)),
            pl.BlockSpec((1, D_MODEL), lambda i, *_: (0, 0)),
        ],
        out_specs=pl.BlockSpec(memory_space=pl.ANY),
        scratch_shapes=[pltpu.VMEM((2, rows * TILE_ROWS, LANES), F32), pltpu.VMEM((tm * TILE_ROWS, LANES), F32),
                        pltpu.SemaphoreType.DMA((2,)), pltpu.SemaphoreType.DMA(())],
    )
    return pl.pallas_call(
        functools.partial(_dispatch_body, rows, tm, nb),
        grid_spec=grid_spec,
        out_shape=jax.ShapeDtypeStruct((nb * tm * TILE_ROWS, LANES), F32),
        compiler_params=pltpu.CompilerParams(dimension_semantics=("arbitrary",), vmem_limit_bytes=VMEM_LIMIT),
        name="dispatch",
    )(n_used, pend, counts, dest3, x2, g)


def _moe_body(tm, nu_ref, be_ref, xg_ref, wup_ref, bup_ref, wdn_ref, bdn_ref, ys_ref, wup_bf, wdn_bf):
    b = pl.program_id(0)
    in_use = b < nu_ref[0]

    @pl.when(in_use & ((b == 0) | (be_ref[b] != be_ref[jnp.maximum(b - 1, 0)])))
    def _():
        wup_bf[...] = wup_ref[0].astype(BF16)
        wdn_bf[...] = wdn_ref[0].astype(BF16)

    @pl.when(in_use)
    def _():
        hh = _load_token_tiles(xg_ref, tm).astype(BF16)
        uu = jnp.dot(hh, wup_bf[...], preferred_element_type=F32) + bup_ref[0]
        glu = jnp.minimum(uu[:, :D_FF], SWIGLU_LIMIT)
        lin = jnp.clip(uu[:, D_FF:], -SWIGLU_LIMIT, SWIGLU_LIMIT)
        act = glu * jax.nn.sigmoid(SWIGLU_ALPHA * glu) * (lin + 1.0)
        _store_token_tiles(ys_ref, jnp.dot(act.astype(BF16), wdn_bf[...], preferred_element_type=F32) + bdn_ref[0])

    @pl.when(b >= nu_ref[0])
    def _():
        ys_ref[...] = jnp.zeros_like(ys_ref)


def _moe(xg, block_e, n_used, w, tm, nb):
    def used(b, nu):
        return jnp.minimum(b, nu[0] - 1)

    def per_expert(shape):
        return pl.BlockSpec((1,) + shape, lambda b, nu, be: (be[used(b, nu)], 0, 0))

    grid_spec = pltpu.PrefetchScalarGridSpec(
        num_scalar_prefetch=2,
        grid=(nb,),
        in_specs=[
            pl.BlockSpec((tm * TILE_ROWS, LANES), lambda b, nu, be: (used(b, nu), 0)),
            per_expert((D_MODEL, 2 * D_FF)), per_expert((1, 2 * D_FF)),
            per_expert((D_FF, D_MODEL)), per_expert((1, D_MODEL)),
        ],
        out_specs=pl.BlockSpec((tm * TILE_ROWS, LANES), lambda b, nu, be: (b, 0)),
        scratch_shapes=[pltpu.VMEM((D_MODEL, 2 * D_FF), BF16), pltpu.VMEM((D_FF, D_MODEL), BF16)],
    )
    return pl.pallas_call(
        functools.partial(_moe_body, tm),
        grid_spec=grid_spec,
        out_shape=jax.ShapeDtypeStruct((nb * tm * TILE_ROWS, LANES), F32),
        compiler_params=pltpu.CompilerParams(dimension_semantics=("arbitrary",), vmem_limit_bytes=VMEM_LIMIT),
        name="moe",
    )(n_used, block_e, xg, w["w_up"], w["b_up"], w["w_down"], w["b_down"])


def _combine_body(rows, destc_ref, destn_ref, x_ref, gate_ref, g_ref, ys_hbm, o_ref, gbuf, sem):
    i = pl.program_id(0)
    n_steps = pl.num_programs(0)
    slot = lax.rem(i, 2)
    groups = rows // 8

    def fetch(dest_ref, s):
        def fetch_group(gi, c):
            base = gi * (8 * TOP_K)
            srcs = [dest_ref[0, 0, base + j] for j in range(8 * TOP_K)]
            for r in range(8):
                for kk in range(TOP_K):
                    src = srcs[r * TOP_K + kk]
                    pltpu.make_async_copy(
                        ys_hbm.at[pl.ds(pl.multiple_of(src * TILE_ROWS, TILE_ROWS), TILE_ROWS)],
                        gbuf.at[s, kk, gi, :, r, :],
                        sem.at[s]).start(priority=kk % 2)
            return c
        lax.fori_loop(0, groups, fetch_group, 0)

    @pl.when(i == 0)
    def _():
        fetch(destc_ref, 0)

    @pl.when(i + 1 < n_steps)
    def _():
        fetch(destn_ref, 1 - slot)

    for kk in range(TOP_K):
        for r in range(8):
            pltpu.make_async_copy(ys_hbm.at[pl.ds(0, groups * TILE_ROWS)].reshape(groups, TILE_ROWS, LANES),
                                  gbuf.at[slot, kk, :, :, r, :], sem.at[slot]).wait()

    gate = gate_ref[...]
    acc = x_ref[...]
    for kk in range(TOP_K):
        acc = acc + _plain_rows(gbuf[slot, kk]) * gate[:, ROUTE_GATE + kk:ROUTE_GATE + kk + 1]
    o_ref[...] = _rms(acc, g_ref[...])


def _combine(x2, ys, dest, route, g):
    t = x2.shape[0]
    rows = min(COMBINE_ROWS, t)
    steps = t // rows
    dest3 = dest.reshape(steps, 1, rows * TOP_K)
    smem_cur = pl.BlockSpec((1, 1, rows * TOP_K), lambda i: (i, 0, 0), memory_space=pltpu.SMEM)
    smem_next = pl.BlockSpec((1, 1, rows * TOP_K), lambda i: (jnp.minimum(i + 1, steps - 1), 0, 0),
                             memory_space=pltpu.SMEM)
    return pl.pallas_call(
        functools.partial(_combine_body, rows),
        grid=(steps,),
        in_specs=[smem_cur, smem_next,
                  pl.BlockSpec((rows, D_MODEL), lambda i: (i, 0)),
                  pl.BlockSpec((rows, LANES), lambda i: (i, 0)),
                  pl.BlockSpec((1, D_MODEL), lambda i: (0, 0)),
                  pl.BlockSpec(memory_space=pl.ANY)],
        out_specs=pl.BlockSpec((rows, D_MODEL), lambda i: (i, 0)),
        out_shape=jax.ShapeDtypeStruct((t, D_MODEL), F32),
        scratch_shapes=[pltpu.VMEM((2, TOP_K, rows // 8, TILE_ROWS, 8, LANES), F32), pltpu.SemaphoreType.DMA((2,))],
        compiler_params=pltpu.CompilerParams(dimension_semantics=("arbitrary",), vmem_limit_bytes=VMEM_LIMIT),
        name="combine",
    )(dest3, dest3, x2, route, g, ys)


def _moe_layer(x2, route, counts_f, w, norm_final):
    t = x2.shape[0]
    tm = MOE_ROWS
    nb = (t * TOP_K + N_EXPERTS * (tm - 1) + tm - 1) // tm
    dest, block_e, n_used, pend, counts = _route_tables(route, counts_f, tm, nb)
    xg = _dispatch(x2, dest, n_used, pend, counts, w["norm_moe"], tm, nb)
    ys = _moe(xg, block_e, n_used, w, tm, nb)
    return _combine(x2, ys, dest, route, norm_final.reshape(1, -1))


def _state_to_t(s):
    bsz = s.shape[0]
    st = jnp.zeros((bsz, GLA_HEADS, GLA_DV, GLA_HEADS, GLA_DK), F32)
    for hh in range(GLA_HEADS):
        st = st.at[:, hh, :, hh, :].set(jnp.swapaxes(s[:, hh], 1, 2).astype(F32))
    return st.reshape(bsz, GLA_WIDTH, GLA_KW)


def _state_from_t(st):
    bsz = st.shape[0]
    s5 = st.reshape(bsz, GLA_HEADS, GLA_DV, GLA_HEADS, GLA_DK)
    return jnp.stack([jnp.swapaxes(s5[:, hh, :, hh, :], 1, 2) for hh in range(GLA_HEADS)], axis=1)


def _prep_weights(norm_mix, w_in, w_gate, b_gate, gla_gain, pool_w, pool_scale, w_out, norm_mem_q, w_mq, w_mo,
                  norm_moe, w_router, b_router, w_up, b_up, w_down, b_down):
    n_qkvg = 2 * GLA_KW + 2 * GLA_WIDTH
    w_qkvg = jnp.zeros((D_MODEL, n_qkvg + LANES), BF16).at[:, :n_qkvg + GATE_RANK].set(
        w_in[:, :n_qkvg + GATE_RANK].astype(BF16))
    w_g = jnp.zeros((LANES, GLA_KW), BF16).at[:GATE_RANK].set(w_gate.astype(BF16))
    return {
        "norm_mix": norm_mix.reshape(1, -1),
        "w_qkvg": w_qkvg,
        "w_u": w_in[:, n_qkvg + GATE_RANK:].astype(BF16),
        "w_gate": w_g,
        "b_gate": b_gate.reshape(1, -1),
        "gla_gain": gla_gain.reshape(1, -1),
        "pool_w": pool_w.astype(BF16),
        "pool_scale": pool_scale.reshape(1, -1),
        "w_out": w_out.astype(BF16),
        "norm_mem_q": norm_mem_q.reshape(1, -1),
        "w_mq": w_mq.astype(BF16),
        "w_mo": w_mo.astype(BF16),
        "norm_moe": norm_moe.reshape(1, -1),
        "w_router": w_router.T.astype(BF16),
        "b_router": b_router.reshape(-1, 1),
        "w_up": w_up,
        "b_up": b_up.reshape(N_EXPERTS, 1, -1),
        "w_down": w_down,
        "b_down": b_down.reshape(N_EXPERTS, 1, -1),
    }


def _block(x, s_gla, pool_prev, hist_len, mk, mv, w, norm_final):
    batch, seq, _ = x.shape
    x2d = x.reshape(batch * seq, D_MODEL)
    pool0 = jnp.concatenate([jnp.zeros((batch, 1, POOL_WIDTH), F32), pool_prev.astype(F32)], axis=1)
    x2, route, cnt, st_t, pool_t = _premoe(x2d, batch, seq, hist_len, _state_to_t(s_gla), pool0,
                                            mk.reshape(batch, N_MEM, D_MODEL), mv.reshape(batch, N_MEM, D_MODEL), w)
    out = _moe_layer(x2, route, cnt, w, norm_final)
    return out.reshape(batch, seq, D_MODEL), _state_from_t(st_t), pool_t[:, 1:, :]


def kernel(x_prompt, x_sample, mem_prompt, state_gla, state_pool, cache_mem_k, cache_mem_v, norm_mix, w_in, w_gate, b_gate, gla_gain, pool_w, pool_scale, w_out, norm_mem_q, norm_mem_kv, w_mq, w_mk, w_mv, w_mo, norm_moe, w_router, b_router, w_up, b_up, w_down, b_down, norm_final):
    depth = w_in.shape[0]
    assert depth == 1
    xp, xs = x_prompt, x_sample
    bp = xp.shape[0]
    gla_p, pool_p, mk_p, mv_p, gla_s, pool_s = [], [], [], [], [], []
    for l in range(depth):
        w = _prep_weights(norm_mix[l], w_in[l], w_gate[l], b_gate[l], gla_gain[l], pool_w[l], pool_scale[l], w_out[l],
                          norm_mem_q[l], w_mq[l], w_mo[l], norm_moe[l], w_router[l], b_router[l],
                          w_up[l], b_up[l], w_down[l], b_down[l])
        mk2, mv2 = _mem_kv(mem_prompt.reshape(bp * N_MEM, D_MODEL), norm_mem_kv[l].reshape(1, -1),
                           w_mk[l].astype(BF16), w_mv[l].astype(BF16))
        mk = mk2.reshape(bp, N_MEM, MEM_HEADS, MEM_HD)
        mv = mv2.reshape(bp, N_MEM, MEM_HEADS, MEM_HD)
        xs, ss, ps = _block(xs, state_gla[l], state_pool[l], PAST_LEN, cache_mem_k[l], cache_mem_v[l], w, norm_final)
        gla_s.append(ss)
        pool_s.append(ps)
        s0 = jnp.zeros((bp, GLA_HEADS, GLA_DK, GLA_DV), F32)
        p0 = jnp.zeros((bp, POOL_BUF, POOL_WIDTH), F32)
        xp, sp, pp = _block(xp, s0, p0, 0, mk, mv, w, norm_final)
        gla_p.append(sp)
        pool_p.append(pp)
        mk_p.append(mk)
        mv_p.append(mv)
    return (xp, xs, jnp.stack(gla_p), jnp.stack(pool_p), jnp.stack(mk_p), jnp.stack(mv_p),
            jnp.stack(gla_s), jnp.stack(pool_s))
```

```python
import functools

import numpy as np
import jax
import jax.numpy as jnp
from jax import lax
from jax.experimental import pallas as pl
from jax.experimental.pallas import tpu as pltpu

F32 = jnp.float32
BF16 = jnp.bfloat16
I32 = jnp.int32

D_MODEL = 1024
CHUNK = 64
GLA_HEADS = 4
GLA_DK = 64
GLA_DV = 128
GLA_KW = GLA_HEADS * GLA_DK
GLA_WIDTH = GLA_HEADS * GLA_DV
GATE_RANK = 16
GATE_TAU = 16.0
POOL_WIDTH = 512
POOL_GC = 128
POOL_WINDOWS = (2, 4, 8, 16)
POOL_BUF = 15
PAST_LEN = 4096
POOL_ROWS = 16
N_MEM = 256
MEM_HEADS = 4
MEM_HD = 256
N_EXPERTS = 32
TOP_K = 4
D_FF = 1024
SWIGLU_ALPHA = 1.702
SWIGLU_LIMIT = 7.0
EPS = 1e-6
LANES = 128
TILE_ROWS = D_MODEL // LANES
ROUTE_EXPERT, ROUTE_RANK, ROUTE_GATE = 0, 4, 8

PREMOE_ROWS = 512
PREMOE_SUB = 256
MOE_ROWS = 512
DISPATCH_ROWS = 256
COMBINE_ROWS = 256
VMEM_LIMIT = 56 * 1024 * 1024

NT_DIMS = (((1,), (1,)), ((), ()))
TN_DIMS = (((0,), (0,)), ((), ()))


def _rms(x, g):
    ms = jnp.mean(x * x, axis=-1, keepdims=True)
    return x * lax.rsqrt(ms + EPS) * g


def _load_token_tiles(ref, rows):
    return jnp.concatenate([ref[pl.ds(s, rows, stride=TILE_ROWS), :] for s in range(TILE_ROWS)], axis=1)


def _plain_rows(v):
    g = v.shape[0]
    return jnp.concatenate([v[:, c].reshape(g * 8, LANES) for c in range(TILE_ROWS)], axis=1)


def _store_token_tiles(ref, val):
    rows = val.shape[0]
    for s in range(TILE_ROWS):
        ref[pl.ds(s, rows, stride=TILE_ROWS), :] = val[:, s * LANES:(s + 1) * LANES]


def _const_spec(shape):
    nd = len(shape)
    return pl.BlockSpec(shape, lambda *_: (0,) * nd, pipeline_mode=pl.Buffered(1))


def _memkv_body(m_ref, g_ref, wk_ref, wv_ref, k_ref, v_ref):
    m = _rms(m_ref[...], g_ref[...]).astype(BF16)
    k_ref[...] = jnp.dot(m, wk_ref[...], preferred_element_type=F32)
    v_ref[...] = jnp.dot(m, wv_ref[...], preferred_element_type=F32)


def _mem_kv(mem2d, g, wk, wv):
    n = mem2d.shape[0]
    tm = 512
    row = pl.BlockSpec((tm, D_MODEL), lambda i: (i, 0))
    return pl.pallas_call(
        _memkv_body,
        grid=(n // tm,),
        in_specs=[row, _const_spec((1, D_MODEL)), _const_spec((D_MODEL, D_MODEL)), _const_spec((D_MODEL, D_MODEL))],
        out_specs=[row, row],
        out_shape=[jax.ShapeDtypeStruct((n, D_MODEL), F32)] * 2,
        compiler_params=pltpu.CompilerParams(dimension_semantics=("arbitrary",), vmem_limit_bytes=VMEM_LIMIT),
        name="mem_kv",
    )(mem2d, g, wk, wv)


def _premoe_body(hist, rows,
                 x_ref, s0_ref, p0_ref, k_ref, v_ref, cnt0_ref,
                 nmix_ref, wqkvg_ref, wu_ref, wgate_ref, bgate_ref, ggain_ref, poolw_ref, pscale_ref,
                 wout_ref, nq_ref, wmq_ref, wmo_ref, nmoe_ref, wrt_ref, brt_ref,
                 tri2_ref, utri_ref, band_ref,
                 x2_ref, route_ref, cnt_ref, sto_ref, po_ref,
                 st_sc, prev_sc, cnt_sc):
    b = pl.program_id(0)
    l = pl.program_id(1)
    n_chunks = rows // CHUNK
    sub = min(rows, PREMOE_SUB)
    subs = [slice(s0, s0 + sub) for s0 in range(0, rows, sub)]

    @pl.when(l == 0)
    def _():
        st_sc[...] = s0_ref[0]
        prow = lax.broadcasted_iota(I32, (POOL_ROWS, POOL_WIDTH), 0)
        prev_sc[...] = jnp.where(prow >= POOL_ROWS - hist, p0_ref[0], 0.0)

    @pl.when((b == 0) & (l == 0))
    def _():
        cnt_sc[...] = cnt0_ref[...]

    x = x_ref[...]
    h = _rms(x, nmix_ref[...]).astype(BF16)
    z = jnp.dot(h, wqkvg_ref[...], preferred_element_type=F32)
    q = z[:, 0:GLA_KW] * (GLA_DK ** -0.5)
    k = z[:, GLA_KW:2 * GLA_KW]
    v = z[:, 2 * GLA_KW:2 * GLA_KW + GLA_WIDTH]
    g = z[:, 2 * GLA_KW + GLA_WIDTH:2 * GLA_KW + 2 * GLA_WIDTH]
    r = z[:, 2 * GLA_KW + 2 * GLA_WIDTH:]

    gp = jnp.dot(r.astype(BF16), wgate_ref[...], preferred_element_type=F32) + bgate_ref[...]
    la = jax.nn.log_sigmoid(gp) * (1.0 / GATE_TAU)

    hi = la.astype(BF16)
    r1 = la - hi.astype(F32)
    mid = r1.astype(BF16)
    lo = (r1 - mid.astype(F32)).astype(BF16)
    la3 = jnp.concatenate([hi, mid, lo], axis=1)
    bcum_parts, btot_parts = [], []
    for sl in subs:
        bb = jnp.dot(tri2_ref[...], la3[sl], preferred_element_type=F32)
        bsum = (bb[:, 0:GLA_KW] + bb[:, GLA_KW:2 * GLA_KW]) + bb[:, 2 * GLA_KW:]
        bcum_parts.append(bsum[:sub])
        btot_parts.append(bsum[sub:])
    bcum = jnp.concatenate(bcum_parts, axis=0)
    btot = jnp.concatenate(btot_parts, axis=0)

    qd = q * jnp.exp(bcum)
    kd = k * jnp.exp(-bcum)
    kl = k * jnp.exp(btot - bcum)

    lane_kw = lax.broadcasted_iota(I32, (rows, GLA_KW), 1)
    rowi = lax.broadcasted_iota(I32, (sub, sub), 0)
    coli = lax.broadcasted_iota(I32, (sub, sub), 1)
    amask = (coli <= rowi) & (coli >= (rowi & ~(CHUNK - 1)))
    vb = v.astype(BF16)

    o_heads = []
    for hh in range(GLA_HEADS):
        mh = (lane_kw >> 6) == hh
        qh = jnp.where(mh, qd, 0.0).astype(BF16)
        kh = jnp.where(mh, kd, 0.0).astype(BF16)
        o_sub = []
        for sl in subs:
            a = lax.dot_general(qh[sl], kh[sl], NT_DIMS, preferred_element_type=F32)
            a = jnp.where(amask, a, 0.0).astype(BF16)
            o_sub.append(jnp.dot(a, vb[sl, hh * GLA_DV:(hh + 1) * GLA_DV], preferred_element_type=F32))
        o_heads.append(jnp.concatenate(o_sub, axis=0))
    o_intra = jnp.concatenate(o_heads, axis=1)

    srow = lax.broadcasted_iota(I32, (GLA_WIDTH, GLA_KW), 0)
    scol = lax.broadcasted_iota(I32, (GLA_WIDTH, GLA_KW), 1)
    smask = (srow >> 7) == (scol >> 6)
    qdb = qd.astype(BF16)
    klb = kl.astype(BF16)
    st = st_sc[...]
    oi_parts = []
    for n in range(n_chunks):
        lo_r, hi_r = n * CHUNK, (n + 1) * CHUNK
        oi_parts.append(lax.dot_general(qdb[lo_r:hi_r], st.astype(BF16), NT_DIMS, preferred_element_type=F32))
        upd = lax.dot_general(vb[lo_r:hi_r], klb[lo_r:hi_r], TN_DIMS, preferred_element_type=F32)
        dec = jnp.exp(btot[lo_r:lo_r + 1, :])
        st = st * dec + jnp.where(smask, upd, 0.0)
    st_sc[...] = st
    sto_ref[0] = st
    o_inter = oi_parts[0] if n_chunks == 1 else jnp.concatenate(oi_parts, axis=0)
    o = o_intra + o_inter

    gains = ggain_ref[...]
    on_parts = []
    for hh in range(GLA_HEADS):
        oh = o[:, hh * GLA_DV:(hh + 1) * GLA_DV]
        ms = jnp.mean(oh * oh, axis=-1, keepdims=True)
        on_parts.append(oh * lax.rsqrt(ms + EPS) * gains[:, hh * GLA_DV:(hh + 1) * GLA_DV])
    on = jnp.concatenate(on_parts, axis=1) * (g * jax.nn.sigmoid(g))

    u = jnp.dot(h, wu_ref[...], preferred_element_type=F32)
    prevs = [prev_sc[...]] + [u[sl.stop - POOL_ROWS:sl.stop] for sl in subs[:-1]]
    exts = [jnp.concatenate([pv, u[sl]], axis=0).astype(BF16) for pv, sl in zip(prevs, subs)]
    pos = l * rows + lax.broadcasted_iota(I32, (rows, POOL_GC), 0)
    p_parts = []
    for gi, w in enumerate(POOL_WINDOWS):
        cols = slice(gi * POOL_GC, (gi + 1) * POOL_GC)
        s = jnp.concatenate([jnp.dot(band_ref[gi], ext[:, cols], preferred_element_type=F32) for ext in exts], axis=0)
        cnt_w = jnp.minimum(w, pos + 1 + hist).astype(F32)
        dd = s / cnt_w - u[:, cols]
        p_parts.append(jnp.dot(dd.astype(BF16), poolw_ref[gi], preferred_element_type=F32))
    p = jnp.concatenate(p_parts, axis=1) * pscale_ref[...]
    tail = u[rows - POOL_ROWS:rows, :]
    prev_sc[...] = tail
    po_ref[0] = tail

    cat = jnp.concatenate([on, p], axis=1).astype(BF16)
    x1 = x + jnp.dot(cat, wout_ref[...], preferred_element_type=F32)

    h2 = _rms(x1, nq_ref[...]).astype(BF16)
    qm = jnp.dot(h2, wmq_ref[...], preferred_element_type=F32).astype(BF16)
    kk = k_ref[0].astype(BF16)
    vv = v_ref[0].astype(BF16)
    a_parts = []
    for hh in range(MEM_HEADS):
        cols = slice(hh * MEM_HD, (hh + 1) * MEM_HD)
        s = lax.dot_general(qm[:, cols], kk[:, cols], NT_DIMS, preferred_element_type=F32) * (MEM_HD ** -0.5)
        e = jnp.exp(s - jnp.max(s, axis=-1, keepdims=True))
        pr = e / jnp.sum(e, axis=-1, keepdims=True)
        a_parts.append(jnp.dot(pr.astype(BF16), vv[:, cols], preferred_element_type=F32))
    att = jnp.concatenate(a_parts, axis=1).astype(BF16)
    x2 = x1 + jnp.dot(att, wmo_ref[...], preferred_element_type=F32)
    x2_ref[...] = x2

    rp = max(rows, LANES)
    h3 = _rms(x2, nmoe_ref[...]).astype(BF16)
    if rp > rows:
        h3 = jnp.concatenate([h3, jnp.zeros((rp - rows, D_MODEL), BF16)], axis=0)
    logits = lax.dot_general(wrt_ref[...], h3, NT_DIMS, preferred_element_type=F32) + brt_ref[...]
    eid = lax.broadcasted_iota(I32, (N_EXPERTS, rp), 0).astype(F32)
    work = logits
    vals, idxs = [], []
    for _ in range(TOP_K):
        m = jnp.max(work, axis=0, keepdims=True)
        idx = jnp.min(jnp.where(work == m, eid, float(N_EXPERTS)), axis=0, keepdims=True)
        vals.append(m)
        idxs.append(idx)
        work = jnp.where(eid == idx, -jnp.inf, work)
    exps = [jnp.exp(vk - vals[0]) for vk in vals]
    den = (exps[0] + exps[1]) + (exps[2] + exps[3])
    gates = [ek / den for ek in exps]

    valid = lax.broadcasted_iota(I32, (N_EXPERTS, rp), 1) < rows
    hot = jnp.zeros((N_EXPERTS, rp), F32)
    for idx in idxs:
        hot = hot + jnp.where((eid == idx) & valid, 1.0, 0.0)
    prefix = jnp.dot(hot.astype(BF16), utri_ref[...], preferred_element_type=F32) + cnt_sc[:, 0:1]
    ranks = [jnp.sum(jnp.where(eid == idx, prefix, 0.0), axis=0, keepdims=True) for idx in idxs]
    cnt_sc[...] = cnt_sc[...] + jnp.sum(hot, axis=1, keepdims=True)
    cnt_ref[...] = cnt_sc[...]

    pieces = idxs + ranks + gates
    rid = lax.broadcasted_iota(I32, (2 * 8, rp), 0)
    rec = jnp.zeros((2 * 8, rp), F32)
    for j, piece in enumerate(pieces):
        rec = jnp.where(rid == j, piece, rec)
    rec = jnp.concatenate([rec, jnp.zeros((LANES - 2 * 8, rp), F32)], axis=0)
    route_ref[...] = rec.T[:rows]


def _premoe_consts(rows):
    sub = min(rows, PREMOE_SUB)
    i = np.arange(sub)[:, None]
    j = np.arange(sub)[None, :]
    same = (i // CHUNK) == (j // CHUNK)
    tri = (same & (j <= i)).astype(np.float32)
    tot = same.astype(np.float32)
    tri2 = np.concatenate([tri, tot], axis=0)
    je = np.arange(sub + POOL_ROWS)[None, :]
    band = np.stack([((je <= i + POOL_ROWS) & (je > i + POOL_ROWS - w)).astype(np.float32) for w in POOL_WINDOWS])
    rp = max(rows, LANES)
    utri = (np.arange(rp)[:, None] < np.arange(rp)[None, :]).astype(np.float32)
    return jnp.asarray(tri2, BF16), jnp.asarray(utri, BF16), jnp.asarray(band, BF16)


def _premoe(x2d, batch, seq, hist_len, s0t, pool0, kmem, vmem, cnt0, w):
    rows = min(PREMOE_ROWS, seq)
    nl = seq // rows
    t = batch * seq
    hist = min(int(hist_len), POOL_BUF)
    tri2, utri, band = _premoe_consts(rows)
    row_spec = pl.BlockSpec((rows, D_MODEL), lambda b, l: (b * nl + l, 0))
    lane_spec = pl.BlockSpec((rows, LANES), lambda b, l: (b * nl + l, 0))

    def per_batch(shape):
        return pl.BlockSpec((1,) + shape, lambda b, l: (b, 0, 0))

    consts = [w["norm_mix"], w["w_qkvg"], w["w_u"], w["w_gate"], w["b_gate"], w["gla_gain"], w["pool_w"],
              w["pool_scale"], w["w_out"], w["norm_mem_q"], w["w_mq"], w["w_mo"], w["norm_moe"], w["w_router"],
              w["b_router"], tri2, utri, band]
    in_specs = [row_spec, per_batch((GLA_WIDTH, GLA_KW)), per_batch((POOL_ROWS, POOL_WIDTH)),
                per_batch((N_MEM, D_MODEL)), per_batch((N_MEM, D_MODEL)), _const_spec(cnt0.shape)]
    in_specs += [_const_spec(c.shape) for c in consts]
    out_specs = [row_spec, lane_spec, pl.BlockSpec((N_EXPERTS, LANES), lambda b, l: (0, 0)),
                 per_batch((GLA_WIDTH, GLA_KW)), per_batch((POOL_ROWS, POOL_WIDTH))]
    out_shape = [jax.ShapeDtypeStruct((t, D_MODEL), F32), jax.ShapeDtypeStruct((t, LANES), F32),
                 jax.ShapeDtypeStruct((N_EXPERTS, LANES), F32),
                 jax.ShapeDtypeStruct((batch, GLA_WIDTH, GLA_KW), F32),
                 jax.ShapeDtypeStruct((batch, POOL_ROWS, POOL_WIDTH), F32)]
    return pl.pallas_call(
        functools.partial(_premoe_body, hist, rows),
        grid=(batch, nl),
        in_specs=in_specs,
        out_specs=out_specs,
        out_shape=out_shape,
        scratch_shapes=[pltpu.VMEM((GLA_WIDTH, GLA_KW), F32), pltpu.VMEM((POOL_ROWS, POOL_WIDTH), F32),
                        pltpu.VMEM((N_EXPERTS, LANES), F32)],
        compiler_params=pltpu.CompilerParams(dimension_semantics=("arbitrary", "arbitrary"),
                                             vmem_limit_bytes=VMEM_LIMIT),
        name="premoe",
    )(x2d, s0t, pool0, kmem, vmem, cnt0, *consts)


def _route_tables(routes, counts_f, tm, nb):
    e = jnp.concatenate([r[:, ROUTE_EXPERT:ROUTE_EXPERT + TOP_K] for r in routes], axis=0).astype(I32)
    rank = jnp.concatenate([r[:, ROUTE_RANK:ROUTE_RANK + TOP_K] for r in routes], axis=0).astype(I32)
    counts = counts_f[:, 0].astype(I32)
    padded = ((counts + tm - 1) // tm) * tm
    pend = jnp.cumsum(padded)
    pstart = pend - padded
    eids = jnp.arange(N_EXPERTS, dtype=I32)
    dest = jnp.sum(jnp.where(e[:, :, None] == eids, pstart, 0), axis=-1) + rank
    blk_start = jnp.arange(nb, dtype=I32) * tm
    block_e = jnp.minimum(jnp.sum((pend[None, :] <= blk_start[:, None]).astype(I32), axis=1), N_EXPERTS - 1)
    n_used = (pend[-1] // tm).astype(I32).reshape(1)
    return dest.astype(I32), block_e.astype(I32), n_used, pend.astype(I32), counts


def _dispatch_body(rows, tm, nb, na, nu_ref, pend_ref, cnt_ref, dest_ref, xa_ref, xb_ref, g_ref, xg_hbm,
                   hbuf, zbuf, sem, zsem):
    i = pl.program_id(0)
    n_steps = pl.num_programs(0)
    slot = lax.rem(i, 2)
    nu = nu_ref[0]
    groups = rows // 8

    def tile(ref, row):
        return ref.at[pl.ds(pl.multiple_of(row * TILE_ROWS, TILE_ROWS), TILE_ROWS)]

    def zero_copy(start):
        return pltpu.make_async_copy(zbuf, xg_hbm.at[pl.ds(pl.multiple_of(start * TILE_ROWS, TILE_ROWS), tm * TILE_ROWS)],
                                     zsem)

    def for_each_fill(fn):
        def per_expert(e, c):
            @pl.when(cnt_ref[e] > 0)
            def _():
                fn(pend_ref[e] - tm)
            return c
        lax.fori_loop(0, N_EXPERTS, per_expert, 0)

        def per_tail(b, c):
            fn(b * tm)
            return c
        lax.fori_loop(nu, nb, per_tail, 0)

    @pl.when(i == 0)
    def _():
        zbuf[...] = jnp.zeros_like(zbuf)
        for_each_fill(lambda start: zero_copy(start).start())
        for_each_fill(lambda start: zero_copy(start).wait())

    def row_wait(s):
        for _ in range(TOP_K):
            pltpu.make_async_copy(hbuf.at[s], xg_hbm.at[pl.ds(0, rows * TILE_ROWS)], sem.at[s]).wait()

    @pl.when(i >= 2)
    def _():
        row_wait(slot)

    x = jnp.where(i < na, xa_ref[...], xb_ref[...])
    _store_token_tiles(hbuf.at[slot], _rms(x, g_ref[...]))

    def send_group(gi, c):
        base = gi * (8 * TOP_K)
        for r in range(8):
            for kk in range(TOP_K):
                dst = dest_ref[0, 0, base + r * TOP_K + kk]
                pltpu.make_async_copy(tile(hbuf.at[slot], gi * 8 + r), tile(xg_hbm, dst),
                                      sem.at[slot]).start(priority=kk % 2)
        return c
    lax.fori_loop(0, groups, send_group, 0)

    @pl.when(i == n_steps - 1)
    def _():
        row_wait(slot)

        @pl.when(i >= 1)
        def _():
            row_wait(1 - slot)


def _dispatch(x2a, x2b, dest, n_used, pend, counts, g, tm, nb):
    rows = DISPATCH_ROWS
    na, nbs = x2a.shape[0] // rows, x2b.shape[0] // rows
    steps = na + nbs
    dest3 = dest.reshape(steps, 1, rows * TOP_K)
    grid_spec = pltpu.PrefetchScalarGridSpec(
        num_scalar_prefetch=3,
        grid=(steps,),
        in_specs=[
            pl.BlockSpec((1, 1, rows * TOP_K), lambda i, *_: (i, 0, 0), memory_space=pltpu.SMEM),
            pl.BlockSpec((rows, D_MODEL), lambda i, *_: (jnp.minimum(i, na - 1), 0)),
            pl.BlockSpec((rows, D_MODEL), lambda i, *_: (jnp.maximum(i - na, 0), 0)),
            pl.BlockSpec((1, D_MODEL), lambda i, *_: (0, 0)),
        ],
        out_specs=pl.BlockSpec(memory_space=pl.ANY),
        scratch_shapes=[pltpu.VMEM((2, rows * TILE_ROWS, LANES), F32), pltpu.VMEM((tm * TILE_ROWS, LANES), F32),
                        pltpu.SemaphoreType.DMA((2,)), pltpu.SemaphoreType.DMA(())],
    )
    return pl.pallas_call(
        functools.partial(_dispatch_body, rows, tm, nb, na),
        grid_spec=grid_spec,
        out_shape=jax.ShapeDtypeStruct((nb * tm * TILE_ROWS, LANES), F32),
        compiler_params=pltpu.CompilerParams(dimension_semantics=("arbitrary",), vmem_limit_bytes=VMEM_LIMIT),
        name="dispatch",
    )(n_used, pend, counts, dest3, x2a, x2b, g)


def _moe_body(tm, nu_ref, be_ref, xg_ref, wup_ref, bup_ref, wdn_ref, bdn_ref, ys_ref, wup_bf, wdn_bf):
    b = pl.program_id(0)
    in_use = b < nu_ref[0]

    @pl.when(in_use & ((b == 0) | (be_ref[b] != be_ref[jnp.maximum(b - 1, 0)])))
    def _():
        wup_bf[...] = wup_ref[0].astype(BF16)
        wdn_bf[...] = wdn_ref[0].astype(BF16)

    @pl.when(in_use)
    def _():
        hh = _load_token_tiles(xg_ref, tm).astype(BF16)
        uu = jnp.dot(hh, wup_bf[...], preferred_element_type=F32) + bup_ref[0]
        glu = jnp.minimum(uu[:, :D_FF], SWIGLU_LIMIT)
        lin = jnp.clip(uu[:, D_FF:], -SWIGLU_LIMIT, SWIGLU_LIMIT)
        act = glu * jax.nn.sigmoid(SWIGLU_ALPHA * glu) * (lin + 1.0)
        _store_token_tiles(ys_ref, jnp.dot(act.astype(BF16), wdn_bf[...], preferred_element_type=F32) + bdn_ref[0])

    @pl.when(b >= nu_ref[0])
    def _():
        ys_ref[...] = jnp.zeros_like(ys_ref)


def _moe(xg, block_e, n_used, w, tm, nb):
    def used(b, nu):
        return jnp.minimum(b, nu[0] - 1)

    def per_expert(shape):
        return pl.BlockSpec((1,) + shape, lambda b, nu, be: (be[used(b, nu)], 0, 0))

    grid_spec = pltpu.PrefetchScalarGridSpec(
        num_scalar_prefetch=2,
        grid=(nb,),
        in_specs=[
            pl.BlockSpec((tm * TILE_ROWS, LANES), lambda b, nu, be: (used(b, nu), 0)),
            per_expert((D_MODEL, 2 * D_FF)), per_expert((1, 2 * D_FF)),
            per_expert((D_FF, D_MODEL)), per_expert((1, D_MODEL)),
        ],
        out_specs=pl.BlockSpec((tm * TILE_ROWS, LANES), lambda b, nu, be: (b, 0)),
        scratch_shapes=[pltpu.VMEM((D_MODEL, 2 * D_FF), BF16), pltpu.VMEM((D_FF, D_MODEL), BF16)],
    )
    return pl.pallas_call(
        functools.partial(_moe_body, tm),
        grid_spec=grid_spec,
        out_shape=jax.ShapeDtypeStruct((nb * tm * TILE_ROWS, LANES), F32),
        compiler_params=pltpu.CompilerParams(dimension_semantics=("arbitrary",), vmem_limit_bytes=VMEM_LIMIT),
        name="moe",
    )(n_used, block_e, xg, w["w_up"], w["b_up"], w["w_down"], w["b_down"])


def _combine_body(rows, na, destc_ref, destn_ref, xa_ref, xb_ref, ra_ref, rb_ref, g_ref, ys_hbm, oa_ref, ob_ref,
                  gbuf, sem):
    i = pl.program_id(0)
    n_steps = pl.num_programs(0)
    slot = lax.rem(i, 2)
    groups = rows // 8

    def fetch(dest_ref, s):
        def fetch_group(gi, c):
            base = gi * (8 * TOP_K)
            srcs = [dest_ref[0, 0, base + j] for j in range(8 * TOP_K)]
            for r in range(8):
                for kk in range(TOP_K):
                    src = srcs[r * TOP_K + kk]
                    pltpu.make_async_copy(
                        ys_hbm.at[pl.ds(pl.multiple_of(src * TILE_ROWS, TILE_ROWS), TILE_ROWS)],
                        gbuf.at[s, kk, gi, :, r, :],
                        sem.at[s]).start(priority=kk % 2)
            return c
        lax.fori_loop(0, groups, fetch_group, 0)

    @pl.when(i == 0)
    def _():
        fetch(destc_ref, 0)

    @pl.when(i + 1 < n_steps)
    def _():
        fetch(destn_ref, 1 - slot)

    for kk in range(TOP_K):
        for r in range(8):
            pltpu.make_async_copy(ys_hbm.at[pl.ds(0, groups * TILE_ROWS)].reshape(groups, TILE_ROWS, LANES),
                                  gbuf.at[slot, kk, :, :, r, :], sem.at[slot]).wait()

    first = i < na
    gate = jnp.where(first, ra_ref[...], rb_ref[...])
    acc = jnp.where(first, xa_ref[...], xb_ref[...])
    for kk in range(TOP_K):
        acc = acc + _plain_rows(gbuf[slot, kk]) * gate[:, ROUTE_GATE + kk:ROUTE_GATE + kk + 1]
    out = _rms(acc, g_ref[...])

    @pl.when(first)
    def _():
        oa_ref[...] = out

    @pl.when(jnp.logical_not(first))
    def _():
        ob_ref[...] = out


def _combine(x2a, x2b, ys, dest, route_a, route_b, g):
    rows = COMBINE_ROWS
    na, nbs = x2a.shape[0] // rows, x2b.shape[0] // rows
    steps = na + nbs
    dest3 = dest.reshape(steps, 1, rows * TOP_K)

    def seg_a(width):
        return pl.BlockSpec((rows, width), lambda i: (jnp.minimum(i, na - 1), 0))

    def seg_b(width):
        return pl.BlockSpec((rows, width), lambda i: (jnp.maximum(i - na, 0), 0))

    smem_cur = pl.BlockSpec((1, 1, rows * TOP_K), lambda i: (i, 0, 0), memory_space=pltpu.SMEM)
    smem_next = pl.BlockSpec((1, 1, rows * TOP_K), lambda i: (jnp.minimum(i + 1, steps - 1), 0, 0),
                             memory_space=pltpu.SMEM)
    return pl.pallas_call(
        functools.partial(_combine_body, rows, na),
        grid=(steps,),
        in_specs=[smem_cur, smem_next, seg_a(D_MODEL), seg_b(D_MODEL), seg_a(LANES), seg_b(LANES),
                  pl.BlockSpec((1, D_MODEL), lambda i: (0, 0)),
                  pl.BlockSpec(memory_space=pl.ANY)],
        out_specs=[seg_a(D_MODEL), seg_b(D_MODEL)],
        out_shape=[jax.ShapeDtypeStruct(x2a.shape, F32), jax.ShapeDtypeStruct(x2b.shape, F32)],
        scratch_shapes=[pltpu.VMEM((2, TOP_K, rows // 8, TILE_ROWS, 8, LANES), F32), pltpu.SemaphoreType.DMA((2,))],
        compiler_params=pltpu.CompilerParams(dimension_semantics=("arbitrary",), vmem_limit_bytes=VMEM_LIMIT),
        name="combine",
    )(dest3, dest3, x2a, x2b, route_a, route_b, g, ys)


def _moe_layer(x2a, x2b, route_a, route_b, counts_f, w, norm_final):
    t = x2a.shape[0] + x2b.shape[0]
    tm = MOE_ROWS
    nb = (t * TOP_K + N_EXPERTS * (tm - 1) + tm - 1) // tm
    dest, block_e, n_used, pend, counts = _route_tables([route_a, route_b], counts_f, tm, nb)
    xg = _dispatch(x2a, x2b, dest, n_used, pend, counts, w["norm_moe"], tm, nb)
    ys = _moe(xg, block_e, n_used, w, tm, nb)
    return _combine(x2a, x2b, ys, dest, route_a, route_b, norm_final.reshape(1, -1))


def _state_to_t(s):
    bsz = s.shape[0]
    st = jnp.zeros((bsz, GLA_HEADS, GLA_DV, GLA_HEADS, GLA_DK), F32)
    for hh in range(GLA_HEADS):
        st = st.at[:, hh, :, hh, :].set(jnp.swapaxes(s[:, hh], 1, 2).astype(F32))
    return st.reshape(bsz, GLA_WIDTH, GLA_KW)


def _state_from_t(st):
    bsz = st.shape[0]
    s5 = st.reshape(bsz, GLA_HEADS, GLA_DV, GLA_HEADS, GLA_DK)
    return jnp.stack([jnp.swapaxes(s5[:, hh, :, hh, :], 1, 2) for hh in range(GLA_HEADS)], axis=1)


def _prep_weights(norm_mix, w_in, w_gate, b_gate, gla_gain, pool_w, pool_scale, w_out, norm_mem_q, w_mq, w_mo,
                  norm_moe, w_router, b_router, w_up, b_up, w_down, b_down):
    n_qkvg = 2 * GLA_KW + 2 * GLA_WIDTH
    w_qkvg = jnp.zeros((D_MODEL, n_qkvg + LANES), BF16).at[:, :n_qkvg + GATE_RANK].set(
        w_in[:, :n_qkvg + GATE_RANK].astype(BF16))
    w_g = jnp.zeros((LANES, GLA_KW), BF16).at[:GATE_RANK].set(w_gate.astype(BF16))
    return {
        "norm_mix": norm_mix.reshape(1, -1),
        "w_qkvg": w_qkvg,
        "w_u": w_in[:, n_qkvg + GATE_RANK:].astype(BF16),
        "w_gate": w_g,
        "b_gate": b_gate.reshape(1, -1),
        "gla_gain": gla_gain.reshape(1, -1),
        "pool_w": pool_w.astype(BF16),
        "pool_scale": pool_scale.reshape(1, -1),
        "w_out": w_out.astype(BF16),
        "norm_mem_q": norm_mem_q.reshape(1, -1),
        "w_mq": w_mq.astype(BF16),
        "w_mo": w_mo.astype(BF16),
        "norm_moe": norm_moe.reshape(1, -1),
        "w_router": w_router.T.astype(BF16),
        "b_router": b_router.reshape(-1, 1),
        "w_up": w_up,
        "b_up": b_up.reshape(N_EXPERTS, 1, -1),
        "w_down": w_down,
        "b_down": b_down.reshape(N_EXPERTS, 1, -1),
    }


def _mix(x, s_gla, pool_prev, hist_len, mk, mv, cnt0, w):
    batch, seq, _ = x.shape
    x2d = x.reshape(batch * seq, D_MODEL)
    pool0 = jnp.concatenate([jnp.zeros((batch, 1, POOL_WIDTH), F32), pool_prev.astype(F32)], axis=1)
    x2, route, cnt, st_t, pool_t = _premoe(x2d, batch, seq, hist_len, _state_to_t(s_gla), pool0,
                                            mk.reshape(batch, N_MEM, D_MODEL), mv.reshape(batch, N_MEM, D_MODEL),
                                            cnt0, w)
    return x2, route, cnt, _state_from_t(st_t), pool_t[:, 1:, :]


def kernel(x_prompt, x_sample, mem_prompt, state_gla, state_pool, cache_mem_k, cache_mem_v, norm_mix, w_in, w_gate, b_gate, gla_gain, pool_w, pool_scale, w_out, norm_mem_q, norm_mem_kv, w_mq, w_mk, w_mv, w_mo, norm_moe, w_router, b_router, w_up, b_up, w_down, b_down, norm_final):
    depth = w_in.shape[0]
    assert depth == 1
    xp, xs = x_prompt, x_sample
    bp = xp.shape[0]
    gla_p, pool_p, mk_p, mv_p, gla_s, pool_s = [], [], [], [], [], []
    for l in range(depth):
        w = _prep_weights(norm_mix[l], w_in[l], w_gate[l], b_gate[l], gla_gain[l], pool_w[l], pool_scale[l], w_out[l],
                          norm_mem_q[l], w_mq[l], w_mo[l], norm_moe[l], w_router[l], b_router[l],
                          w_up[l], b_up[l], w_down[l], b_down[l])
        mk2, mv2 = _mem_kv(mem_prompt.reshape(bp * N_MEM, D_MODEL), norm_mem_kv[l].reshape(1, -1),
                           w_mk[l].astype(BF16), w_mv[l].astype(BF16))
        mk = mk2.reshape(bp, N_MEM, MEM_HEADS, MEM_HD)
        mv = mv2.reshape(bp, N_MEM, MEM_HEADS, MEM_HD)
        cnt0 = jnp.zeros((N_EXPERTS, LANES), F32)
        x2s, route_s, cnt_s, ss, ps = _mix(xs, state_gla[l], state_pool[l], PAST_LEN, cache_mem_k[l], cache_mem_v[l],
                                           cnt0, w)
        gla_s.append(ss)
        pool_s.append(ps)
        s0 = jnp.zeros((bp, GLA_HEADS, GLA_DK, GLA_DV), F32)
        p0 = jnp.zeros((bp, POOL_BUF, POOL_WIDTH), F32)
        x2p, route_p, cnt_all, sp, pp = _mix(xp, s0, p0, 0, mk, mv, cnt_s, w)
        ys, yp = _moe_layer(x2s, x2p, route_s, route_p, cnt_all, w, norm_final)
        xs, xp = ys.reshape(xs.shape), yp.reshape(xp.shape)
        gla_p.append(sp)
        pool_p.append(pp)
        mk_p.append(mk)
        mv_p.append(mv)
    return (xp, xs, jnp.stack(gla_p), jnp.stack(pool_p), jnp.stack(mk_p), jnp.stack(mv_p),
            jnp.stack(gla_s), jnp.stack(pool_s))
```

```python
import functools

import numpy as np
import jax
import jax.numpy as jnp
from jax import lax
from jax.experimental import pallas as pl
from jax.experimental.pallas import tpu as pltpu

F32 = jnp.float32
BF16 = jnp.bfloat16
I32 = jnp.int32

D_MODEL = 1024
CHUNK = 64
GLA_HEADS = 4
GLA_DK = 64
GLA_DV = 128
GLA_KW = GLA_HEADS * GLA_DK
GLA_WIDTH = GLA_HEADS * GLA_DV
GATE_RANK = 16
GATE_TAU = 16.0
POOL_WIDTH = 512
POOL_GC = 128
POOL_WINDOWS = (2, 4, 8, 16)
POOL_BUF = 15
PAST_LEN = 4096
POOL_ROWS = 16
N_MEM = 256
MEM_HEADS = 4
MEM_HD = 256
N_EXPERTS = 32
TOP_K = 4
D_FF = 1024
SWIGLU_ALPHA = 1.702
SWIGLU_LIMIT = 7.0
EPS = 1e-6
LANES = 128
SUBLANES = 8
TILE_ROWS = D_MODEL // LANES
ROUTE_EXPERT, ROUTE_RANK, ROUTE_GATE = 0, 4, 8

PREMOE_ROWS = 512
PREMOE_SUB = 256
MOE_ROWS = 512
DISPATCH_ROWS = 256
COMBINE_ROWS = 256
VMEM_LIMIT = 56 * 1024 * 1024

NT_DIMS = (((1,), (1,)), ((), ()))
TN_DIMS = (((0,), (0,)), ((), ()))


def _rms(x, g):
    ms = jnp.mean(x * x, axis=-1, keepdims=True)
    return x * lax.rsqrt(ms + EPS) * g


def _load_token_tiles(ref, rows):
    return jnp.concatenate([ref[pl.ds(s, rows, stride=TILE_ROWS), :] for s in range(TILE_ROWS)], axis=1)


def _plain_rows(v):
    g = v.shape[0]
    return jnp.concatenate([v[:, c].reshape(g * SUBLANES, LANES) for c in range(TILE_ROWS)], axis=1)


def _store_token_tiles(ref, val):
    rows = val.shape[0]
    for s in range(TILE_ROWS):
        ref[pl.ds(s, rows, stride=TILE_ROWS), :] = val[:, s * LANES:(s + 1) * LANES]


def _const_spec(shape):
    nd = len(shape)
    return pl.BlockSpec(shape, lambda *_: (0,) * nd, pipeline_mode=pl.Buffered(1))


def _memkv_body(m_ref, g_ref, wk_ref, wv_ref, k_ref, v_ref):
    m = _rms(m_ref[...], g_ref[...]).astype(BF16)
    k_ref[...] = jnp.dot(m, wk_ref[...], preferred_element_type=F32)
    v_ref[...] = jnp.dot(m, wv_ref[...], preferred_element_type=F32)


def _mem_kv(mem2d, g, wk, wv):
    n = mem2d.shape[0]
    tm = 512
    row = pl.BlockSpec((tm, D_MODEL), lambda i: (i, 0))
    return pl.pallas_call(
        _memkv_body,
        grid=(n // tm,),
        in_specs=[row, _const_spec((1, D_MODEL)), _const_spec((D_MODEL, D_MODEL)), _const_spec((D_MODEL, D_MODEL))],
        out_specs=[row, row],
        out_shape=[jax.ShapeDtypeStruct((n, D_MODEL), F32)] * 2,
        compiler_params=pltpu.CompilerParams(dimension_semantics=("arbitrary",), vmem_limit_bytes=VMEM_LIMIT),
        name="mem_kv",
    )(mem2d, g, wk, wv)


def _premoe_body(hist, rows,
                 x_ref, s0_ref, p0_ref, k_ref, v_ref, cnt0_ref,
                 nmix_ref, wqkvg_ref, wu_ref, wgate_ref, bgate_ref, ggain_ref, poolw_ref, pscale_ref,
                 wout_ref, nq_ref, wmq_ref, wmo_ref, nmoe_ref, wrt_ref, brt_ref,
                 tri_ref, utri_ref, band_ref,
                 x2_ref, route_ref, cnt_ref, sto_ref, po_ref,
                 st_sc, prev_sc, cnt_sc):
    b = pl.program_id(0)
    l = pl.program_id(1)
    n_chunks = rows // CHUNK
    sub = min(rows, PREMOE_SUB)
    subs = [slice(s0, s0 + sub) for s0 in range(0, rows, sub)]

    @pl.when(l == 0)
    def _():
        st_sc[...] = s0_ref[0]
        prow = lax.broadcasted_iota(I32, (POOL_ROWS, POOL_WIDTH), 0)
        prev_sc[...] = jnp.where(prow >= POOL_ROWS - hist, p0_ref[0], 0.0)

    @pl.when((b == 0) & (l == 0))
    def _():
        cnt_sc[...] = cnt0_ref[...]

    x = x_ref[...]
    h = _rms(x, nmix_ref[...]).astype(BF16)
    z = jnp.dot(h, wqkvg_ref[...], preferred_element_type=F32)
    q = z[:, 0:GLA_KW] * (GLA_DK ** -0.5)
    k = z[:, GLA_KW:2 * GLA_KW]
    v = z[:, 2 * GLA_KW:2 * GLA_KW + GLA_WIDTH]
    g = z[:, 2 * GLA_KW + GLA_WIDTH:2 * GLA_KW + 2 * GLA_WIDTH]
    r = z[:, 2 * GLA_KW + 2 * GLA_WIDTH:]

    gp = jnp.dot(r.astype(BF16), wgate_ref[...], preferred_element_type=F32) + bgate_ref[...]
    la = jax.nn.log_sigmoid(gp) * (1.0 / GATE_TAU)

    hi = la.astype(BF16)
    r1 = la - hi.astype(F32)
    mid = r1.astype(BF16)
    lo = (r1 - mid.astype(F32)).astype(BF16)
    la3 = jnp.concatenate([hi, mid, lo], axis=1)
    bcum_parts = []
    for sl in subs:
        bb = jnp.dot(tri_ref[...], la3[sl], preferred_element_type=F32)
        bcum_parts.append((bb[:, 0:GLA_KW] + bb[:, GLA_KW:2 * GLA_KW]) + bb[:, 2 * GLA_KW:])
    bcum = jnp.concatenate(bcum_parts, axis=0)
    btot = jnp.concatenate([jnp.broadcast_to(bcum[(n + 1) * CHUNK - 1:(n + 1) * CHUNK, :], (CHUNK, GLA_KW))
                            for n in range(n_chunks)], axis=0)

    qd = q * jnp.exp(bcum)
    kd = k * jnp.exp(-bcum)
    kl = k * jnp.exp(btot - bcum)

    lane_kw = lax.broadcasted_iota(I32, (rows, GLA_KW), 1)
    rowi = lax.broadcasted_iota(I32, (sub, sub), 0)
    coli = lax.broadcasted_iota(I32, (sub, sub), 1)
    amask = (coli <= rowi) & (coli >= (rowi & ~(CHUNK - 1)))
    vb = v.astype(BF16)

    o_heads = []
    for hh in range(GLA_HEADS):
        mh = (lane_kw // GLA_DK) == hh
        qh = jnp.where(mh, qd, 0.0).astype(BF16)
        kh = jnp.where(mh, kd, 0.0).astype(BF16)
        o_sub = []
        for sl in subs:
            a = lax.dot_general(qh[sl], kh[sl], NT_DIMS, preferred_element_type=F32)
            a = jnp.where(amask, a, 0.0).astype(BF16)
            o_sub.append(jnp.dot(a, vb[sl, hh * GLA_DV:(hh + 1) * GLA_DV], preferred_element_type=F32))
        o_heads.append(jnp.concatenate(o_sub, axis=0))
    o_intra = jnp.concatenate(o_heads, axis=1)

    srow = lax.broadcasted_iota(I32, (GLA_WIDTH, GLA_KW), 0)
    scol = lax.broadcasted_iota(I32, (GLA_WIDTH, GLA_KW), 1)
    smask = (srow // GLA_DV) == (scol // GLA_DK)
    qdb = qd.astype(BF16)
    klb = kl.astype(BF16)
    st = st_sc[...]
    oi_parts = []
    for n in range(n_chunks):
        lo_r, hi_r = n * CHUNK, (n + 1) * CHUNK
        oi_parts.append(lax.dot_general(qdb[lo_r:hi_r], st.astype(BF16), NT_DIMS, preferred_element_type=F32))
        upd = lax.dot_general(vb[lo_r:hi_r], klb[lo_r:hi_r], TN_DIMS, preferred_element_type=F32)
        dec = jnp.exp(btot[lo_r:lo_r + 1, :])
        st = st * dec + jnp.where(smask, upd, 0.0)
    st_sc[...] = st
    sto_ref[0] = st
    o_inter = oi_parts[0] if n_chunks == 1 else jnp.concatenate(oi_parts, axis=0)
    o = o_intra + o_inter

    gains = ggain_ref[...]
    on_parts = []
    for hh in range(GLA_HEADS):
        oh = o[:, hh * GLA_DV:(hh + 1) * GLA_DV]
        ms = jnp.mean(oh * oh, axis=-1, keepdims=True)
        on_parts.append(oh * lax.rsqrt(ms + EPS) * gains[:, hh * GLA_DV:(hh + 1) * GLA_DV])
    on = jnp.concatenate(on_parts, axis=1) * (g * jax.nn.sigmoid(g))

    u = jnp.dot(h, wu_ref[...], preferred_element_type=F32)
    prevs = [prev_sc[...]] + [u[sl.stop - POOL_ROWS:sl.stop] for sl in subs[:-1]]
    exts = [jnp.concatenate([pv, u[sl]], axis=0).astype(BF16) for pv, sl in zip(prevs, subs)]
    pos = l * rows + lax.broadcasted_iota(I32, (rows, POOL_GC), 0)
    p_parts = []
    for gi, w in enumerate(POOL_WINDOWS):
        cols = slice(gi * POOL_GC, (gi + 1) * POOL_GC)
        s = jnp.concatenate([jnp.dot(band_ref[gi], ext[:, cols], preferred_element_type=F32) for ext in exts], axis=0)
        cnt_w = jnp.minimum(w, pos + 1 + hist).astype(F32)
        dd = s / cnt_w - u[:, cols]
        p_parts.append(jnp.dot(dd.astype(BF16), poolw_ref[gi], preferred_element_type=F32))
    p = jnp.concatenate(p_parts, axis=1) * pscale_ref[...]
    tail = u[rows - POOL_ROWS:rows, :]
    prev_sc[...] = tail
    po_ref[0] = tail

    cat = jnp.concatenate([on, p], axis=1).astype(BF16)
    x1 = x + jnp.dot(cat, wout_ref[...], preferred_element_type=F32)

    h2 = _rms(x1, nq_ref[...]).astype(BF16)
    qm = jnp.dot(h2, wmq_ref[...], preferred_element_type=F32).astype(BF16)
    kk = k_ref[0].astype(BF16)
    vv = v_ref[0].astype(BF16)
    a_parts = []
    for hh in range(MEM_HEADS):
        cols = slice(hh * MEM_HD, (hh + 1) * MEM_HD)
        s = lax.dot_general(qm[:, cols], kk[:, cols], NT_DIMS, preferred_element_type=F32) * (MEM_HD ** -0.5)
        e = jnp.exp(s - jnp.max(s, axis=-1, keepdims=True))
        pr = e / jnp.sum(e, axis=-1, keepdims=True)
        a_parts.append(jnp.dot(pr.astype(BF16), vv[:, cols], preferred_element_type=F32))
    att = jnp.concatenate(a_parts, axis=1).astype(BF16)
    x2 = x1 + jnp.dot(att, wmo_ref[...], preferred_element_type=F32)
    x2_ref[...] = x2

    rp = max(rows, LANES)
    h3 = _rms(x2, nmoe_ref[...]).astype(BF16)
    if rp > rows:
        h3 = jnp.concatenate([h3, jnp.zeros((rp - rows, D_MODEL), BF16)], axis=0)
    logits = lax.dot_general(wrt_ref[...], h3, NT_DIMS, preferred_element_type=F32) + brt_ref[...]
    eid = lax.broadcasted_iota(I32, (N_EXPERTS, rp), 0).astype(F32)
    work = logits
    vals, idxs = [], []
    for _ in range(TOP_K):
        m = jnp.max(work, axis=0, keepdims=True)
        idx = jnp.min(jnp.where(work == m, eid, float(N_EXPERTS)), axis=0, keepdims=True)
        vals.append(m)
        idxs.append(idx)
        work = jnp.where(eid == idx, -jnp.inf, work)
    exps = [jnp.exp(vk - vals[0]) for vk in vals]
    den = (exps[0] + exps[1]) + (exps[2] + exps[3])
    gates = [ek / den for ek in exps]

    valid = lax.broadcasted_iota(I32, (N_EXPERTS, rp), 1) < rows
    hot = jnp.zeros((N_EXPERTS, rp), F32)
    for idx in idxs:
        hot = hot + jnp.where((eid == idx) & valid, 1.0, 0.0)
    prefix = jnp.dot(hot.astype(BF16), utri_ref[...], preferred_element_type=F32) + cnt_sc[:, 0:1]
    ranks = [jnp.sum(jnp.where(eid == idx, prefix, 0.0), axis=0, keepdims=True) for idx in idxs]
    cnt_sc[...] = cnt_sc[...] + jnp.sum(hot, axis=1, keepdims=True)
    cnt_ref[...] = cnt_sc[...]

    pieces = idxs + ranks + gates
    rec_rows = 2 * SUBLANES
    rid = lax.broadcasted_iota(I32, (rec_rows, rp), 0)
    rec = jnp.zeros((rec_rows, rp), F32)
    for j, piece in enumerate(pieces):
        rec = jnp.where(rid == j, piece, rec)
    rec = jnp.concatenate([rec, jnp.zeros((LANES - rec_rows, rp), F32)], axis=0)
    route_ref[...] = rec.T[:rows]


def _premoe_consts(rows):
    sub = min(rows, PREMOE_SUB)
    i = np.arange(sub)[:, None]
    j = np.arange(sub)[None, :]
    same = (i // CHUNK) == (j // CHUNK)
    tri = (same & (j <= i)).astype(np.float32)
    je = np.arange(sub + POOL_ROWS)[None, :]
    band = np.stack([((je <= i + POOL_ROWS) & (je > i + POOL_ROWS - w)).astype(np.float32) for w in POOL_WINDOWS])
    rp = max(rows, LANES)
    utri = (np.arange(rp)[:, None] < np.arange(rp)[None, :]).astype(np.float32)
    return jnp.asarray(tri, BF16), jnp.asarray(utri, BF16), jnp.asarray(band, BF16)


def _premoe(x2d, batch, seq, hist_len, s0t, pool0, kmem, vmem, cnt0, w):
    rows = min(PREMOE_ROWS, seq)
    nl = seq // rows
    t = batch * seq
    hist = min(int(hist_len), POOL_BUF)
    tri, utri, band = _premoe_consts(rows)
    row_spec = pl.BlockSpec((rows, D_MODEL), lambda b, l: (b * nl + l, 0))
    lane_spec = pl.BlockSpec((rows, LANES), lambda b, l: (b * nl + l, 0))

    def per_batch(shape):
        return pl.BlockSpec((1,) + shape, lambda b, l: (b, 0, 0))

    consts = [w["norm_mix"], w["w_qkvg"], w["w_u"], w["w_gate"], w["b_gate"], w["gla_gain"], w["pool_w"],
              w["pool_scale"], w["w_out"], w["norm_mem_q"], w["w_mq"], w["w_mo"], w["norm_moe"], w["w_router"],
              w["b_router"], tri, utri, band]
    in_specs = [row_spec, per_batch((GLA_WIDTH, GLA_KW)), per_batch((POOL_ROWS, POOL_WIDTH)),
                per_batch((N_MEM, D_MODEL)), per_batch((N_MEM, D_MODEL)), _const_spec(cnt0.shape)]
    in_specs += [_const_spec(c.shape) for c in consts]
    out_specs = [row_spec, lane_spec, pl.BlockSpec((N_EXPERTS, LANES), lambda b, l: (0, 0)),
                 per_batch((GLA_WIDTH, GLA_KW)), per_batch((POOL_ROWS, POOL_WIDTH))]
    out_shape = [jax.ShapeDtypeStruct((t, D_MODEL), F32), jax.ShapeDtypeStruct((t, LANES), F32),
                 jax.ShapeDtypeStruct((N_EXPERTS, LANES), F32),
                 jax.ShapeDtypeStruct((batch, GLA_WIDTH, GLA_KW), F32),
                 jax.ShapeDtypeStruct((batch, POOL_ROWS, POOL_WIDTH), F32)]
    return pl.pallas_call(
        functools.partial(_premoe_body, hist, rows),
        grid=(batch, nl),
        in_specs=in_specs,
        out_specs=out_specs,
        out_shape=out_shape,
        scratch_shapes=[pltpu.VMEM((GLA_WIDTH, GLA_KW), F32), pltpu.VMEM((POOL_ROWS, POOL_WIDTH), F32),
                        pltpu.VMEM((N_EXPERTS, LANES), F32)],
        compiler_params=pltpu.CompilerParams(dimension_semantics=("arbitrary", "arbitrary"),
                                             vmem_limit_bytes=VMEM_LIMIT),
        name="premoe",
    )(x2d, s0t, pool0, kmem, vmem, cnt0, *consts)


def _route_tables(routes, counts_f, tm, nb):
    e = jnp.concatenate([r[:, ROUTE_EXPERT:ROUTE_EXPERT + TOP_K] for r in routes], axis=0).astype(I32)
    rank = jnp.concatenate([r[:, ROUTE_RANK:ROUTE_RANK + TOP_K] for r in routes], axis=0).astype(I32)
    counts = counts_f[:, 0].astype(I32)
    padded = ((counts + tm - 1) // tm) * tm
    pend = jnp.cumsum(padded)
    pstart = pend - padded
    eids = jnp.arange(N_EXPERTS, dtype=I32)
    dest = jnp.sum(jnp.where(e[:, :, None] == eids, pstart, 0), axis=-1) + rank
    blk_start = jnp.arange(nb, dtype=I32) * tm
    block_e = jnp.minimum(jnp.sum((pend[None, :] <= blk_start[:, None]).astype(I32), axis=1), N_EXPERTS - 1)
    n_used = (pend[-1] // tm).astype(I32)
    run_end = pend[block_e] // tm
    next_e = jnp.where(run_end < n_used, block_e[jnp.minimum(run_end, nb - 1)], -1)
    return dest.astype(I32), block_e.astype(I32), next_e.astype(I32), n_used.reshape(1), pend.astype(I32), counts


def _dispatch_body(rows, tm, nb, na, nu_ref, pend_ref, cnt_ref, dest_ref, xa_ref, xb_ref, g_ref, xg_hbm,
                   hbuf, zbuf, sem, zsem):
    i = pl.program_id(0)
    n_steps = pl.num_programs(0)
    slot = lax.rem(i, 2)
    nu = nu_ref[0]
    groups = rows // SUBLANES

    def tile(ref, row):
        return ref.at[pl.ds(pl.multiple_of(row * TILE_ROWS, TILE_ROWS), TILE_ROWS)]

    def zero_copy(start):
        return pltpu.make_async_copy(zbuf, xg_hbm.at[pl.ds(pl.multiple_of(start * TILE_ROWS, TILE_ROWS), tm * TILE_ROWS)],
                                     zsem)

    def for_each_fill(fn):
        def per_expert(e, c):
            @pl.when(cnt_ref[e] > 0)
            def _():
                fn(pend_ref[e] - tm)
            return c
        lax.fori_loop(0, N_EXPERTS, per_expert, 0)

        def per_tail(b, c):
            fn(b * tm)
            return c
        lax.fori_loop(nu, nb, per_tail, 0)

    @pl.when(i == 0)
    def _():
        zbuf[...] = jnp.zeros_like(zbuf)
        for_each_fill(lambda start: zero_copy(start).start())
        for_each_fill(lambda start: zero_copy(start).wait())

    def row_wait(s):
        for _ in range(TOP_K):
            pltpu.make_async_copy(hbuf.at[s], xg_hbm.at[pl.ds(0, rows * TILE_ROWS)], sem.at[s]).wait()

    @pl.when(i >= 2)
    def _():
        row_wait(slot)

    x = jnp.where(i < na, xa_ref[...], xb_ref[...])
    _store_token_tiles(hbuf.at[slot], _rms(x, g_ref[...]))

    def send_group(gi, c):
        base = gi * (SUBLANES * TOP_K)
        for r in range(SUBLANES):
            for kk in range(TOP_K):
                dst = dest_ref[0, 0, base + r * TOP_K + kk]
                pltpu.make_async_copy(tile(hbuf.at[slot], gi * SUBLANES + r), tile(xg_hbm, dst),
                                      sem.at[slot]).start(priority=kk % 2)
        return c
    lax.fori_loop(0, groups, send_group, 0)

    @pl.when(i == n_steps - 1)
    def _():
        row_wait(slot)

        @pl.when(i >= 1)
        def _():
            row_wait(1 - slot)


def _dispatch(x2a, x2b, dest, n_used, pend, counts, g, tm, nb):
    rows = DISPATCH_ROWS
    na, nbs = x2a.shape[0] // rows, x2b.shape[0] // rows
    steps = na + nbs
    dest3 = dest.reshape(steps, 1, rows * TOP_K)
    grid_spec = pltpu.PrefetchScalarGridSpec(
        num_scalar_prefetch=3,
        grid=(steps,),
        in_specs=[
            pl.BlockSpec((1, 1, rows * TOP_K), lambda i, *_: (i, 0, 0), memory_space=pltpu.SMEM),
            pl.BlockSpec((rows, D_MODEL), lambda i, *_: (jnp.minimum(i, na - 1), 0)),
            pl.BlockSpec((rows, D_MODEL), lambda i, *_: (jnp.maximum(i - na, 0), 0)),
            pl.BlockSpec((1, D_MODEL), lambda i, *_: (0, 0)),
        ],
        out_specs=pl.BlockSpec(memory_space=pl.ANY),
        scratch_shapes=[pltpu.VMEM((2, rows * TILE_ROWS, LANES), F32), pltpu.VMEM((tm * TILE_ROWS, LANES), F32),
                        pltpu.SemaphoreType.DMA((2,)), pltpu.SemaphoreType.DMA(())],
    )
    return pl.pallas_call(
        functools.partial(_dispatch_body, rows, tm, nb, na),
        grid_spec=grid_spec,
        out_shape=jax.ShapeDtypeStruct((nb * tm * TILE_ROWS, LANES), F32),
        compiler_params=pltpu.CompilerParams(dimension_semantics=("arbitrary",), vmem_limit_bytes=VMEM_LIMIT),
        name="dispatch",
    )(n_used, pend, counts, dest3, x2a, x2b, g)


def _moe_body(tm, nu_ref, be_ref, nxt_ref, xg_ref, wup_hbm, bup_ref, wdn_hbm, bdn_ref, ys_ref,
              wup_st, wdn_st, wup_bf, wdn_bf, wsem):
    b = pl.program_id(0)
    in_use = b < nu_ref[0]

    def fetch_up(e):
        return pltpu.make_async_copy(wup_hbm.at[e], wup_st, wsem.at[0])

    def fetch_dn(e):
        return pltpu.make_async_copy(wdn_hbm.at[e], wdn_st, wsem.at[1])

    @pl.when(b == 0)
    def _():
        fetch_up(be_ref[0]).start()
        fetch_dn(be_ref[0]).start()

    @pl.when(in_use & ((b == 0) | (be_ref[b] != be_ref[jnp.maximum(b - 1, 0)])))
    def _():
        fetch_up(be_ref[b]).wait()
        fetch_dn(be_ref[b]).wait()
        wup_bf[...] = wup_st[...].astype(BF16)
        wdn_bf[...] = wdn_st[...].astype(BF16)

        @pl.when(nxt_ref[b] >= 0)
        def _():
            fetch_up(nxt_ref[b]).start()
            fetch_dn(nxt_ref[b]).start()

    @pl.when(in_use)
    def _():
        hh = _load_token_tiles(xg_ref, tm).astype(BF16)
        uu = jnp.dot(hh, wup_bf[...], preferred_element_type=F32) + bup_ref[0]
        glu = jnp.minimum(uu[:, :D_FF], SWIGLU_LIMIT)
        lin = jnp.clip(uu[:, D_FF:], -SWIGLU_LIMIT, SWIGLU_LIMIT)
        act = glu * jax.nn.sigmoid(SWIGLU_ALPHA * glu) * (lin + 1.0)
        _store_token_tiles(ys_ref, jnp.dot(act.astype(BF16), wdn_bf[...], preferred_element_type=F32) + bdn_ref[0])

    @pl.when(b >= nu_ref[0])
    def _():
        ys_ref[...] = jnp.zeros_like(ys_ref)


def _moe(xg, block_e, next_e, n_used, w, tm, nb):
    def used(b, nu):
        return jnp.minimum(b, nu[0] - 1)

    def per_expert(shape):
        return pl.BlockSpec((1,) + shape, lambda b, nu, be, nxt: (be[used(b, nu)], 0, 0))

    grid_spec = pltpu.PrefetchScalarGridSpec(
        num_scalar_prefetch=3,
        grid=(nb,),
        in_specs=[
            pl.BlockSpec((tm * TILE_ROWS, LANES), lambda b, nu, be, nxt: (used(b, nu), 0)),
            pl.BlockSpec(memory_space=pl.ANY), per_expert((1, 2 * D_FF)),
            pl.BlockSpec(memory_space=pl.ANY), per_expert((1, D_MODEL)),
        ],
        out_specs=pl.BlockSpec((tm * TILE_ROWS, LANES), lambda b, nu, be, nxt: (b, 0)),
        scratch_shapes=[pltpu.VMEM((D_MODEL, 2 * D_FF), F32), pltpu.VMEM((D_FF, D_MODEL), F32),
                        pltpu.VMEM((D_MODEL, 2 * D_FF), BF16), pltpu.VMEM((D_FF, D_MODEL), BF16),
                        pltpu.SemaphoreType.DMA((2,))],
    )
    return pl.pallas_call(
        functools.partial(_moe_body, tm),
        grid_spec=grid_spec,
        out_shape=jax.ShapeDtypeStruct((nb * tm * TILE_ROWS, LANES), F32),
        compiler_params=pltpu.CompilerParams(dimension_semantics=("arbitrary",), vmem_limit_bytes=VMEM_LIMIT),
        name="moe",
    )(n_used, block_e, next_e, xg, w["w_up"], w["b_up"], w["w_down"], w["b_down"])


def _combine_body(rows, na, destc_ref, destn_ref, xa_ref, xb_ref, ra_ref, rb_ref, g_ref, ys_hbm, oa_ref, ob_ref,
                  gbuf, sem):
    i = pl.program_id(0)
    n_steps = pl.num_programs(0)
    slot = lax.rem(i, 2)
    groups = rows // SUBLANES

    def fetch(dest_ref, s):
        def fetch_group(gi, c):
            base = gi * (SUBLANES * TOP_K)
            srcs = [dest_ref[0, 0, base + j] for j in range(SUBLANES * TOP_K)]
            for r in range(SUBLANES):
                for kk in range(TOP_K):
                    src = srcs[r * TOP_K + kk]
                    pltpu.make_async_copy(
                        ys_hbm.at[pl.ds(pl.multiple_of(src * TILE_ROWS, TILE_ROWS), TILE_ROWS)],
                        gbuf.at[s, kk, gi, :, r, :],
                        sem.at[s]).start(priority=kk % 2)
            return c
        lax.fori_loop(0, groups, fetch_group, 0)

    @pl.when(i == 0)
    def _():
        fetch(destc_ref, 0)

    @pl.when(i + 1 < n_steps)
    def _():
        fetch(destn_ref, 1 - slot)

    for kk in range(TOP_K):
        for r in range(SUBLANES):
            pltpu.make_async_copy(ys_hbm.at[pl.ds(0, groups * TILE_ROWS)].reshape(groups, TILE_ROWS, LANES),
                                  gbuf.at[slot, kk, :, :, r, :], sem.at[slot]).wait()

    first = i < na
    gate = jnp.where(first, ra_ref[...], rb_ref[...])
    acc = jnp.where(first, xa_ref[...], xb_ref[...])
    for kk in range(TOP_K):
        acc = acc + _plain_rows(gbuf[slot, kk]) * gate[:, ROUTE_GATE + kk:ROUTE_GATE + kk + 1]
    out = _rms(acc, g_ref[...])

    @pl.when(first)
    def _():
        oa_ref[...] = out

    @pl.when(jnp.logical_not(first))
    def _():
        ob_ref[...] = out


def _combine(x2a, x2b, ys, dest, route_a, route_b, g):
    rows = COMBINE_ROWS
    na, nbs = x2a.shape[0] // rows, x2b.shape[0] // rows
    steps = na + nbs
    dest3 = dest.reshape(steps, 1, rows * TOP_K)

    def seg_a(width):
        return pl.BlockSpec((rows, width), lambda i: (jnp.minimum(i, na - 1), 0))

    def seg_b(width):
        return pl.BlockSpec((rows, width), lambda i: (jnp.maximum(i - na, 0), 0))

    smem_cur = pl.BlockSpec((1, 1, rows * TOP_K), lambda i: (i, 0, 0), memory_space=pltpu.SMEM)
    smem_next = pl.BlockSpec((1, 1, rows * TOP_K), lambda i: (jnp.minimum(i + 1, steps - 1), 0, 0),
                             memory_space=pltpu.SMEM)
    return pl.pallas_call(
        functools.partial(_combine_body, rows, na),
        grid=(steps,),
        in_specs=[smem_cur, smem_next, seg_a(D_MODEL), seg_b(D_MODEL), seg_a(LANES), seg_b(LANES),
                  pl.BlockSpec((1, D_MODEL), lambda i: (0, 0)),
                  pl.BlockSpec(memory_space=pl.ANY)],
        out_specs=[seg_a(D_MODEL), seg_b(D_MODEL)],
        out_shape=[jax.ShapeDtypeStruct(x2a.shape, F32), jax.ShapeDtypeStruct(x2b.shape, F32)],
        scratch_shapes=[pltpu.VMEM((2, TOP_K, rows // SUBLANES, TILE_ROWS, SUBLANES, LANES), F32), pltpu.SemaphoreType.DMA((2,))],
        compiler_params=pltpu.CompilerParams(dimension_semantics=("arbitrary",), vmem_limit_bytes=VMEM_LIMIT),
        name="combine",
    )(dest3, dest3, x2a, x2b, route_a, route_b, g, ys)


def _moe_layer(x2a, x2b, route_a, route_b, counts_f, w, norm_final):
    t = x2a.shape[0] + x2b.shape[0]
    tm = MOE_ROWS
    nb = (t * TOP_K + N_EXPERTS * (tm - 1) + tm - 1) // tm
    dest, block_e, next_e, n_used, pend, counts = _route_tables([route_a, route_b], counts_f, tm, nb)
    xg = _dispatch(x2a, x2b, dest, n_used, pend, counts, w["norm_moe"], tm, nb)
    ys = _moe(xg, block_e, next_e, n_used, w, tm, nb)
    return _combine(x2a, x2b, ys, dest, route_a, route_b, norm_final.reshape(1, -1))


def _state_to_t(s):
    bsz = s.shape[0]
    st = jnp.zeros((bsz, GLA_HEADS, GLA_DV, GLA_HEADS, GLA_DK), F32)
    for hh in range(GLA_HEADS):
        st = st.at[:, hh, :, hh, :].set(jnp.swapaxes(s[:, hh], 1, 2).astype(F32))
    return st.reshape(bsz, GLA_WIDTH, GLA_KW)


def _state_from_t(st):
    bsz = st.shape[0]
    s5 = st.reshape(bsz, GLA_HEADS, GLA_DV, GLA_HEADS, GLA_DK)
    return jnp.stack([jnp.swapaxes(s5[:, hh, :, hh, :], 1, 2) for hh in range(GLA_HEADS)], axis=1)


def _prep_weights(norm_mix, w_in, w_gate, b_gate, gla_gain, pool_w, pool_scale, w_out, norm_mem_q, w_mq, w_mo,
                  norm_moe, w_router, b_router, w_up, b_up, w_down, b_down):
    n_qkvg = 2 * GLA_KW + 2 * GLA_WIDTH
    w_qkvg = jnp.zeros((D_MODEL, n_qkvg + LANES), BF16).at[:, :n_qkvg + GATE_RANK].set(
        w_in[:, :n_qkvg + GATE_RANK].astype(BF16))
    w_g = jnp.zeros((LANES, GLA_KW), BF16).at[:GATE_RANK].set(w_gate.astype(BF16))
    return {
        "norm_mix": norm_mix.reshape(1, -1),
        "w_qkvg": w_qkvg,
        "w_u": w_in[:, n_qkvg + GATE_RANK:].astype(BF16),
        "w_gate": w_g,
        "b_gate": b_gate.reshape(1, -1),
        "gla_gain": gla_gain.reshape(1, -1),
        "pool_w": pool_w.astype(BF16),
        "pool_scale": pool_scale.reshape(1, -1),
        "w_out": w_out.astype(BF16),
        "norm_mem_q": norm_mem_q.reshape(1, -1),
        "w_mq": w_mq.astype(BF16),
        "w_mo": w_mo.astype(BF16),
        "norm_moe": norm_moe.reshape(1, -1),
        "w_router": w_router.T.astype(BF16),
        "b_router": b_router.reshape(-1, 1),
        "w_up": w_up,
        "b_up": b_up.reshape(N_EXPERTS, 1, -1),
        "w_down": w_down,
        "b_down": b_down.reshape(N_EXPERTS, 1, -1),
    }


def _mix(x, s_gla, pool_prev, hist_len, mk, mv, cnt0, w):
    batch, seq, _ = x.shape
    x2d = x.reshape(batch * seq, D_MODEL)
    pool0 = jnp.concatenate([jnp.zeros((batch, 1, POOL_WIDTH), F32), pool_prev.astype(F32)], axis=1)
    x2, route, cnt, st_t, pool_t = _premoe(x2d, batch, seq, hist_len, _state_to_t(s_gla), pool0,
                                            mk.reshape(batch, N_MEM, D_MODEL), mv.reshape(batch, N_MEM, D_MODEL),
                                            cnt0, w)
    return x2, route, cnt, _state_from_t(st_t), pool_t[:, 1:, :]


def kernel(x_prompt, x_sample, mem_prompt, state_gla, state_pool, cache_mem_k, cache_mem_v, norm_mix, w_in, w_gate, b_gate, gla_gain, pool_w, pool_scale, w_out, norm_mem_q, norm_mem_kv, w_mq, w_mk, w_mv, w_mo, norm_moe, w_router, b_router, w_up, b_up, w_down, b_down, norm_final):
    depth = w_in.shape[0]
    assert depth == 1
    xp, xs = x_prompt, x_sample
    bp = xp.shape[0]
    gla_p, pool_p, mk_p, mv_p, gla_s, pool_s = [], [], [], [], [], []
    for l in range(depth):
        w = _prep_weights(norm_mix[l], w_in[l], w_gate[l], b_gate[l], gla_gain[l], pool_w[l], pool_scale[l], w_out[l],
                          norm_mem_q[l], w_mq[l], w_mo[l], norm_moe[l], w_router[l], b_router[l],
                          w_up[l], b_up[l], w_down[l], b_down[l])
        mk2, mv2 = _mem_kv(mem_prompt.reshape(bp * N_MEM, D_MODEL), norm_mem_kv[l].reshape(1, -1),
                           w_mk[l].astype(BF16), w_mv[l].astype(BF16))
        mk = mk2.reshape(bp, N_MEM, MEM_HEADS, MEM_HD)
        mv = mv2.reshape(bp, N_MEM, MEM_HEADS, MEM_HD)
        cnt0 = jnp.zeros((N_EXPERTS, LANES), F32)
        x2s, route_s, cnt_s, ss, ps = _mix(xs, state_gla[l], state_pool[l], PAST_LEN, cache_mem_k[l], cache_mem_v[l],
                                           cnt0, w)
        gla_s.append(ss)
        pool_s.append(ps)
        s0 = jnp.zeros((bp, GLA_HEADS, GLA_DK, GLA_DV), F32)
        p0 = jnp.zeros((bp, POOL_BUF, POOL_WIDTH), F32)
        x2p, route_p, cnt_all, sp, pp = _mix(xp, s0, p0, 0, mk, mv, cnt_s, w)
        ys, yp = _moe_layer(x2s, x2p, route_s, route_p, cnt_all, w, norm_final)
        xs, xp = ys.reshape(xs.shape), yp.reshape(xp.shape)
        gla_p.append(sp)
        pool_p.append(pp)
        mk_p.append(mk)
        mv_p.append(mv)
    return (xp, xs, jnp.stack(gla_p), jnp.stack(pool_p), jnp.stack(mk_p), jnp.stack(mv_p),
            jnp.stack(gla_s), jnp.stack(pool_s))
```

```python
import functools

import numpy as np
import jax
import jax.numpy as jnp
from jax import lax
from jax.experimental import pallas as pl
from jax.experimental.pallas import tpu as pltpu

F32 = jnp.float32
BF16 = jnp.bfloat16
I32 = jnp.int32

D_MODEL = 1024
CHUNK = 64
GLA_HEADS = 4
GLA_DK = 64
GLA_DV = 128
GLA_KW = GLA_HEADS * GLA_DK
GLA_WIDTH = GLA_HEADS * GLA_DV
GATE_RANK = 16
GATE_TAU = 16.0
POOL_WIDTH = 512
POOL_GC = 128
POOL_WINDOWS = (2, 4, 8, 16)
POOL_BUF = 15
PAST_LEN = 4096
POOL_ROWS = 16
N_MEM = 256
MEM_HEADS = 4
MEM_HD = 256
N_EXPERTS = 32
TOP_K = 4
D_FF = 1024
SWIGLU_ALPHA = 1.702
SWIGLU_LIMIT = 7.0
EPS = 1e-6
LANES = 128
SUBLANES = 8
TILE_ROWS = D_MODEL // LANES
ROUTE_EXPERT, ROUTE_RANK, ROUTE_GATE = 0, 4, 8

PREMOE_ROWS = 512
PREMOE_SUB = 256
MOE_ROWS = 512
DISPATCH_ROWS = 256
COMBINE_ROWS = 256
VMEM_LIMIT = 56 * 1024 * 1024

assert 2 * GLA_DK == LANES and GLA_DV == LANES

NT_DIMS = (((1,), (1,)), ((), ()))
TN_DIMS = (((0,), (0,)), ((), ()))


def _rms(x, g):
    ms = jnp.mean(x * x, axis=-1, keepdims=True)
    return x * lax.rsqrt(ms + EPS) * g


def _load_token_tiles(ref, rows):
    return jnp.concatenate([ref[pl.ds(s, rows, stride=TILE_ROWS), :] for s in range(TILE_ROWS)], axis=1)


def _plain_rows(v):
    g = v.shape[0]
    return jnp.concatenate([v[:, c].reshape(g * SUBLANES, LANES) for c in range(TILE_ROWS)], axis=1)


def _store_token_tiles(ref, val):
    rows = val.shape[0]
    for s in range(TILE_ROWS):
        ref[pl.ds(s, rows, stride=TILE_ROWS), :] = val[:, s * LANES:(s + 1) * LANES]


def _const_spec(shape):
    nd = len(shape)
    return pl.BlockSpec(shape, lambda *_: (0,) * nd, pipeline_mode=pl.Buffered(1))


def _memkv_body(m_ref, g_ref, wk_ref, wv_ref, k_ref, v_ref):
    m = _rms(m_ref[...], g_ref[...]).astype(BF16)
    k_ref[...] = jnp.dot(m, wk_ref[...], preferred_element_type=F32)
    v_ref[...] = jnp.dot(m, wv_ref[...], preferred_element_type=F32)


def _mem_kv(mem2d, g, wk, wv):
    n = mem2d.shape[0]
    tm = 512
    row = pl.BlockSpec((tm, D_MODEL), lambda i: (i, 0))
    return pl.pallas_call(
        _memkv_body,
        grid=(n // tm,),
        in_specs=[row, _const_spec((1, D_MODEL)), _const_spec((D_MODEL, D_MODEL)), _const_spec((D_MODEL, D_MODEL))],
        out_specs=[row, row],
        out_shape=[jax.ShapeDtypeStruct((n, D_MODEL), F32)] * 2,
        compiler_params=pltpu.CompilerParams(dimension_semantics=("arbitrary",), vmem_limit_bytes=VMEM_LIMIT),
        name="mem_kv",
    )(mem2d, g, wk, wv)


def _premoe_body(hist, rows,
                 x_ref, s0_ref, p0_ref, k_ref, v_ref, cnt0_ref,
                 nmix_ref, wqkvg_ref, wu_ref, wgate_ref, bgate_ref, ggain_ref, poolw_ref, pscale_ref,
                 wout_ref, nq_ref, wmq_ref, wmo_ref, nmoe_ref, wrt_ref, brt_ref,
                 tri_ref, utri_ref, band_ref,
                 x2_ref, route_ref, cnt_ref, sto_ref, po_ref,
                 st_sc, prev_sc, cnt_sc):
    b = pl.program_id(0)
    l = pl.program_id(1)
    n_chunks = rows // CHUNK
    sub = min(rows, PREMOE_SUB)
    subs = [slice(s0, s0 + sub) for s0 in range(0, rows, sub)]

    zeros_dv = jnp.zeros((GLA_DV, LANES), F32)
    half = jnp.zeros((GLA_DK, GLA_DV), F32)

    @pl.when(l == 0)
    def _():
        row_blocks = []
        for hh in range(GLA_HEADS):
            s_h = s0_ref[0, hh]
            padded = jnp.concatenate([s_h, half] if hh % 2 == 0 else [half, s_h], axis=0)
            t_h = padded.T
            row_blocks.append(jnp.concatenate([t_h, zeros_dv] if hh // 2 == 0 else [zeros_dv, t_h], axis=1))
        st_sc[...] = jnp.concatenate(row_blocks, axis=0)
        prow = lax.broadcasted_iota(I32, (POOL_ROWS, POOL_WIDTH), 0)
        prev_sc[...] = jnp.where(prow >= POOL_ROWS - hist, p0_ref[0], 0.0)

    @pl.when((b == 0) & (l == 0))
    def _():
        cnt_sc[...] = cnt0_ref[...]

    x = x_ref[...]
    h = _rms(x, nmix_ref[...]).astype(BF16)
    z = jnp.dot(h, wqkvg_ref[...], preferred_element_type=F32)
    q = z[:, 0:GLA_KW] * (GLA_DK ** -0.5)
    k = z[:, GLA_KW:2 * GLA_KW]
    v = z[:, 2 * GLA_KW:2 * GLA_KW + GLA_WIDTH]
    g = z[:, 2 * GLA_KW + GLA_WIDTH:2 * GLA_KW + 2 * GLA_WIDTH]
    r = z[:, 2 * GLA_KW + 2 * GLA_WIDTH:]

    gp = jnp.dot(r.astype(BF16), wgate_ref[...], preferred_element_type=F32) + bgate_ref[...]
    la = jax.nn.log_sigmoid(gp) * (1.0 / GATE_TAU)

    hi = la.astype(BF16)
    r1 = la - hi.astype(F32)
    mid = r1.astype(BF16)
    lo = (r1 - mid.astype(F32)).astype(BF16)
    la3 = jnp.concatenate([hi, mid, lo], axis=1)
    bcum_parts = []
    for sl in subs:
        bb = jnp.dot(tri_ref[...], la3[sl], preferred_element_type=F32)
        bcum_parts.append((bb[:, 0:GLA_KW] + bb[:, GLA_KW:2 * GLA_KW]) + bb[:, 2 * GLA_KW:])
    bcum = jnp.concatenate(bcum_parts, axis=0)
    btot = jnp.concatenate([jnp.broadcast_to(bcum[(n + 1) * CHUNK - 1:(n + 1) * CHUNK, :], (CHUNK, GLA_KW))
                            for n in range(n_chunks)], axis=0)

    qd = q * jnp.exp(bcum)
    kd = k * jnp.exp(-bcum)
    kl = k * jnp.exp(btot - bcum)

    lane_kw = lax.broadcasted_iota(I32, (rows, GLA_KW), 1)
    rowi = lax.broadcasted_iota(I32, (sub, sub), 0)
    coli = lax.broadcasted_iota(I32, (sub, sub), 1)
    amask = (coli <= rowi) & (coli >= (rowi & ~(CHUNK - 1)))
    vb = v.astype(BF16)

    o_heads = []
    for hh in range(GLA_HEADS):
        mh = (lane_kw // GLA_DK) == hh
        qh = jnp.where(mh, qd, 0.0).astype(BF16)
        kh = jnp.where(mh, kd, 0.0).astype(BF16)
        o_sub = []
        for sl in subs:
            a = lax.dot_general(qh[sl], kh[sl], NT_DIMS, preferred_element_type=F32)
            a = jnp.where(amask, a, 0.0).astype(BF16)
            o_sub.append(jnp.dot(a, vb[sl, hh * GLA_DV:(hh + 1) * GLA_DV], preferred_element_type=F32))
        o_heads.append(jnp.concatenate(o_sub, axis=0))
    o_intra = jnp.concatenate(o_heads, axis=1)

    srow = lax.broadcasted_iota(I32, (GLA_WIDTH, GLA_KW), 0)
    scol = lax.broadcasted_iota(I32, (GLA_WIDTH, GLA_KW), 1)
    smask = (srow // GLA_DV) == (scol // GLA_DK)
    qdb = qd.astype(BF16)
    klb = kl.astype(BF16)
    st = st_sc[...]
    oi_parts = []
    for n in range(n_chunks):
        lo_r, hi_r = n * CHUNK, (n + 1) * CHUNK
        oi_parts.append(lax.dot_general(qdb[lo_r:hi_r], st.astype(BF16), NT_DIMS, preferred_element_type=F32))
        upd = lax.dot_general(vb[lo_r:hi_r], klb[lo_r:hi_r], TN_DIMS, preferred_element_type=F32)
        dec = jnp.exp(btot[lo_r:lo_r + 1, :])
        st = st * dec + jnp.where(smask, upd, 0.0)
    st_sc[...] = st

    @pl.when(l == pl.num_programs(1) - 1)
    def _():
        for hh in range(GLA_HEADS):
            t_h = st[hh * GLA_DV:(hh + 1) * GLA_DV, (hh // 2) * LANES:(hh // 2 + 1) * LANES]
            sto_ref[0, hh] = t_h.T[(hh % 2) * GLA_DK:(hh % 2 + 1) * GLA_DK, :]
    o_inter = oi_parts[0] if n_chunks == 1 else jnp.concatenate(oi_parts, axis=0)
    o = o_intra + o_inter

    gains = ggain_ref[...]
    on_parts = []
    for hh in range(GLA_HEADS):
        oh = o[:, hh * GLA_DV:(hh + 1) * GLA_DV]
        ms = jnp.mean(oh * oh, axis=-1, keepdims=True)
        on_parts.append(oh * lax.rsqrt(ms + EPS) * gains[:, hh * GLA_DV:(hh + 1) * GLA_DV])
    on = jnp.concatenate(on_parts, axis=1) * (g * jax.nn.sigmoid(g))

    u = jnp.dot(h, wu_ref[...], preferred_element_type=F32)
    prevs = [prev_sc[...]] + [u[sl.stop - POOL_ROWS:sl.stop] for sl in subs[:-1]]
    exts = [jnp.concatenate([pv, u[sl]], axis=0).astype(BF16) for pv, sl in zip(prevs, subs)]
    pos = l * rows + lax.broadcasted_iota(I32, (rows, POOL_GC), 0)
    p_parts = []
    for gi, w in enumerate(POOL_WINDOWS):
        cols = slice(gi * POOL_GC, (gi + 1) * POOL_GC)
        s = jnp.concatenate([jnp.dot(band_ref[gi], ext[:, cols], preferred_element_type=F32) for ext in exts], axis=0)
        cnt_w = jnp.minimum(w, pos + 1 + hist).astype(F32)
        dd = s / cnt_w - u[:, cols]
        p_parts.append(jnp.dot(dd.astype(BF16), poolw_ref[gi], preferred_element_type=F32))
    p = jnp.concatenate(p_parts, axis=1) * pscale_ref[...]
    tail = u[rows - POOL_ROWS:rows, :]
    prev_sc[...] = tail
    po_ref[0] = tail

    cat = jnp.concatenate([on, p], axis=1).astype(BF16)
    x1 = x + jnp.dot(cat, wout_ref[...], preferred_element_type=F32)

    h2 = _rms(x1, nq_ref[...]).astype(BF16)
    qm = jnp.dot(h2, wmq_ref[...], preferred_element_type=F32).astype(BF16)
    kk = k_ref[0].astype(BF16)
    vv = v_ref[0].astype(BF16)
    a_parts = []
    for hh in range(MEM_HEADS):
        cols = slice(hh * MEM_HD, (hh + 1) * MEM_HD)
        s = lax.dot_general(qm[:, cols], kk[:, cols], NT_DIMS, preferred_element_type=F32) * (MEM_HD ** -0.5)
        e = jnp.exp(s - jnp.max(s, axis=-1, keepdims=True))
        pr = e / jnp.sum(e, axis=-1, keepdims=True)
        a_parts.append(jnp.dot(pr.astype(BF16), vv[:, cols], preferred_element_type=F32))
    att = jnp.concatenate(a_parts, axis=1).astype(BF16)
    x2 = x1 + jnp.dot(att, wmo_ref[...], preferred_element_type=F32)
    x2_ref[...] = x2

    rp = max(rows, LANES)
    h3 = _rms(x2, nmoe_ref[...]).astype(BF16)
    if rp > rows:
        h3 = jnp.concatenate([h3, jnp.zeros((rp - rows, D_MODEL), BF16)], axis=0)
    logits = lax.dot_general(wrt_ref[...], h3, NT_DIMS, preferred_element_type=F32) + brt_ref[...]
    eid = lax.broadcasted_iota(I32, (N_EXPERTS, rp), 0).astype(F32)
    work = logits
    vals, idxs = [], []
    for _ in range(TOP_K):
        m = jnp.max(work, axis=0, keepdims=True)
        idx = jnp.min(jnp.where(work == m, eid, float(N_EXPERTS)), axis=0, keepdims=True)
        vals.append(m)
        idxs.append(idx)
        work = jnp.where(eid == idx, -jnp.inf, work)
    exps = [jnp.exp(vk - vals[0]) for vk in vals]
    den = (exps[0] + exps[1]) + (exps[2] + exps[3])
    gates = [ek / den for ek in exps]

    valid = lax.broadcasted_iota(I32, (N_EXPERTS, rp), 1) < rows
    hot = jnp.zeros((N_EXPERTS, rp), F32)
    for idx in idxs:
        hot = hot + jnp.where((eid == idx) & valid, 1.0, 0.0)
    prefix = jnp.dot(hot.astype(BF16), utri_ref[...], preferred_element_type=F32) + cnt_sc[:, 0:1]
    ranks = [jnp.sum(jnp.where(eid == idx, prefix, 0.0), axis=0, keepdims=True) for idx in idxs]
    cnt_sc[...] = cnt_sc[...] + jnp.sum(hot, axis=1, keepdims=True)
    cnt_ref[...] = cnt_sc[...]

    pieces = idxs + ranks + gates
    rec_rows = 2 * SUBLANES
    rid = lax.broadcasted_iota(I32, (rec_rows, rp), 0)
    rec = jnp.zeros((rec_rows, rp), F32)
    for j, piece in enumerate(pieces):
        rec = jnp.where(rid == j, piece, rec)
    rec = jnp.concatenate([rec, jnp.zeros((LANES - rec_rows, rp), F32)], axis=0)
    route_ref[...] = rec.T[:rows]


def _premoe_consts(rows):
    sub = min(rows, PREMOE_SUB)
    i = np.arange(sub)[:, None]
    j = np.arange(sub)[None, :]
    same = (i // CHUNK) == (j // CHUNK)
    tri = (same & (j <= i)).astype(np.float32)
    je = np.arange(sub + POOL_ROWS)[None, :]
    band = np.stack([((je <= i + POOL_ROWS) & (je > i + POOL_ROWS - w)).astype(np.float32) for w in POOL_WINDOWS])
    rp = max(rows, LANES)
    utri = (np.arange(rp)[:, None] < np.arange(rp)[None, :]).astype(np.float32)
    return jnp.asarray(tri, BF16), jnp.asarray(utri, BF16), jnp.asarray(band, BF16)


def _premoe(x2d, batch, seq, hist_len, s0, pool0, kmem, vmem, cnt0, w):
    rows = min(PREMOE_ROWS, seq)
    nl = seq // rows
    t = batch * seq
    hist = min(int(hist_len), POOL_BUF)
    tri, utri, band = _premoe_consts(rows)
    row_spec = pl.BlockSpec((rows, D_MODEL), lambda b, l: (b * nl + l, 0))
    lane_spec = pl.BlockSpec((rows, LANES), lambda b, l: (b * nl + l, 0))

    def per_batch(shape):
        return pl.BlockSpec((1,) + shape, lambda b, l: (b, 0, 0))

    consts = [w["norm_mix"], w["w_qkvg"], w["w_u"], w["w_gate"], w["b_gate"], w["gla_gain"], w["pool_w"],
              w["pool_scale"], w["w_out"], w["norm_mem_q"], w["w_mq"], w["w_mo"], w["norm_moe"], w["w_router"],
              w["b_router"], tri, utri, band]
    state_spec = pl.BlockSpec((1, GLA_HEADS, GLA_DK, GLA_DV), lambda b, l: (b, 0, 0, 0))
    in_specs = [row_spec, state_spec, per_batch((POOL_ROWS, POOL_WIDTH)),
                per_batch((N_MEM, D_MODEL)), per_batch((N_MEM, D_MODEL)), _const_spec(cnt0.shape)]
    in_specs += [_const_spec(c.shape) for c in consts]
    out_specs = [row_spec, lane_spec, pl.BlockSpec((N_EXPERTS, LANES), lambda b, l: (0, 0)),
                 state_spec, per_batch((POOL_ROWS, POOL_WIDTH))]
    out_shape = [jax.ShapeDtypeStruct((t, D_MODEL), F32), jax.ShapeDtypeStruct((t, LANES), F32),
                 jax.ShapeDtypeStruct((N_EXPERTS, LANES), F32),
                 jax.ShapeDtypeStruct((batch, GLA_HEADS, GLA_DK, GLA_DV), F32),
                 jax.ShapeDtypeStruct((batch, POOL_ROWS, POOL_WIDTH), F32)]
    return pl.pallas_call(
        functools.partial(_premoe_body, hist, rows),
        grid=(batch, nl),
        in_specs=in_specs,
        out_specs=out_specs,
        out_shape=out_shape,
        scratch_shapes=[pltpu.VMEM((GLA_WIDTH, GLA_KW), F32), pltpu.VMEM((POOL_ROWS, POOL_WIDTH), F32),
                        pltpu.VMEM((N_EXPERTS, LANES), F32)],
        compiler_params=pltpu.CompilerParams(dimension_semantics=("arbitrary", "arbitrary"),
                                             vmem_limit_bytes=VMEM_LIMIT),
        name="premoe",
    )(x2d, s0, pool0, kmem, vmem, cnt0, *consts)


def _route_tables(routes, counts_f, tm, nb):
    e = jnp.concatenate([r[:, ROUTE_EXPERT:ROUTE_EXPERT + TOP_K] for r in routes], axis=0).astype(I32)
    rank = jnp.concatenate([r[:, ROUTE_RANK:ROUTE_RANK + TOP_K] for r in routes], axis=0).astype(I32)
    counts = counts_f[:, 0].astype(I32)
    padded = ((counts + tm - 1) // tm) * tm
    pend = jnp.cumsum(padded)
    pstart = pend - padded
    eids = jnp.arange(N_EXPERTS, dtype=I32)
    dest = jnp.sum(jnp.where(e[:, :, None] == eids, pstart, 0), axis=-1) + rank
    blk_start = jnp.arange(nb, dtype=I32) * tm
    block_e = jnp.minimum(jnp.sum((pend[None, :] <= blk_start[:, None]).astype(I32), axis=1), N_EXPERTS - 1)
    n_used = (pend[-1] // tm).astype(I32)
    run_end = pend[block_e] // tm
    next_e = jnp.where(run_end < n_used, block_e[jnp.minimum(run_end, nb - 1)], -1)
    return dest.astype(I32), block_e.astype(I32), next_e.astype(I32), n_used.reshape(1), pend.astype(I32), counts


def _dispatch_body(rows, tm, nb, na, nu_ref, pend_ref, cnt_ref, dest_ref, xa_ref, xb_ref, g_ref, xg_hbm,
                   hbuf, zbuf, sem, zsem):
    i = pl.program_id(0)
    n_steps = pl.num_programs(0)
    slot = lax.rem(i, 2)
    nu = nu_ref[0]
    groups = rows // SUBLANES

    def tile(ref, row):
        return ref.at[pl.ds(pl.multiple_of(row * TILE_ROWS, TILE_ROWS), TILE_ROWS)]

    def zero_copy(start):
        return pltpu.make_async_copy(zbuf, xg_hbm.at[pl.ds(pl.multiple_of(start * TILE_ROWS, TILE_ROWS), tm * TILE_ROWS)],
                                     zsem)

    def for_each_fill(fn):
        def per_expert(e, c):
            @pl.when(cnt_ref[e] > 0)
            def _():
                fn(pend_ref[e] - tm)
            return c
        lax.fori_loop(0, N_EXPERTS, per_expert, 0)

        def per_tail(b, c):
            fn(b * tm)
            return c
        lax.fori_loop(nu, nb, per_tail, 0)

    @pl.when(i == 0)
    def _():
        zbuf[...] = jnp.zeros_like(zbuf)
        for_each_fill(lambda start: zero_copy(start).start())
        for_each_fill(lambda start: zero_copy(start).wait())

    def row_wait(s):
        for _ in range(TOP_K):
            pltpu.make_async_copy(hbuf.at[s], xg_hbm.at[pl.ds(0, rows * TILE_ROWS)], sem.at[s]).wait()

    @pl.when(i >= 2)
    def _():
        row_wait(slot)

    x = jnp.where(i < na, xa_ref[...], xb_ref[...])
    _store_token_tiles(hbuf.at[slot], _rms(x, g_ref[...]))

    def send_group(gi, c):
        base = gi * (SUBLANES * TOP_K)
        for r in range(SUBLANES):
            for kk in range(TOP_K):
                dst = dest_ref[0, 0, base + r * TOP_K + kk]
                pltpu.make_async_copy(tile(hbuf.at[slot], gi * SUBLANES + r), tile(xg_hbm, dst),
                                      sem.at[slot]).start(priority=kk % 2)
        return c
    lax.fori_loop(0, groups, send_group, 0)

    @pl.when(i == n_steps - 1)
    def _():
        row_wait(slot)

        @pl.when(i >= 1)
        def _():
            row_wait(1 - slot)


def _dispatch(x2a, x2b, dest, n_used, pend, counts, g, tm, nb):
    rows = DISPATCH_ROWS
    na, nbs = x2a.shape[0] // rows, x2b.shape[0] // rows
    steps = na + nbs
    dest3 = dest.reshape(steps, 1, rows * TOP_K)
    grid_spec = pltpu.PrefetchScalarGridSpec(
        num_scalar_prefetch=3,
        grid=(steps,),
        in_specs=[
            pl.BlockSpec((1, 1, rows * TOP_K), lambda i, *_: (i, 0, 0), memory_space=pltpu.SMEM),
            pl.BlockSpec((rows, D_MODEL), lambda i, *_: (jnp.minimum(i, na - 1), 0)),
            pl.BlockSpec((rows, D_MODEL), lambda i, *_: (jnp.maximum(i - na, 0), 0)),
            pl.BlockSpec((1, D_MODEL), lambda i, *_: (0, 0)),
        ],
        out_specs=pl.BlockSpec(memory_space=pl.ANY),
        scratch_shapes=[pltpu.VMEM((2, rows * TILE_ROWS, LANES), F32), pltpu.VMEM((tm * TILE_ROWS, LANES), F32),
                        pltpu.SemaphoreType.DMA((2,)), pltpu.SemaphoreType.DMA(())],
    )
    return pl.pallas_call(
        functools.partial(_dispatch_body, rows, tm, nb, na),
        grid_spec=grid_spec,
        out_shape=jax.ShapeDtypeStruct((nb * tm * TILE_ROWS, LANES), F32),
        compiler_params=pltpu.CompilerParams(dimension_semantics=("arbitrary",), vmem_limit_bytes=VMEM_LIMIT),
        name="dispatch",
    )(n_used, pend, counts, dest3, x2a, x2b, g)


def _moe_body(tm, nu_ref, be_ref, nxt_ref, xg_ref, wup_hbm, bup_ref, wdn_hbm, bdn_ref, ys_ref,
              wup_st, wdn_st, wup_bf, wdn_bf, wsem):
    b = pl.program_id(0)
    in_use = b < nu_ref[0]

    def fetch_up(e):
        return pltpu.make_async_copy(wup_hbm.at[e], wup_st, wsem.at[0])

    def fetch_dn(e):
        return pltpu.make_async_copy(wdn_hbm.at[e], wdn_st, wsem.at[1])

    @pl.when(b == 0)
    def _():
        fetch_up(be_ref[0]).start()
        fetch_dn(be_ref[0]).start()

    @pl.when(in_use & ((b == 0) | (be_ref[b] != be_ref[jnp.maximum(b - 1, 0)])))
    def _():
        fetch_up(be_ref[b]).wait()
        fetch_dn(be_ref[b]).wait()
        wup_bf[...] = wup_st[...].astype(BF16)
        wdn_bf[...] = wdn_st[...].astype(BF16)

        @pl.when(nxt_ref[b] >= 0)
        def _():
            fetch_up(nxt_ref[b]).start()
            fetch_dn(nxt_ref[b]).start()

    @pl.when(in_use)
    def _():
        hh = _load_token_tiles(xg_ref, tm).astype(BF16)
        uu = jnp.dot(hh, wup_bf[...], preferred_element_type=F32) + bup_ref[0]
        glu = jnp.minimum(uu[:, :D_FF], SWIGLU_LIMIT)
        lin = jnp.clip(uu[:, D_FF:], -SWIGLU_LIMIT, SWIGLU_LIMIT)
        act = glu * jax.nn.sigmoid(SWIGLU_ALPHA * glu) * (lin + 1.0)
        _store_token_tiles(ys_ref, jnp.dot(act.astype(BF16), wdn_bf[...], preferred_element_type=F32) + bdn_ref[0])

    @pl.when(b >= nu_ref[0])
    def _():
        ys_ref[...] = jnp.zeros_like(ys_ref)


def _moe(xg, block_e, next_e, n_used, w, tm, nb):
    def used(b, nu):
        return jnp.minimum(b, nu[0] - 1)

    def per_expert(shape):
        return pl.BlockSpec((1,) + shape, lambda b, nu, be, nxt: (be[used(b, nu)], 0, 0))

    grid_spec = pltpu.PrefetchScalarGridSpec(
        num_scalar_prefetch=3,
        grid=(nb,),
        in_specs=[
            pl.BlockSpec((tm * TILE_ROWS, LANES), lambda b, nu, be, nxt: (used(b, nu), 0)),
            pl.BlockSpec(memory_space=pl.ANY), per_expert((1, 2 * D_FF)),
            pl.BlockSpec(memory_space=pl.ANY), per_expert((1, D_MODEL)),
        ],
        out_specs=pl.BlockSpec((tm * TILE_ROWS, LANES), lambda b, nu, be, nxt: (b, 0)),
        scratch_shapes=[pltpu.VMEM((D_MODEL, 2 * D_FF), F32), pltpu.VMEM((D_FF, D_MODEL), F32),
                        pltpu.VMEM((D_MODEL, 2 * D_FF), BF16), pltpu.VMEM((D_FF, D_MODEL), BF16),
                        pltpu.SemaphoreType.DMA((2,))],
    )
    return pl.pallas_call(
        functools.partial(_moe_body, tm),
        grid_spec=grid_spec,
        out_shape=jax.ShapeDtypeStruct((nb * tm * TILE_ROWS, LANES), F32),
        compiler_params=pltpu.CompilerParams(dimension_semantics=("arbitrary",), vmem_limit_bytes=VMEM_LIMIT),
        name="moe",
    )(n_used, block_e, next_e, xg, w["w_up"], w["b_up"], w["w_down"], w["b_down"])


def _combine_body(rows, na, destc_ref, destn_ref, xa_ref, xb_ref, ra_ref, rb_ref, g_ref, ys_hbm, oa_ref, ob_ref,
                  gbuf, sem):
    i = pl.program_id(0)
    n_steps = pl.num_programs(0)
    slot = lax.rem(i, 2)
    groups = rows // SUBLANES

    def fetch(dest_ref, s):
        def fetch_group(gi, c):
            base = gi * (SUBLANES * TOP_K)
            srcs = [dest_ref[0, 0, base + j] for j in range(SUBLANES * TOP_K)]
            for r in range(SUBLANES):
                for kk in range(TOP_K):
                    src = srcs[r * TOP_K + kk]
                    pltpu.make_async_copy(
                        ys_hbm.at[pl.ds(pl.multiple_of(src * TILE_ROWS, TILE_ROWS), TILE_ROWS)],
                        gbuf.at[s, kk, gi, :, r, :],
                        sem.at[s]).start(priority=kk % 2)
            return c
        lax.fori_loop(0, groups, fetch_group, 0)

    @pl.when(i == 0)
    def _():
        fetch(destc_ref, 0)

    @pl.when(i + 1 < n_steps)
    def _():
        fetch(destn_ref, 1 - slot)

    for kk in range(TOP_K):
        for r in range(SUBLANES):
            pltpu.make_async_copy(ys_hbm.at[pl.ds(0, groups * TILE_ROWS)].reshape(groups, TILE_ROWS, LANES),
                                  gbuf.at[slot, kk, :, :, r, :], sem.at[slot]).wait()

    first = i < na
    gate = jnp.where(first, ra_ref[...], rb_ref[...])
    acc = jnp.where(first, xa_ref[...], xb_ref[...])
    for kk in range(TOP_K):
        acc = acc + _plain_rows(gbuf[slot, kk]) * gate[:, ROUTE_GATE + kk:ROUTE_GATE + kk + 1]
    out = _rms(acc, g_ref[...])

    @pl.when(first)
    def _():
        oa_ref[...] = out

    @pl.when(jnp.logical_not(first))
    def _():
        ob_ref[...] = out


def _combine(x2a, x2b, ys, dest, route_a, route_b, g):
    rows = COMBINE_ROWS
    na, nbs = x2a.shape[0] // rows, x2b.shape[0] // rows
    steps = na + nbs
    dest3 = dest.reshape(steps, 1, rows * TOP_K)

    def seg_a(width):
        return pl.BlockSpec((rows, width), lambda i: (jnp.minimum(i, na - 1), 0))

    def seg_b(width):
        return pl.BlockSpec((rows, width), lambda i: (jnp.maximum(i - na, 0), 0))

    smem_cur = pl.BlockSpec((1, 1, rows * TOP_K), lambda i: (i, 0, 0), memory_space=pltpu.SMEM)
    smem_next = pl.BlockSpec((1, 1, rows * TOP_K), lambda i: (jnp.minimum(i + 1, steps - 1), 0, 0),
                             memory_space=pltpu.SMEM)
    return pl.pallas_call(
        functools.partial(_combine_body, rows, na),
        grid=(steps,),
        in_specs=[smem_cur, smem_next, seg_a(D_MODEL), seg_b(D_MODEL), seg_a(LANES), seg_b(LANES),
                  pl.BlockSpec((1, D_MODEL), lambda i: (0, 0)),
                  pl.BlockSpec(memory_space=pl.ANY)],
        out_specs=[seg_a(D_MODEL), seg_b(D_MODEL)],
        out_shape=[jax.ShapeDtypeStruct(x2a.shape, F32), jax.ShapeDtypeStruct(x2b.shape, F32)],
        scratch_shapes=[pltpu.VMEM((2, TOP_K, rows // SUBLANES, TILE_ROWS, SUBLANES, LANES), F32), pltpu.SemaphoreType.DMA((2,))],
        compiler_params=pltpu.CompilerParams(dimension_semantics=("arbitrary",), vmem_limit_bytes=VMEM_LIMIT),
        name="combine",
    )(dest3, dest3, x2a, x2b, route_a, route_b, g, ys)


def _moe_layer(x2a, x2b, route_a, route_b, counts_f, w, norm_final):
    t = x2a.shape[0] + x2b.shape[0]
    tm = MOE_ROWS
    nb = (t * TOP_K + N_EXPERTS * (tm - 1) + tm - 1) // tm
    dest, block_e, next_e, n_used, pend, counts = _route_tables([route_a, route_b], counts_f, tm, nb)
    xg = _dispatch(x2a, x2b, dest, n_used, pend, counts, w["norm_moe"], tm, nb)
    ys = _moe(xg, block_e, next_e, n_used, w, tm, nb)
    return _combine(x2a, x2b, ys, dest, route_a, route_b, norm_final.reshape(1, -1))


def _prep_weights(norm_mix, w_in, w_gate, b_gate, gla_gain, pool_w, pool_scale, w_out, norm_mem_q, w_mq, w_mo,
                  norm_moe, w_router, b_router, w_up, b_up, w_down, b_down):
    n_qkvg = 2 * GLA_KW + 2 * GLA_WIDTH
    w_qkvg = jnp.zeros((D_MODEL, n_qkvg + LANES), BF16).at[:, :n_qkvg + GATE_RANK].set(
        w_in[:, :n_qkvg + GATE_RANK].astype(BF16))
    w_g = jnp.zeros((LANES, GLA_KW), BF16).at[:GATE_RANK].set(w_gate.astype(BF16))
    return {
        "norm_mix": norm_mix.reshape(1, -1),
        "w_qkvg": w_qkvg,
        "w_u": w_in[:, n_qkvg + GATE_RANK:].astype(BF16),
        "w_gate": w_g,
        "b_gate": b_gate.reshape(1, -1),
        "gla_gain": gla_gain.reshape(1, -1),
        "pool_w": pool_w.astype(BF16),
        "pool_scale": pool_scale.reshape(1, -1),
        "w_out": w_out.astype(BF16),
        "norm_mem_q": norm_mem_q.reshape(1, -1),
        "w_mq": w_mq.astype(BF16),
        "w_mo": w_mo.astype(BF16),
        "norm_moe": norm_moe.reshape(1, -1),
        "w_router": w_router.T.astype(BF16),
        "b_router": b_router.reshape(-1, 1),
        "w_up": w_up,
        "b_up": b_up.reshape(N_EXPERTS, 1, -1),
        "w_down": w_down,
        "b_down": b_down.reshape(N_EXPERTS, 1, -1),
    }


def _mix(x, s_gla, pool_prev, hist_len, mk, mv, cnt0, w):
    batch, seq, _ = x.shape
    x2d = x.reshape(batch * seq, D_MODEL)
    pool0 = jnp.concatenate([jnp.zeros((batch, 1, POOL_WIDTH), F32), pool_prev.astype(F32)], axis=1)
    x2, route, cnt, s_new, pool_t = _premoe(x2d, batch, seq, hist_len, s_gla.astype(F32), pool0,
                                            mk.reshape(batch, N_MEM, D_MODEL), mv.reshape(batch, N_MEM, D_MODEL),
                                            cnt0, w)
    return x2, route, cnt, s_new, pool_t[:, 1:, :]


def kernel(x_prompt, x_sample, mem_prompt, state_gla, state_pool, cache_mem_k, cache_mem_v, norm_mix, w_in, w_gate, b_gate, gla_gain, pool_w, pool_scale, w_out, norm_mem_q, norm_mem_kv, w_mq, w_mk, w_mv, w_mo, norm_moe, w_router, b_router, w_up, b_up, w_down, b_down, norm_final):
    depth = w_in.shape[0]
    assert depth == 1
    xp, xs = x_prompt, x_sample
    bp = xp.shape[0]
    gla_p, pool_p, mk_p, mv_p, gla_s, pool_s = [], [], [], [], [], []
    for l in range(depth):
        w = _prep_weights(norm_mix[l], w_in[l], w_gate[l], b_gate[l], gla_gain[l], pool_w[l], pool_scale[l], w_out[l],
                          norm_mem_q[l], w_mq[l], w_mo[l], norm_moe[l], w_router[l], b_router[l],
                          w_up[l], b_up[l], w_down[l], b_down[l])
        mk2, mv2 = _mem_kv(mem_prompt.reshape(bp * N_MEM, D_MODEL), norm_mem_kv[l].reshape(1, -1),
                           w_mk[l].astype(BF16), w_mv[l].astype(BF16))
        mk = mk2.reshape(bp, N_MEM, MEM_HEADS, MEM_HD)
        mv = mv2.reshape(bp, N_MEM, MEM_HEADS, MEM_HD)
        cnt0 = jnp.zeros((N_EXPERTS, LANES), F32)
        x2s, route_s, cnt_s, ss, ps = _mix(xs, state_gla[l], state_pool[l], PAST_LEN, cache_mem_k[l], cache_mem_v[l],
                                           cnt0, w)
        gla_s.append(ss)
        pool_s.append(ps)
        s0 = jnp.zeros((bp, GLA_HEADS, GLA_DK, GLA_DV), F32)
        p0 = jnp.zeros((bp, POOL_BUF, POOL_WIDTH), F32)
        x2p, route_p, cnt_all, sp, pp = _mix(xp, s0, p0, 0, mk, mv, cnt_s, w)
        ys, yp = _moe_layer(x2s, x2p, route_s, route_p, cnt_all, w, norm_final)
        xs, xp = ys.reshape(xs.shape), yp.reshape(xp.shape)
        gla_p.append(sp)
        pool_p.append(pp)
        mk_p.append(mk)
        mv_p.append(mv)
    return (xp, xs, jnp.stack(gla_p), jnp.stack(pool_p), jnp.stack(mk_p), jnp.stack(mv_p),
            jnp.stack(gla_s), jnp.stack(pool_s))
```

```python
import functools

import numpy as np
import jax
import jax.numpy as jnp
from jax import lax
from jax.experimental import pallas as pl
from jax.experimental.pallas import tpu as pltpu

F32 = jnp.float32
BF16 = jnp.bfloat16
I32 = jnp.int32

D_MODEL = 1024
CHUNK = 64
GLA_HEADS = 4
GLA_DK = 64
GLA_DV = 128
GLA_KW = GLA_HEADS * GLA_DK
GLA_WIDTH = GLA_HEADS * GLA_DV
GATE_RANK = 16
GATE_TAU = 16.0
POOL_WIDTH = 512
POOL_GC = 128
POOL_WINDOWS = (2, 4, 8, 16)
POOL_BUF = 15
PAST_LEN = 4096
POOL_ROWS = 16
N_MEM = 256
MEM_HEADS = 4
MEM_HD = 256
N_EXPERTS = 32
TOP_K = 4
D_FF = 1024
SWIGLU_ALPHA = 1.702
SWIGLU_LIMIT = 7.0
EPS = 1e-6
LANES = 128
SUBLANES = 8
TILE_ROWS = D_MODEL // LANES
ROUTE_EXPERT, ROUTE_RANK, ROUTE_GATE = 0, 4, 8

PREMOE_ROWS = 512
PREMOE_SUB = 256
MOE_ROWS = 512
DISPATCH_ROWS = 256
COMBINE_ROWS = 256
VMEM_LIMIT = 56 * 1024 * 1024

assert 2 * GLA_DK == LANES and GLA_DV == LANES

NT_DIMS = (((1,), (1,)), ((), ()))
TN_DIMS = (((0,), (0,)), ((), ()))


def _rms(x, g):
    ms = jnp.mean(x * x, axis=-1, keepdims=True)
    return x * lax.rsqrt(ms + EPS) * g


def _load_token_tiles(ref, rows):
    return jnp.concatenate([ref[pl.ds(s, rows, stride=TILE_ROWS), :] for s in range(TILE_ROWS)], axis=1)


def _plain_rows(v):
    g = v.shape[0]
    return jnp.concatenate([v[:, c].reshape(g * SUBLANES, LANES) for c in range(TILE_ROWS)], axis=1)


def _store_token_tiles(ref, val):
    rows = val.shape[0]
    for s in range(TILE_ROWS):
        ref[pl.ds(s, rows, stride=TILE_ROWS), :] = val[:, s * LANES:(s + 1) * LANES]


def _const_spec(shape):
    nd = len(shape)
    return pl.BlockSpec(shape, lambda *_: (0,) * nd, pipeline_mode=pl.Buffered(1))


def _memkv_body(m_ref, g_ref, wk_ref, wv_ref, k_ref, v_ref):
    m = _rms(m_ref[...], g_ref[...]).astype(BF16)
    k_ref[...] = jnp.dot(m, wk_ref[...], preferred_element_type=F32)
    v_ref[...] = jnp.dot(m, wv_ref[...], preferred_element_type=F32)


def _mem_kv(mem2d, g, wk, wv):
    n = mem2d.shape[0]
    tm = 512
    row = pl.BlockSpec((tm, D_MODEL), lambda i: (i, 0))
    return pl.pallas_call(
        _memkv_body,
        grid=(n // tm,),
        in_specs=[row, _const_spec((1, D_MODEL)), _const_spec((D_MODEL, D_MODEL)), _const_spec((D_MODEL, D_MODEL))],
        out_specs=[row, row],
        out_shape=[jax.ShapeDtypeStruct((n, D_MODEL), F32)] * 2,
        compiler_params=pltpu.CompilerParams(dimension_semantics=("arbitrary",), vmem_limit_bytes=VMEM_LIMIT),
        name="mem_kv",
    )(mem2d, g, wk, wv)


def _premoe_body(hist, rows,
                 x_ref, s0_ref, p0_ref, k_ref, v_ref, cnt0_ref,
                 nmix_ref, wqkvg_ref, wu_ref, wgate_ref, bgate_ref, ggain_ref, poolw_ref, pscale_ref,
                 wout_ref, nq_ref, wmq_ref, wmo_ref, nmoe_ref, wrt_ref, brt_ref,
                 tri_ref, utri_ref, band_ref,
                 x2_ref, route_ref, cnt_ref, sto_ref, po_ref,
                 st_sc, prev_sc, cnt_sc):
    b = pl.program_id(0)
    l = pl.program_id(1)
    n_chunks = rows // CHUNK
    sub = min(rows, PREMOE_SUB)
    subs = [slice(s0, s0 + sub) for s0 in range(0, rows, sub)]

    zeros_dv = jnp.zeros((GLA_DV, LANES), F32)
    half = jnp.zeros((GLA_DK, GLA_DV), F32)

    @pl.when(l == 0)
    def _():
        row_blocks = []
        for hh in range(GLA_HEADS):
            s_h = s0_ref[0, hh]
            padded = jnp.concatenate([s_h, half] if hh % 2 == 0 else [half, s_h], axis=0)
            t_h = padded.T
            row_blocks.append(jnp.concatenate([t_h, zeros_dv] if hh // 2 == 0 else [zeros_dv, t_h], axis=1))
        st_sc[...] = jnp.concatenate(row_blocks, axis=0)
        prow = lax.broadcasted_iota(I32, (POOL_ROWS, POOL_WIDTH), 0)
        prev_sc[...] = jnp.where(prow >= POOL_ROWS - hist, p0_ref[0], 0.0)

    @pl.when((b == 0) & (l == 0))
    def _():
        cnt_sc[...] = cnt0_ref[...]

    x = x_ref[...]
    h = _rms(x, nmix_ref[...]).astype(BF16)
    z = jnp.dot(h, wqkvg_ref[...], preferred_element_type=F32)
    q = z[:, 0:GLA_KW] * (GLA_DK ** -0.5)
    k = z[:, GLA_KW:2 * GLA_KW]
    v = z[:, 2 * GLA_KW:2 * GLA_KW + GLA_WIDTH]
    g = z[:, 2 * GLA_KW + GLA_WIDTH:2 * GLA_KW + 2 * GLA_WIDTH]
    r = z[:, 2 * GLA_KW + 2 * GLA_WIDTH:]

    gp = jnp.dot(r.astype(BF16), wgate_ref[...], preferred_element_type=F32) + bgate_ref[...]
    la = jax.nn.log_sigmoid(gp) * (1.0 / GATE_TAU)

    hi = la.astype(BF16)
    r1 = la - hi.astype(F32)
    mid = r1.astype(BF16)
    lo = (r1 - mid.astype(F32)).astype(BF16)
    la3 = jnp.concatenate([hi, mid, lo], axis=1)
    bcum_parts = []
    for sl in subs:
        bb = jnp.dot(tri_ref[...], la3[sl], preferred_element_type=F32)
        bcum_parts.append((bb[:, 0:GLA_KW] + bb[:, GLA_KW:2 * GLA_KW]) + bb[:, 2 * GLA_KW:])
    bcum = jnp.concatenate(bcum_parts, axis=0)
    btot = jnp.concatenate([jnp.broadcast_to(bcum[(n + 1) * CHUNK - 1:(n + 1) * CHUNK, :], (CHUNK, GLA_KW))
                            for n in range(n_chunks)], axis=0)

    qd = q * jnp.exp(bcum)
    kd = k * jnp.exp(-bcum)
    kl = k * jnp.exp(btot - bcum)

    lane_kw = lax.broadcasted_iota(I32, (rows, GLA_KW), 1)
    rowi = lax.broadcasted_iota(I32, (sub, sub), 0)
    coli = lax.broadcasted_iota(I32, (sub, sub), 1)
    amask = (coli <= rowi) & (coli >= (rowi & ~(CHUNK - 1)))
    vb = v.astype(BF16)

    o_heads = []
    for hh in range(GLA_HEADS):
        mh = (lane_kw // GLA_DK) == hh
        qh = jnp.where(mh, qd, 0.0).astype(BF16)
        kh = jnp.where(mh, kd, 0.0).astype(BF16)
        o_sub = []
        for sl in subs:
            a = lax.dot_general(qh[sl], kh[sl], NT_DIMS, preferred_element_type=F32)
            a = jnp.where(amask, a, 0.0).astype(BF16)
            o_sub.append(jnp.dot(a, vb[sl, hh * GLA_DV:(hh + 1) * GLA_DV], preferred_element_type=F32))
        o_heads.append(jnp.concatenate(o_sub, axis=0))
    o_intra = jnp.concatenate(o_heads, axis=1)

    srow = lax.broadcasted_iota(I32, (GLA_WIDTH, GLA_KW), 0)
    scol = lax.broadcasted_iota(I32, (GLA_WIDTH, GLA_KW), 1)
    smask = (srow // GLA_DV) == (scol // GLA_DK)
    qdb = qd.astype(BF16)
    klb = kl.astype(BF16)
    st = st_sc[...]
    oi_parts = []
    for n in range(n_chunks):
        lo_r, hi_r = n * CHUNK, (n + 1) * CHUNK
        oi_parts.append(lax.dot_general(qdb[lo_r:hi_r], st.astype(BF16), NT_DIMS, preferred_element_type=F32))
        upd = lax.dot_general(vb[lo_r:hi_r], klb[lo_r:hi_r], TN_DIMS, preferred_element_type=F32)
        dec = jnp.exp(btot[lo_r:lo_r + 1, :])
        st = st * dec + jnp.where(smask, upd, 0.0)
    st_sc[...] = st
    o_inter = oi_parts[0] if n_chunks == 1 else jnp.concatenate(oi_parts, axis=0)
    o = o_intra + o_inter

    gains = ggain_ref[...]
    on_parts = []
    for hh in range(GLA_HEADS):
        oh = o[:, hh * GLA_DV:(hh + 1) * GLA_DV]
        ms = jnp.mean(oh * oh, axis=-1, keepdims=True)
        on_parts.append(oh * lax.rsqrt(ms + EPS) * gains[:, hh * GLA_DV:(hh + 1) * GLA_DV])
    on = jnp.concatenate(on_parts, axis=1) * (g * jax.nn.sigmoid(g))

    u = jnp.dot(h, wu_ref[...], preferred_element_type=F32)
    prevs = [prev_sc[...]] + [u[sl.stop - POOL_ROWS:sl.stop] for sl in subs[:-1]]
    exts = [jnp.concatenate([pv, u[sl]], axis=0).astype(BF16) for pv, sl in zip(prevs, subs)]
    pos = l * rows + lax.broadcasted_iota(I32, (rows, POOL_GC), 0)
    p_parts = []
    for gi, w in enumerate(POOL_WINDOWS):
        cols = slice(gi * POOL_GC, (gi + 1) * POOL_GC)
        s = jnp.concatenate([jnp.dot(band_ref[gi], ext[:, cols], preferred_element_type=F32) for ext in exts], axis=0)
        cnt_w = jnp.minimum(w, pos + 1 + hist).astype(F32)
        dd = s / cnt_w - u[:, cols]
        p_parts.append(jnp.dot(dd.astype(BF16), poolw_ref[gi], preferred_element_type=F32))
    p = jnp.concatenate(p_parts, axis=1) * pscale_ref[...]
    tail = u[rows - POOL_ROWS:rows, :]
    prev_sc[...] = tail
    po_ref[0] = tail

    cat = jnp.concatenate([on, p], axis=1).astype(BF16)
    x1 = x + jnp.dot(cat, wout_ref[...], preferred_element_type=F32)

    h2 = _rms(x1, nq_ref[...]).astype(BF16)
    qm = jnp.dot(h2, wmq_ref[...], preferred_element_type=F32).astype(BF16)
    kk = k_ref[0].astype(BF16)
    vv = v_ref[0].astype(BF16)
    a_parts = []
    for hh in range(MEM_HEADS):
        cols = slice(hh * MEM_HD, (hh + 1) * MEM_HD)
        s = lax.dot_general(qm[:, cols], kk[:, cols], NT_DIMS, preferred_element_type=F32) * (MEM_HD ** -0.5)
        e = jnp.exp(s - jnp.max(s, axis=-1, keepdims=True))
        pr = e / jnp.sum(e, axis=-1, keepdims=True)
        a_parts.append(jnp.dot(pr.astype(BF16), vv[:, cols], preferred_element_type=F32))
    att = jnp.concatenate(a_parts, axis=1).astype(BF16)
    x2 = x1 + jnp.dot(att, wmo_ref[...], preferred_element_type=F32)
    x2_ref[...] = x2

    rp = max(rows, LANES)
    h3 = _rms(x2, nmoe_ref[...]).astype(BF16)
    if rp > rows:
        h3 = jnp.concatenate([h3, jnp.zeros((rp - rows, D_MODEL), BF16)], axis=0)
    logits = lax.dot_general(wrt_ref[...], h3, NT_DIMS, preferred_element_type=F32) + brt_ref[...]
    eid = lax.broadcasted_iota(I32, (N_EXPERTS, rp), 0).astype(F32)
    work = logits
    vals, idxs = [], []
    for _ in range(TOP_K):
        m = jnp.max(work, axis=0, keepdims=True)
        idx = jnp.min(jnp.where(work == m, eid, float(N_EXPERTS)), axis=0, keepdims=True)
        vals.append(m)
        idxs.append(idx)
        work = jnp.where(eid == idx, -jnp.inf, work)
    exps = [jnp.exp(vk - vals[0]) for vk in vals]
    den = (exps[0] + exps[1]) + (exps[2] + exps[3])
    gates = [ek / den for ek in exps]

    valid = lax.broadcasted_iota(I32, (N_EXPERTS, rp), 1) < rows
    hot = jnp.zeros((N_EXPERTS, rp), F32)
    for idx in idxs:
        hot = hot + jnp.where((eid == idx) & valid, 1.0, 0.0)
    prefix = jnp.dot(hot.astype(BF16), utri_ref[...], preferred_element_type=F32) + cnt_sc[:, 0:1]
    ranks = [jnp.sum(jnp.where(eid == idx, prefix, 0.0), axis=0, keepdims=True) for idx in idxs]
    cnt_sc[...] = cnt_sc[...] + jnp.sum(hot, axis=1, keepdims=True)
    cnt_ref[...] = cnt_sc[...]

    pieces = idxs + ranks + gates
    rec_rows = 2 * SUBLANES
    rid = lax.broadcasted_iota(I32, (rec_rows, rp), 0)
    rec = jnp.zeros((rec_rows, rp), F32)
    for j, piece in enumerate(pieces):
        rec = jnp.where(rid == j, piece, rec)
    rec = jnp.concatenate([rec, jnp.zeros((LANES - rec_rows, rp), F32)], axis=0)
    route_ref[...] = rec.T[:rows]

    @pl.when(l == pl.num_programs(1) - 1)
    def _():
        for hh in range(GLA_HEADS):
            t_h = st_sc[hh * GLA_DV:(hh + 1) * GLA_DV, (hh // 2) * LANES:(hh // 2 + 1) * LANES]
            sto_ref[0, hh] = t_h.T[(hh % 2) * GLA_DK:(hh % 2 + 1) * GLA_DK, :]


def _premoe_consts(rows):
    sub = min(rows, PREMOE_SUB)
    i = np.arange(sub)[:, None]
    j = np.arange(sub)[None, :]
    same = (i // CHUNK) == (j // CHUNK)
    tri = (same & (j <= i)).astype(np.float32)
    je = np.arange(sub + POOL_ROWS)[None, :]
    band = np.stack([((je <= i + POOL_ROWS) & (je > i + POOL_ROWS - w)).astype(np.float32) for w in POOL_WINDOWS])
    rp = max(rows, LANES)
    utri = (np.arange(rp)[:, None] < np.arange(rp)[None, :]).astype(np.float32)
    return jnp.asarray(tri, BF16), jnp.asarray(utri, BF16), jnp.asarray(band, BF16)


def _premoe(x2d, batch, seq, hist_len, s0, pool0, kmem, vmem, cnt0, w):
    rows = min(PREMOE_ROWS, seq)
    nl = seq // rows
    t = batch * seq
    hist = min(int(hist_len), POOL_BUF)
    tri, utri, band = _premoe_consts(rows)
    row_spec = pl.BlockSpec((rows, D_MODEL), lambda b, l: (b * nl + l, 0))
    lane_spec = pl.BlockSpec((rows, LANES), lambda b, l: (b * nl + l, 0))

    def per_batch(shape):
        return pl.BlockSpec((1,) + shape, lambda b, l: (b, 0, 0))

    consts = [w["norm_mix"], w["w_qkvg"], w["w_u"], w["w_gate"], w["b_gate"], w["gla_gain"], w["pool_w"],
              w["pool_scale"], w["w_out"], w["norm_mem_q"], w["w_mq"], w["w_mo"], w["norm_moe"], w["w_router"],
              w["b_router"], tri, utri, band]
    state_spec = pl.BlockSpec((1, GLA_HEADS, GLA_DK, GLA_DV), lambda b, l: (b, 0, 0, 0))
    in_specs = [row_spec, state_spec, per_batch((POOL_ROWS, POOL_WIDTH)),
                per_batch((N_MEM, D_MODEL)), per_batch((N_MEM, D_MODEL)), _const_spec(cnt0.shape)]
    in_specs += [_const_spec(c.shape) for c in consts]
    out_specs = [row_spec, lane_spec, pl.BlockSpec((N_EXPERTS, LANES), lambda b, l: (0, 0)),
                 state_spec, per_batch((POOL_ROWS, POOL_WIDTH))]
    out_shape = [jax.ShapeDtypeStruct((t, D_MODEL), F32), jax.ShapeDtypeStruct((t, LANES), F32),
                 jax.ShapeDtypeStruct((N_EXPERTS, LANES), F32),
                 jax.ShapeDtypeStruct((batch, GLA_HEADS, GLA_DK, GLA_DV), F32),
                 jax.ShapeDtypeStruct((batch, POOL_ROWS, POOL_WIDTH), F32)]
    return pl.pallas_call(
        functools.partial(_premoe_body, hist, rows),
        grid=(batch, nl),
        in_specs=in_specs,
        out_specs=out_specs,
        out_shape=out_shape,
        scratch_shapes=[pltpu.VMEM((GLA_WIDTH, GLA_KW), F32), pltpu.VMEM((POOL_ROWS, POOL_WIDTH), F32),
                        pltpu.VMEM((N_EXPERTS, LANES), F32)],
        compiler_params=pltpu.CompilerParams(dimension_semantics=("arbitrary", "arbitrary"),
                                             vmem_limit_bytes=VMEM_LIMIT),
        name="premoe",
    )(x2d, s0, pool0, kmem, vmem, cnt0, *consts)


def _route_tables(routes, counts_f, tm, nb):
    e = jnp.concatenate([r[:, ROUTE_EXPERT:ROUTE_EXPERT + TOP_K] for r in routes], axis=0).astype(I32)
    rank = jnp.concatenate([r[:, ROUTE_RANK:ROUTE_RANK + TOP_K] for r in routes], axis=0).astype(I32)
    counts = counts_f[:, 0].astype(I32)
    padded = ((counts + tm - 1) // tm) * tm
    pend = jnp.cumsum(padded)
    pstart = pend - padded
    eids = jnp.arange(N_EXPERTS, dtype=I32)
    dest = jnp.sum(jnp.where(e[:, :, None] == eids, pstart, 0), axis=-1) + rank
    blk_start = jnp.arange(nb, dtype=I32) * tm
    block_e = jnp.minimum(jnp.sum((pend[None, :] <= blk_start[:, None]).astype(I32), axis=1), N_EXPERTS - 1)
    n_used = (pend[-1] // tm).astype(I32)
    run_end = pend[block_e] // tm
    next_e = jnp.where(run_end < n_used, block_e[jnp.minimum(run_end, nb - 1)], -1)
    return dest.astype(I32), block_e.astype(I32), next_e.astype(I32), n_used.reshape(1), pend.astype(I32), counts


def _dispatch_body(rows, tm, nb, na, nu_ref, pend_ref, cnt_ref, dest_ref, xa_ref, xb_ref, g_ref, xg_hbm,
                   hbuf, zbuf, sem, zsem):
    i = pl.program_id(0)
    n_steps = pl.num_programs(0)
    slot = lax.rem(i, 2)
    nu = nu_ref[0]
    groups = rows // SUBLANES

    def tile(ref, row):
        return ref.at[pl.ds(pl.multiple_of(row * TILE_ROWS, TILE_ROWS), TILE_ROWS)]

    def zero_copy(start):
        return pltpu.make_async_copy(zbuf, xg_hbm.at[pl.ds(pl.multiple_of(start * TILE_ROWS, TILE_ROWS), tm * TILE_ROWS)],
                                     zsem)

    def for_each_fill(fn):
        def per_expert(e, c):
            @pl.when(cnt_ref[e] > 0)
            def _():
                fn(pend_ref[e] - tm)
            return c
        lax.fori_loop(0, N_EXPERTS, per_expert, 0)

        def per_tail(b, c):
            fn(b * tm)
            return c
        lax.fori_loop(nu, nb, per_tail, 0)

    @pl.when(i == 0)
    def _():
        zbuf[...] = jnp.zeros_like(zbuf)
        for_each_fill(lambda start: zero_copy(start).start())
        for_each_fill(lambda start: zero_copy(start).wait())

    def row_wait(s):
        for _ in range(TOP_K):
            pltpu.make_async_copy(hbuf.at[s], xg_hbm.at[pl.ds(0, rows * TILE_ROWS)], sem.at[s]).wait()

    @pl.when(i >= 2)
    def _():
        row_wait(slot)

    x = jnp.where(i < na, xa_ref[...], xb_ref[...])
    _store_token_tiles(hbuf.at[slot], _rms(x, g_ref[...]))

    def send_group(gi, c):
        base = gi * (SUBLANES * TOP_K)
        for r in range(SUBLANES):
            for kk in range(TOP_K):
                dst = dest_ref[0, 0, base + r * TOP_K + kk]
                pltpu.make_async_copy(tile(hbuf.at[slot], gi * SUBLANES + r), tile(xg_hbm, dst),
                                      sem.at[slot]).start(priority=kk % 2)
        return c
    lax.fori_loop(0, groups, send_group, 0)

    @pl.when(i == n_steps - 1)
    def _():
        row_wait(slot)

        @pl.when(i >= 1)
        def _():
            row_wait(1 - slot)


def _dispatch(x2a, x2b, dest, n_used, pend, counts, g, tm, nb):
    rows = DISPATCH_ROWS
    na, nbs = x2a.shape[0] // rows, x2b.shape[0] // rows
    steps = na + nbs
    dest3 = dest.reshape(steps, 1, rows * TOP_K)
    grid_spec = pltpu.PrefetchScalarGridSpec(
        num_scalar_prefetch=3,
        grid=(steps,),
        in_specs=[
            pl.BlockSpec((1, 1, rows * TOP_K), lambda i, *_: (i, 0, 0), memory_space=pltpu.SMEM),
            pl.BlockSpec((rows, D_MODEL), lambda i, *_: (jnp.minimum(i, na - 1), 0)),
            pl.BlockSpec((rows, D_MODEL), lambda i, *_: (jnp.maximum(i - na, 0), 0)),
            pl.BlockSpec((1, D_MODEL), lambda i, *_: (0, 0)),
        ],
        out_specs=pl.BlockSpec(memory_space=pl.ANY),
        scratch_shapes=[pltpu.VMEM((2, rows * TILE_ROWS, LANES), F32), pltpu.VMEM((tm * TILE_ROWS, LANES), F32),
                        pltpu.SemaphoreType.DMA((2,)), pltpu.SemaphoreType.DMA(())],
    )
    return pl.pallas_call(
        functools.partial(_dispatch_body, rows, tm, nb, na),
        grid_spec=grid_spec,
        out_shape=jax.ShapeDtypeStruct((nb * tm * TILE_ROWS, LANES), F32),
        compiler_params=pltpu.CompilerParams(dimension_semantics=("arbitrary",), vmem_limit_bytes=VMEM_LIMIT),
        name="dispatch",
    )(n_used, pend, counts, dest3, x2a, x2b, g)


def _moe_body(tm, nu_ref, be_ref, nxt_ref, xg_ref, wup_hbm, bup_ref, wdn_hbm, bdn_ref, ys_ref,
              wup_st, wdn_st, wup_bf, wdn_bf, wsem):
    b = pl.program_id(0)
    in_use = b < nu_ref[0]

    def fetch_up(e):
        return pltpu.make_async_copy(wup_hbm.at[e], wup_st, wsem.at[0])

    def fetch_dn(e):
        return pltpu.make_async_copy(wdn_hbm.at[e], wdn_st, wsem.at[1])

    @pl.when(b == 0)
    def _():
        fetch_up(be_ref[0]).start()
        fetch_dn(be_ref[0]).start()

    @pl.when(in_use & ((b == 0) | (be_ref[b] != be_ref[jnp.maximum(b - 1, 0)])))
    def _():
        fetch_up(be_ref[b]).wait()
        fetch_dn(be_ref[b]).wait()
        wup_bf[...] = wup_st[...].astype(BF16)
        wdn_bf[...] = wdn_st[...].astype(BF16)

        @pl.when(nxt_ref[b] >= 0)
        def _():
            fetch_up(nxt_ref[b]).start()
            fetch_dn(nxt_ref[b]).start()

    @pl.when(in_use)
    def _():
        hh = _load_token_tiles(xg_ref, tm).astype(BF16)
        uu = jnp.dot(hh, wup_bf[...], preferred_element_type=F32) + bup_ref[0]
        glu = jnp.minimum(uu[:, :D_FF], SWIGLU_LIMIT)
        lin = jnp.clip(uu[:, D_FF:], -SWIGLU_LIMIT, SWIGLU_LIMIT)
        act = glu * jax.nn.sigmoid(SWIGLU_ALPHA * glu) * (lin + 1.0)
        _store_token_tiles(ys_ref, jnp.dot(act.astype(BF16), wdn_bf[...], preferred_element_type=F32) + bdn_ref[0])

    @pl.when(b >= nu_ref[0])
    def _():
        ys_ref[...] = jnp.zeros_like(ys_ref)


def _moe(xg, block_e, next_e, n_used, w, tm, nb):
    def used(b, nu):
        return jnp.minimum(b, nu[0] - 1)

    def per_expert(shape):
        return pl.BlockSpec((1,) + shape, lambda b, nu, be, nxt: (be[used(b, nu)], 0, 0))

    grid_spec = pltpu.PrefetchScalarGridSpec(
        num_scalar_prefetch=3,
        grid=(nb,),
        in_specs=[
            pl.BlockSpec((tm * TILE_ROWS, LANES), lambda b, nu, be, nxt: (used(b, nu), 0)),
            pl.BlockSpec(memory_space=pl.ANY), per_expert((1, 2 * D_FF)),
            pl.BlockSpec(memory_space=pl.ANY), per_expert((1, D_MODEL)),
        ],
        out_specs=pl.BlockSpec((tm * TILE_ROWS, LANES), lambda b, nu, be, nxt: (b, 0)),
        scratch_shapes=[pltpu.VMEM((D_MODEL, 2 * D_FF), F32), pltpu.VMEM((D_FF, D_MODEL), F32),
                        pltpu.VMEM((D_MODEL, 2 * D_FF), BF16), pltpu.VMEM((D_FF, D_MODEL), BF16),
                        pltpu.SemaphoreType.DMA((2,))],
    )
    return pl.pallas_call(
        functools.partial(_moe_body, tm),
        grid_spec=grid_spec,
        out_shape=jax.ShapeDtypeStruct((nb * tm * TILE_ROWS, LANES), F32),
        compiler_params=pltpu.CompilerParams(dimension_semantics=("arbitrary",), vmem_limit_bytes=VMEM_LIMIT),
        name="moe",
    )(n_used, block_e, next_e, xg, w["w_up"], w["b_up"], w["w_down"], w["b_down"])


def _combine_body(rows, na, destc_ref, destn_ref, xa_ref, xb_ref, ra_ref, rb_ref, g_ref, ys_hbm, oa_ref, ob_ref,
                  gbuf, sem):
    i = pl.program_id(0)
    n_steps = pl.num_programs(0)
    slot = lax.rem(i, 2)
    groups = rows // SUBLANES

    def fetch(dest_ref, s):
        def fetch_group(gi, c):
            base = gi * (SUBLANES * TOP_K)
            srcs = [dest_ref[0, 0, base + j] for j in range(SUBLANES * TOP_K)]
            for r in range(SUBLANES):
                for kk in range(TOP_K):
                    src = srcs[r * TOP_K + kk]
                    pltpu.make_async_copy(
                        ys_hbm.at[pl.ds(pl.multiple_of(src * TILE_ROWS, TILE_ROWS), TILE_ROWS)],
                        gbuf.at[s, kk, gi, :, r, :],
                        sem.at[s]).start(priority=kk % 2)
            return c
        lax.fori_loop(0, groups, fetch_group, 0)

    @pl.when(i == 0)
    def _():
        fetch(destc_ref, 0)

    @pl.when(i + 1 < n_steps)
    def _():
        fetch(destn_ref, 1 - slot)

    for kk in range(TOP_K):
        for r in range(SUBLANES):
            pltpu.make_async_copy(ys_hbm.at[pl.ds(0, groups * TILE_ROWS)].reshape(groups, TILE_ROWS, LANES),
                                  gbuf.at[slot, kk, :, :, r, :], sem.at[slot]).wait()

    first = i < na
    gate = jnp.where(first, ra_ref[...], rb_ref[...])
    acc = jnp.where(first, xa_ref[...], xb_ref[...])
    for kk in range(TOP_K):
        acc = acc + _plain_rows(gbuf[slot, kk]) * gate[:, ROUTE_GATE + kk:ROUTE_GATE + kk + 1]
    out = _rms(acc, g_ref[...])

    @pl.when(first)
    def _():
        oa_ref[...] = out

    @pl.when(jnp.logical_not(first))
    def _():
        ob_ref[...] = out


def _combine(x2a, x2b, ys, dest, route_a, route_b, g):
    rows = COMBINE_ROWS
    na, nbs = x2a.shape[0] // rows, x2b.shape[0] // rows
    steps = na + nbs
    dest3 = dest.reshape(steps, 1, rows * TOP_K)

    def seg_a(width):
        return pl.BlockSpec((rows, width), lambda i: (jnp.minimum(i, na - 1), 0))

    def seg_b(width):
        return pl.BlockSpec((rows, width), lambda i: (jnp.maximum(i - na, 0), 0))

    smem_cur = pl.BlockSpec((1, 1, rows * TOP_K), lambda i: (i, 0, 0), memory_space=pltpu.SMEM)
    smem_next = pl.BlockSpec((1, 1, rows * TOP_K), lambda i: (jnp.minimum(i + 1, steps - 1), 0, 0),
                             memory_space=pltpu.SMEM)
    return pl.pallas_call(
        functools.partial(_combine_body, rows, na),
        grid=(steps,),
        in_specs=[smem_cur, smem_next, seg_a(D_MODEL), seg_b(D_MODEL), seg_a(LANES), seg_b(LANES),
                  pl.BlockSpec((1, D_MODEL), lambda i: (0, 0)),
                  pl.BlockSpec(memory_space=pl.ANY)],
        out_specs=[seg_a(D_MODEL), seg_b(D_MODEL)],
        out_shape=[jax.ShapeDtypeStruct(x2a.shape, F32), jax.ShapeDtypeStruct(x2b.shape, F32)],
        scratch_shapes=[pltpu.VMEM((2, TOP_K, rows // SUBLANES, TILE_ROWS, SUBLANES, LANES), F32), pltpu.SemaphoreType.DMA((2,))],
        compiler_params=pltpu.CompilerParams(dimension_semantics=("arbitrary",), vmem_limit_bytes=VMEM_LIMIT),
        name="combine",
    )(dest3, dest3, x2a, x2b, route_a, route_b, g, ys)


def _moe_layer(x2a, x2b, route_a, route_b, counts_f, w, norm_final):
    t = x2a.shape[0] + x2b.shape[0]
    tm = MOE_ROWS
    nb = (t * TOP_K + N_EXPERTS * (tm - 1) + tm - 1) // tm
    dest, block_e, next_e, n_used, pend, counts = _route_tables([route_a, route_b], counts_f, tm, nb)
    xg = _dispatch(x2a, x2b, dest, n_used, pend, counts, w["norm_moe"], tm, nb)
    ys = _moe(xg, block_e, next_e, n_used, w, tm, nb)
    return _combine(x2a, x2b, ys, dest, route_a, route_b, norm_final.reshape(1, -1))


def _prep_weights(norm_mix, w_in, w_gate, b_gate, gla_gain, pool_w, pool_scale, w_out, norm_mem_q, w_mq, w_mo,
                  norm_moe, w_router, b_router, w_up, b_up, w_down, b_down):
    n_qkvg = 2 * GLA_KW + 2 * GLA_WIDTH
    w_qkvg = jnp.zeros((D_MODEL, n_qkvg + LANES), BF16).at[:, :n_qkvg + GATE_RANK].set(
        w_in[:, :n_qkvg + GATE_RANK].astype(BF16))
    w_g = jnp.zeros((LANES, GLA_KW), BF16).at[:GATE_RANK].set(w_gate.astype(BF16))
    return {
        "norm_mix": norm_mix.reshape(1, -1),
        "w_qkvg": w_qkvg,
        "w_u": w_in[:, n_qkvg + GATE_RANK:].astype(BF16),
        "w_gate": w_g,
        "b_gate": b_gate.reshape(1, -1),
        "gla_gain": gla_gain.reshape(1, -1),
        "pool_w": pool_w.astype(BF16),
        "pool_scale": pool_scale.reshape(1, -1),
        "w_out": w_out.astype(BF16),
        "norm_mem_q": norm_mem_q.reshape(1, -1),
        "w_mq": w_mq.astype(BF16),
        "w_mo": w_mo.astype(BF16),
        "norm_moe": norm_moe.reshape(1, -1),
        "w_router": w_router.T.astype(BF16),
        "b_router": b_router.reshape(-1, 1),
        "w_up": w_up,
        "b_up": b_up.reshape(N_EXPERTS, 1, -1),
        "w_down": w_down,
        "b_down": b_down.reshape(N_EXPERTS, 1, -1),
    }


def _mix(x, s_gla, pool_prev, hist_len, mk, mv, cnt0, w):
    batch, seq, _ = x.shape
    x2d = x.reshape(batch * seq, D_MODEL)
    pool0 = jnp.concatenate([jnp.zeros((batch, 1, POOL_WIDTH), F32), pool_prev.astype(F32)], axis=1)
    x2, route, cnt, s_new, pool_t = _premoe(x2d, batch, seq, hist_len, s_gla.astype(F32), pool0,
                                            mk.reshape(batch, N_MEM, D_MODEL), mv.reshape(batch, N_MEM, D_MODEL),
                                            cnt0, w)
    return x2, route, cnt, s_new, pool_t[:, 1:, :]


def kernel(x_prompt, x_sample, mem_prompt, state_gla, state_pool, cache_mem_k, cache_mem_v, norm_mix, w_in, w_gate, b_gate, gla_gain, pool_w, pool_scale, w_out, norm_mem_q, norm_mem_kv, w_mq, w_mk, w_mv, w_mo, norm_moe, w_router, b_router, w_up, b_up, w_down, b_down, norm_final):
    depth = w_in.shape[0]
    assert depth == 1
    xp, xs = x_prompt, x_sample
    bp = xp.shape[0]
    gla_p, pool_p, mk_p, mv_p, gla_s, pool_s = [], [], [], [], [], []
    for l in range(depth):
        w = _prep_weights(norm_mix[l], w_in[l], w_gate[l], b_gate[l], gla_gain[l], pool_w[l], pool_scale[l], w_out[l],
                          norm_mem_q[l], w_mq[l], w_mo[l], norm_moe[l], w_router[l], b_router[l],
                          w_up[l], b_up[l], w_down[l], b_down[l])
        mk2, mv2 = _mem_kv(mem_prompt.reshape(bp * N_MEM, D_MODEL), norm_mem_kv[l].reshape(1, -1),
                           w_mk[l].astype(BF16), w_mv[l].astype(BF16))
        mk = mk2.reshape(bp, N_MEM, MEM_HEADS, MEM_HD)
        mv = mv2.reshape(bp, N_MEM, MEM_HEADS, MEM_HD)
        cnt0 = jnp.zeros((N_EXPERTS, LANES), F32)
        x2s, route_s, cnt_s, ss, ps = _mix(xs, state_gla[l], state_pool[l], PAST_LEN, cache_mem_k[l], cache_mem_v[l],
                                           cnt0, w)
        gla_s.append(ss)
        pool_s.append(ps)
        s0 = jnp.zeros((bp, GLA_HEADS, GLA_DK, GLA_DV), F32)
        p0 = jnp.zeros((bp, POOL_BUF, POOL_WIDTH), F32)
        x2p, route_p, cnt_all, sp, pp = _mix(xp, s0, p0, 0, mk, mv, cnt_s, w)
        ys, yp = _moe_layer(x2s, x2p, route_s, route_p, cnt_all, w, norm_final)
        xs, xp = ys.reshape(xs.shape), yp.reshape(xp.shape)
        gla_p.append(sp)
        pool_p.append(pp)
        mk_p.append(mk)
        mv_p.append(mv)
    return (xp, xs, jnp.stack(gla_p), jnp.stack(pool_p), jnp.stack(mk_p), jnp.stack(mv_p),
            jnp.stack(gla_s), jnp.stack(pool_s))
```

```python
import functools

import numpy as np
import jax
import jax.numpy as jnp
from jax import lax
from jax.experimental import pallas as pl
from jax.experimental.pallas import tpu as pltpu

F32 = jnp.float32
BF16 = jnp.bfloat16
I32 = jnp.int32

D_MODEL = 1024
CHUNK = 64
GLA_HEADS = 4
GLA_DK = 64
GLA_DV = 128
GLA_KW = GLA_HEADS * GLA_DK
GLA_WIDTH = GLA_HEADS * GLA_DV
GATE_RANK = 16
GATE_TAU = 16.0
POOL_WIDTH = 512
POOL_GC = 128
POOL_WINDOWS = (2, 4, 8, 16)
POOL_BUF = 15
PAST_LEN = 4096
POOL_ROWS = 16
N_MEM = 256
MEM_HEADS = 4
MEM_HD = 256
N_EXPERTS = 32
TOP_K = 4
D_FF = 1024
SWIGLU_ALPHA = 1.702
SWIGLU_LIMIT = 7.0
EPS = 1e-6
LANES = 128
SUBLANES = 8
TILE_ROWS = D_MODEL // LANES
ROUTE_EXPERT, ROUTE_RANK, ROUTE_GATE = 0, 4, 8
ROUTE_ROWS = 16

PREMOE_ROWS = 512
PREMOE_SUB = 256
MOE_ROWS = 512
DISPATCH_ROWS = 256
COMBINE_ROWS = 256
VMEM_LIMIT = 56 * 1024 * 1024

assert 2 * GLA_DK == LANES and GLA_DV == LANES

NT_DIMS = (((1,), (1,)), ((), ()))
TN_DIMS = (((0,), (0,)), ((), ()))


def _rms(x, g):
    ms = jnp.mean(x * x, axis=-1, keepdims=True)
    return x * lax.rsqrt(ms + EPS) * g


def _load_token_tiles(ref, rows):
    return jnp.concatenate([ref[pl.ds(s, rows, stride=TILE_ROWS), :] for s in range(TILE_ROWS)], axis=1)


def _plain_rows(v):
    g = v.shape[0]
    return jnp.concatenate([v[:, c].reshape(g * SUBLANES, LANES) for c in range(TILE_ROWS)], axis=1)


def _store_token_tiles(ref, val):
    rows = val.shape[0]
    for s in range(TILE_ROWS):
        ref[pl.ds(s, rows, stride=TILE_ROWS), :] = val[:, s * LANES:(s + 1) * LANES]


def _const_spec(shape):
    nd = len(shape)
    return pl.BlockSpec(shape, lambda *_: (0,) * nd, pipeline_mode=pl.Buffered(1))


def _memkv_body(m_ref, g_ref, wk_ref, wv_ref, k_ref, v_ref):
    m = _rms(m_ref[...], g_ref[...]).astype(BF16)
    k_ref[...] = jnp.dot(m, wk_ref[...], preferred_element_type=F32)
    v_ref[...] = jnp.dot(m, wv_ref[...], preferred_element_type=F32)


def _mem_kv(mem2d, g, wk, wv):
    n = mem2d.shape[0]
    tm = 512
    row = pl.BlockSpec((tm, D_MODEL), lambda i: (i, 0))
    return pl.pallas_call(
        _memkv_body,
        grid=(n // tm,),
        in_specs=[row, _const_spec((1, D_MODEL)), _const_spec((D_MODEL, D_MODEL)), _const_spec((D_MODEL, D_MODEL))],
        out_specs=[row, row],
        out_shape=[jax.ShapeDtypeStruct((n, D_MODEL), F32)] * 2,
        compiler_params=pltpu.CompilerParams(dimension_semantics=("arbitrary",), vmem_limit_bytes=VMEM_LIMIT),
        name="mem_kv",
    )(mem2d, g, wk, wv)


def _premoe_body(hist, rows,
                 x_ref, s0_ref, p0_ref, k_ref, v_ref, cnt0_ref,
                 nmix_ref, wqkvg_ref, wu_ref, wgate_ref, bgate_ref, ggain_ref, poolw_ref, pscale_ref,
                 wout_ref, nq_ref, wmq_ref, wmo_ref, nmoe_ref, wrt_ref, brt_ref,
                 tri_ref, utri_ref, band_ref,
                 x2_ref, route_ref, cnt_ref, sto_ref, po_ref, *rest):
    st_sc, prev_sc, cnt_sc = rest[-3:]
    rec_t_ref = rest[0] if len(rest) == 4 else None
    b = pl.program_id(0)
    l = pl.program_id(1)
    n_chunks = rows // CHUNK
    sub = min(rows, PREMOE_SUB)
    subs = [slice(s0, s0 + sub) for s0 in range(0, rows, sub)]

    zeros_dv = jnp.zeros((GLA_DV, LANES), F32)
    half = jnp.zeros((GLA_DK, GLA_DV), F32)

    @pl.when(l == 0)
    def _():
        row_blocks = []
        for hh in range(GLA_HEADS):
            s_h = s0_ref[0, hh]
            padded = jnp.concatenate([s_h, half] if hh % 2 == 0 else [half, s_h], axis=0)
            t_h = padded.T
            row_blocks.append(jnp.concatenate([t_h, zeros_dv] if hh // 2 == 0 else [zeros_dv, t_h], axis=1))
        st_sc[...] = jnp.concatenate(row_blocks, axis=0)
        prow = lax.broadcasted_iota(I32, (POOL_ROWS, POOL_WIDTH), 0)
        prev_sc[...] = jnp.where(prow >= POOL_ROWS - hist, p0_ref[0], 0.0)

    @pl.when((b == 0) & (l == 0))
    def _():
        cnt_sc[...] = cnt0_ref[...]

    x = x_ref[...]
    h = _rms(x, nmix_ref[...]).astype(BF16)
    z = jnp.dot(h, wqkvg_ref[...], preferred_element_type=F32)
    q = z[:, 0:GLA_KW] * (GLA_DK ** -0.5)
    k = z[:, GLA_KW:2 * GLA_KW]
    v = z[:, 2 * GLA_KW:2 * GLA_KW + GLA_WIDTH]
    g = z[:, 2 * GLA_KW + GLA_WIDTH:2 * GLA_KW + 2 * GLA_WIDTH]
    r = z[:, 2 * GLA_KW + 2 * GLA_WIDTH:]

    gp = jnp.dot(r.astype(BF16), wgate_ref[...], preferred_element_type=F32) + bgate_ref[...]
    la = jax.nn.log_sigmoid(gp) * (1.0 / GATE_TAU)

    hi = la.astype(BF16)
    r1 = la - hi.astype(F32)
    mid = r1.astype(BF16)
    lo = (r1 - mid.astype(F32)).astype(BF16)
    la3 = jnp.concatenate([hi, mid, lo], axis=1)
    bcum_parts = []
    for sl in subs:
        bb = jnp.dot(tri_ref[...], la3[sl], preferred_element_type=F32)
        bcum_parts.append((bb[:, 0:GLA_KW] + bb[:, GLA_KW:2 * GLA_KW]) + bb[:, 2 * GLA_KW:])
    bcum = jnp.concatenate(bcum_parts, axis=0)
    btot = jnp.concatenate([jnp.broadcast_to(bcum[(n + 1) * CHUNK - 1:(n + 1) * CHUNK, :], (CHUNK, GLA_KW))
                            for n in range(n_chunks)], axis=0)

    qd = q * jnp.exp(bcum)
    kd = k * jnp.exp(-bcum)
    kl = k * jnp.exp(btot - bcum)

    lane_kw = lax.broadcasted_iota(I32, (rows, GLA_KW), 1)
    rowi = lax.broadcasted_iota(I32, (sub, sub), 0)
    coli = lax.broadcasted_iota(I32, (sub, sub), 1)
    amask = (coli <= rowi) & (coli >= (rowi & ~(CHUNK - 1)))
    vb = v.astype(BF16)

    o_heads = []
    for hh in range(GLA_HEADS):
        mh = (lane_kw // GLA_DK) == hh
        qh = jnp.where(mh, qd, 0.0).astype(BF16)
        kh = jnp.where(mh, kd, 0.0).astype(BF16)
        o_sub = []
        for sl in subs:
            a = lax.dot_general(qh[sl], kh[sl], NT_DIMS, preferred_element_type=F32)
            a = jnp.where(amask, a, 0.0).astype(BF16)
            o_sub.append(jnp.dot(a, vb[sl, hh * GLA_DV:(hh + 1) * GLA_DV], preferred_element_type=F32))
        o_heads.append(jnp.concatenate(o_sub, axis=0))
    o_intra = jnp.concatenate(o_heads, axis=1)

    srow = lax.broadcasted_iota(I32, (GLA_WIDTH, GLA_KW), 0)
    scol = lax.broadcasted_iota(I32, (GLA_WIDTH, GLA_KW), 1)
    smask = (srow // GLA_DV) == (scol // GLA_DK)
    qdb = qd.astype(BF16)
    klb = kl.astype(BF16)
    st = st_sc[...]
    oi_parts = []
    for n in range(n_chunks):
        lo_r, hi_r = n * CHUNK, (n + 1) * CHUNK
        oi_parts.append(lax.dot_general(qdb[lo_r:hi_r], st.astype(BF16), NT_DIMS, preferred_element_type=F32))
        upd = lax.dot_general(vb[lo_r:hi_r], klb[lo_r:hi_r], TN_DIMS, preferred_element_type=F32)
        dec = jnp.exp(btot[lo_r:lo_r + 1, :])
        st = st * dec + jnp.where(smask, upd, 0.0)
    st_sc[...] = st
    o_inter = oi_parts[0] if n_chunks == 1 else jnp.concatenate(oi_parts, axis=0)
    o = o_intra + o_inter

    gains = ggain_ref[...]
    on_parts = []
    for hh in range(GLA_HEADS):
        oh = o[:, hh * GLA_DV:(hh + 1) * GLA_DV]
        ms = jnp.mean(oh * oh, axis=-1, keepdims=True)
        on_parts.append(oh * lax.rsqrt(ms + EPS) * gains[:, hh * GLA_DV:(hh + 1) * GLA_DV])
    on = jnp.concatenate(on_parts, axis=1) * (g * jax.nn.sigmoid(g))

    u = jnp.dot(h, wu_ref[...], preferred_element_type=F32)
    prevs = [prev_sc[...]] + [u[sl.stop - POOL_ROWS:sl.stop] for sl in subs[:-1]]
    exts = [jnp.concatenate([pv, u[sl]], axis=0).astype(BF16) for pv, sl in zip(prevs, subs)]
    pos = l * rows + lax.broadcasted_iota(I32, (rows, POOL_GC), 0)
    p_parts = []
    for gi, w in enumerate(POOL_WINDOWS):
        cols = slice(gi * POOL_GC, (gi + 1) * POOL_GC)
        s = jnp.concatenate([jnp.dot(band_ref[gi], ext[:, cols], preferred_element_type=F32) for ext in exts], axis=0)
        cnt_w = jnp.minimum(w, pos + 1 + hist).astype(F32)
        dd = s / cnt_w - u[:, cols]
        p_parts.append(jnp.dot(dd.astype(BF16), poolw_ref[gi], preferred_element_type=F32))
    p = jnp.concatenate(p_parts, axis=1) * pscale_ref[...]
    tail = u[rows - POOL_ROWS:rows, :]
    prev_sc[...] = tail
    po_ref[0] = tail

    cat = jnp.concatenate([on, p], axis=1).astype(BF16)
    x1 = x + jnp.dot(cat, wout_ref[...], preferred_element_type=F32)

    h2 = _rms(x1, nq_ref[...]).astype(BF16)
    qm = jnp.dot(h2, wmq_ref[...], preferred_element_type=F32).astype(BF16)
    kk = k_ref[0].astype(BF16)
    vv = v_ref[0].astype(BF16)
    a_parts = []
    for hh in range(MEM_HEADS):
        cols = slice(hh * MEM_HD, (hh + 1) * MEM_HD)
        s = lax.dot_general(qm[:, cols], kk[:, cols], NT_DIMS, preferred_element_type=F32) * (MEM_HD ** -0.5)
        e = jnp.exp(s - jnp.max(s, axis=-1, keepdims=True))
        pr = e / jnp.sum(e, axis=-1, keepdims=True)
        a_parts.append(jnp.dot(pr.astype(BF16), vv[:, cols], preferred_element_type=F32))
    att = jnp.concatenate(a_parts, axis=1).astype(BF16)
    x2 = x1 + jnp.dot(att, wmo_ref[...], preferred_element_type=F32)
    x2_ref[...] = x2

    rp = max(rows, LANES)
    h3 = _rms(x2, nmoe_ref[...]).astype(BF16)
    if rp > rows:
        h3 = jnp.concatenate([h3, jnp.zeros((rp - rows, D_MODEL), BF16)], axis=0)
    logits = lax.dot_general(wrt_ref[...], h3, NT_DIMS, preferred_element_type=F32) + brt_ref[...]
    eid = lax.broadcasted_iota(I32, (N_EXPERTS, rp), 0).astype(F32)
    work = logits
    vals, idxs = [], []
    for _ in range(TOP_K):
        m = jnp.max(work, axis=0, keepdims=True)
        idx = jnp.min(jnp.where(work == m, eid, float(N_EXPERTS)), axis=0, keepdims=True)
        vals.append(m)
        idxs.append(idx)
        work = jnp.where(eid == idx, -jnp.inf, work)
    exps = [jnp.exp(vk - vals[0]) for vk in vals]
    den = (exps[0] + exps[1]) + (exps[2] + exps[3])
    gates = [ek / den for ek in exps]

    valid = lax.broadcasted_iota(I32, (N_EXPERTS, rp), 1) < rows
    hot = jnp.zeros((N_EXPERTS, rp), F32)
    for idx in idxs:
        hot = hot + jnp.where((eid == idx) & valid, 1.0, 0.0)
    prefix = jnp.dot(hot.astype(BF16), utri_ref[...], preferred_element_type=F32) + cnt_sc[:, 0:1]
    ranks = [jnp.sum(jnp.where(eid == idx, prefix, 0.0), axis=0, keepdims=True) for idx in idxs]
    cnt_sc[...] = cnt_sc[...] + jnp.sum(hot, axis=1, keepdims=True)
    cnt_ref[...] = cnt_sc[...]

    pieces = idxs + ranks + gates
    rec_rows = ROUTE_ROWS
    rid = lax.broadcasted_iota(I32, (rec_rows, rp), 0)
    rec = jnp.zeros((rec_rows, rp), F32)
    for j, piece in enumerate(pieces):
        rec = jnp.where(rid == j, piece, rec)
    if rec_t_ref is not None:
        rec_t_ref[...] = rec
    rec = jnp.concatenate([rec, jnp.zeros((LANES - rec_rows, rp), F32)], axis=0)
    route_ref[...] = rec.T[:rows]

    @pl.when(l == pl.num_programs(1) - 1)
    def _():
        for hh in range(GLA_HEADS):
            t_h = st_sc[hh * GLA_DV:(hh + 1) * GLA_DV, (hh // 2) * LANES:(hh // 2 + 1) * LANES]
            sto_ref[0, hh] = t_h.T[(hh % 2) * GLA_DK:(hh % 2 + 1) * GLA_DK, :]


def _premoe_consts(rows):
    sub = min(rows, PREMOE_SUB)
    i = np.arange(sub)[:, None]
    j = np.arange(sub)[None, :]
    same = (i // CHUNK) == (j // CHUNK)
    tri = (same & (j <= i)).astype(np.float32)
    je = np.arange(sub + POOL_ROWS)[None, :]
    band = np.stack([((je <= i + POOL_ROWS) & (je > i + POOL_ROWS - w)).astype(np.float32) for w in POOL_WINDOWS])
    rp = max(rows, LANES)
    utri = (np.arange(rp)[:, None] < np.arange(rp)[None, :]).astype(np.float32)
    return jnp.asarray(tri, BF16), jnp.asarray(utri, BF16), jnp.asarray(band, BF16)


def _premoe(x2d, batch, seq, hist_len, s0, pool0, kmem, vmem, cnt0, w):
    rows = min(PREMOE_ROWS, seq)
    nl = seq // rows
    t = batch * seq
    hist = min(int(hist_len), POOL_BUF)
    tri, utri, band = _premoe_consts(rows)
    row_spec = pl.BlockSpec((rows, D_MODEL), lambda b, l: (b * nl + l, 0))
    lane_spec = pl.BlockSpec((rows, LANES), lambda b, l: (b * nl + l, 0))

    def per_batch(shape):
        return pl.BlockSpec((1,) + shape, lambda b, l: (b, 0, 0))

    consts = [w["norm_mix"], w["w_qkvg"], w["w_u"], w["w_gate"], w["b_gate"], w["gla_gain"], w["pool_w"],
              w["pool_scale"], w["w_out"], w["norm_mem_q"], w["w_mq"], w["w_mo"], w["norm_moe"], w["w_router"],
              w["b_router"], tri, utri, band]
    state_spec = pl.BlockSpec((1, GLA_HEADS, GLA_DK, GLA_DV), lambda b, l: (b, 0, 0, 0))
    in_specs = [row_spec, state_spec, per_batch((POOL_ROWS, POOL_WIDTH)),
                per_batch((N_MEM, D_MODEL)), per_batch((N_MEM, D_MODEL)), _const_spec(cnt0.shape)]
    in_specs += [_const_spec(c.shape) for c in consts]
    out_specs = [row_spec, lane_spec, pl.BlockSpec((N_EXPERTS, LANES), lambda b, l: (0, 0)),
                 state_spec, per_batch((POOL_ROWS, POOL_WIDTH))]
    out_shape = [jax.ShapeDtypeStruct((t, D_MODEL), F32), jax.ShapeDtypeStruct((t, LANES), F32),
                 jax.ShapeDtypeStruct((N_EXPERTS, LANES), F32),
                 jax.ShapeDtypeStruct((batch, GLA_HEADS, GLA_DK, GLA_DV), F32),
                 jax.ShapeDtypeStruct((batch, POOL_ROWS, POOL_WIDTH), F32)]
    if rows >= LANES:
        out_specs.append(pl.BlockSpec((ROUTE_ROWS, rows), lambda b, l: (0, b * nl + l)))
        out_shape.append(jax.ShapeDtypeStruct((ROUTE_ROWS, t), F32))
    outs = pl.pallas_call(
        functools.partial(_premoe_body, hist, rows),
        grid=(batch, nl),
        in_specs=in_specs,
        out_specs=out_specs,
        out_shape=out_shape,
        scratch_shapes=[pltpu.VMEM((GLA_WIDTH, GLA_KW), F32), pltpu.VMEM((POOL_ROWS, POOL_WIDTH), F32),
                        pltpu.VMEM((N_EXPERTS, LANES), F32)],
        compiler_params=pltpu.CompilerParams(dimension_semantics=("arbitrary", "arbitrary"),
                                             vmem_limit_bytes=VMEM_LIMIT),
        name="premoe",
    )(x2d, s0, pool0, kmem, vmem, cnt0, *consts)
    return tuple(outs) if rows >= LANES else tuple(outs) + (None,)


def _route_tables(routes, recs_t, counts_f, tm, nb, rows):
    counts = counts_f[:, 0].astype(I32)
    padded = ((counts + tm - 1) // tm) * tm
    pend = jnp.cumsum(padded)
    pstart = pend - padded
    eids = jnp.arange(N_EXPERTS, dtype=I32)
    parts = []
    for route, rec_t in zip(routes, recs_t):
        if rec_t is None:
            rec_t = route[:, :ROUTE_RANK + TOP_K].T
        e = rec_t[ROUTE_EXPERT:ROUTE_EXPERT + TOP_K].astype(I32)
        rank = rec_t[ROUTE_RANK:ROUTE_RANK + TOP_K].astype(I32)
        dest = jnp.sum(jnp.where(e[:, :, None] == eids, pstart, 0), axis=-1) + rank
        steps = dest.shape[1] // rows
        parts.append(dest.reshape(TOP_K, steps, rows).transpose(1, 0, 2).reshape(steps, 1, TOP_K * rows))
    dest = jnp.concatenate(parts, axis=0)
    blk_start = jnp.arange(nb, dtype=I32) * tm
    block_e = jnp.minimum(jnp.sum((pend[None, :] <= blk_start[:, None]).astype(I32), axis=1), N_EXPERTS - 1)
    n_used = (pend[-1] // tm).astype(I32)
    run_end = pend[block_e] // tm
    next_e = jnp.where(run_end < n_used, block_e[jnp.minimum(run_end, nb - 1)], -1)
    return dest.astype(I32), block_e.astype(I32), next_e.astype(I32), n_used.reshape(1), pend.astype(I32), counts


def _dispatch_body(rows, tm, nb, na, nu_ref, pend_ref, cnt_ref, dest_ref, xa_ref, xb_ref, g_ref, xg_hbm,
                   hbuf, zbuf, sem, zsem):
    i = pl.program_id(0)
    n_steps = pl.num_programs(0)
    slot = lax.rem(i, 2)
    nu = nu_ref[0]
    groups = rows // SUBLANES

    def tile(ref, row):
        return ref.at[pl.ds(pl.multiple_of(row * TILE_ROWS, TILE_ROWS), TILE_ROWS)]

    def zero_copy(start):
        return pltpu.make_async_copy(zbuf, xg_hbm.at[pl.ds(pl.multiple_of(start * TILE_ROWS, TILE_ROWS), tm * TILE_ROWS)],
                                     zsem)

    def for_each_fill(fn):
        def per_expert(e, c):
            @pl.when(cnt_ref[e] > 0)
            def _():
                fn(pend_ref[e] - tm)
            return c
        lax.fori_loop(0, N_EXPERTS, per_expert, 0)

        def per_tail(b, c):
            fn(b * tm)
            return c
        lax.fori_loop(nu, nb, per_tail, 0)

    @pl.when(i == 0)
    def _():
        zbuf[...] = jnp.zeros_like(zbuf)
        for_each_fill(lambda start: zero_copy(start).start())
        for_each_fill(lambda start: zero_copy(start).wait())

    def row_wait(s):
        for _ in range(TOP_K):
            pltpu.make_async_copy(hbuf.at[s], xg_hbm.at[pl.ds(0, rows * TILE_ROWS)], sem.at[s]).wait()

    @pl.when(i >= 2)
    def _():
        row_wait(slot)

    x = jnp.where(i < na, xa_ref[...], xb_ref[...])
    _store_token_tiles(hbuf.at[slot], _rms(x, g_ref[...]))

    def send_group(gi, c):
        base = gi * SUBLANES
        for r in range(SUBLANES):
            for kk in range(TOP_K):
                dst = dest_ref[0, 0, kk * rows + base + r]
                pltpu.make_async_copy(tile(hbuf.at[slot], gi * SUBLANES + r), tile(xg_hbm, dst),
                                      sem.at[slot]).start(priority=kk % 2)
        return c
    lax.fori_loop(0, groups, send_group, 0)

    @pl.when(i == n_steps - 1)
    def _():
        row_wait(slot)

        @pl.when(i >= 1)
        def _():
            row_wait(1 - slot)


def _dispatch(x2a, x2b, dest, n_used, pend, counts, g, tm, nb):
    rows = DISPATCH_ROWS
    na, nbs = x2a.shape[0] // rows, x2b.shape[0] // rows
    steps = na + nbs
    dest3 = dest
    grid_spec = pltpu.PrefetchScalarGridSpec(
        num_scalar_prefetch=3,
        grid=(steps,),
        in_specs=[
            pl.BlockSpec((1, 1, rows * TOP_K), lambda i, *_: (i, 0, 0), memory_space=pltpu.SMEM),
            pl.BlockSpec((rows, D_MODEL), lambda i, *_: (jnp.minimum(i, na - 1), 0)),
            pl.BlockSpec((rows, D_MODEL), lambda i, *_: (jnp.maximum(i - na, 0), 0)),
            pl.BlockSpec((1, D_MODEL), lambda i, *_: (0, 0)),
        ],
        out_specs=pl.BlockSpec(memory_space=pl.ANY),
        scratch_shapes=[pltpu.VMEM((2, rows * TILE_ROWS, LANES), F32), pltpu.VMEM((tm * TILE_ROWS, LANES), F32),
                        pltpu.SemaphoreType.DMA((2,)), pltpu.SemaphoreType.DMA(())],
    )
    return pl.pallas_call(
        functools.partial(_dispatch_body, rows, tm, nb, na),
        grid_spec=grid_spec,
        out_shape=jax.ShapeDtypeStruct((nb * tm * TILE_ROWS, LANES), F32),
        compiler_params=pltpu.CompilerParams(dimension_semantics=("arbitrary",), vmem_limit_bytes=VMEM_LIMIT),
        name="dispatch",
    )(n_used, pend, counts, dest3, x2a, x2b, g)


def _moe_body(tm, nu_ref, be_ref, nxt_ref, xg_ref, wup_hbm, bup_ref, wdn_hbm, bdn_ref, ys_ref,
              wup_st, wdn_st, wup_bf, wdn_bf, wsem):
    b = pl.program_id(0)
    in_use = b < nu_ref[0]

    def fetch_up(e):
        return pltpu.make_async_copy(wup_hbm.at[e], wup_st, wsem.at[0])

    def fetch_dn(e):
        return pltpu.make_async_copy(wdn_hbm.at[e], wdn_st, wsem.at[1])

    @pl.when(b == 0)
    def _():
        fetch_up(be_ref[0]).start()
        fetch_dn(be_ref[0]).start()

    @pl.when(in_use & ((b == 0) | (be_ref[b] != be_ref[jnp.maximum(b - 1, 0)])))
    def _():
        fetch_up(be_ref[b]).wait()
        fetch_dn(be_ref[b]).wait()
        wup_bf[...] = wup_st[...].astype(BF16)
        wdn_bf[...] = wdn_st[...].astype(BF16)

        @pl.when(nxt_ref[b] >= 0)
        def _():
            fetch_up(nxt_ref[b]).start()
            fetch_dn(nxt_ref[b]).start()

    @pl.when(in_use)
    def _():
        hh = _load_token_tiles(xg_ref, tm).astype(BF16)
        uu = jnp.dot(hh, wup_bf[...], preferred_element_type=F32) + bup_ref[0]
        glu = jnp.minimum(uu[:, :D_FF], SWIGLU_LIMIT)
        lin = jnp.clip(uu[:, D_FF:], -SWIGLU_LIMIT, SWIGLU_LIMIT)
        act = glu * jax.nn.sigmoid(SWIGLU_ALPHA * glu) * (lin + 1.0)
        _store_token_tiles(ys_ref, jnp.dot(act.astype(BF16), wdn_bf[...], preferred_element_type=F32) + bdn_ref[0])

    @pl.when(b >= nu_ref[0])
    def _():
        ys_ref[...] = jnp.zeros_like(ys_ref)


def _moe(xg, block_e, next_e, n_used, w, tm, nb):
    def used(b, nu):
        return jnp.minimum(b, nu[0] - 1)

    def per_expert(shape):
        return pl.BlockSpec((1,) + shape, lambda b, nu, be, nxt: (be[used(b, nu)], 0, 0))

    grid_spec = pltpu.PrefetchScalarGridSpec(
        num_scalar_prefetch=3,
        grid=(nb,),
        in_specs=[
            pl.BlockSpec((tm * TILE_ROWS, LANES), lambda b, nu, be, nxt: (used(b, nu), 0)),
            pl.BlockSpec(memory_space=pl.ANY), per_expert((1, 2 * D_FF)),
            pl.BlockSpec(memory_space=pl.ANY), per_expert((1, D_MODEL)),
        ],
        out_specs=pl.BlockSpec((tm * TILE_ROWS, LANES), lambda b, nu, be, nxt: (b, 0)),
        scratch_shapes=[pltpu.VMEM((D_MODEL, 2 * D_FF), F32), pltpu.VMEM((D_FF, D_MODEL), F32),
                        pltpu.VMEM((D_MODEL, 2 * D_FF), BF16), pltpu.VMEM((D_FF, D_MODEL), BF16),
                        pltpu.SemaphoreType.DMA((2,))],
    )
    return pl.pallas_call(
        functools.partial(_moe_body, tm),
        grid_spec=grid_spec,
        out_shape=jax.ShapeDtypeStruct((nb * tm * TILE_ROWS, LANES), F32),
        compiler_params=pltpu.CompilerParams(dimension_semantics=("arbitrary",), vmem_limit_bytes=VMEM_LIMIT),
        name="moe",
    )(n_used, block_e, next_e, xg, w["w_up"], w["b_up"], w["w_down"], w["b_down"])


def _combine_body(rows, na, destc_ref, destn_ref, xa_ref, xb_ref, ra_ref, rb_ref, g_ref, ys_hbm, oa_ref, ob_ref,
                  gbuf, sem):
    i = pl.program_id(0)
    n_steps = pl.num_programs(0)
    slot = lax.rem(i, 2)
    groups = rows // SUBLANES

    def fetch(dest_ref, s):
        def fetch_group(gi, c):
            base = gi * SUBLANES
            for r in range(SUBLANES):
                for kk in range(TOP_K):
                    src = dest_ref[0, 0, kk * rows + base + r]
                    pltpu.make_async_copy(
                        ys_hbm.at[pl.ds(pl.multiple_of(src * TILE_ROWS, TILE_ROWS), TILE_ROWS)],
                        gbuf.at[s, kk, gi, :, r, :],
                        sem.at[s]).start(priority=kk % 2)
            return c
        lax.fori_loop(0, groups, fetch_group, 0)

    @pl.when(i == 0)
    def _():
        fetch(destc_ref, 0)

    @pl.when(i + 1 < n_steps)
    def _():
        fetch(destn_ref, 1 - slot)

    for kk in range(TOP_K):
        for r in range(SUBLANES):
            pltpu.make_async_copy(ys_hbm.at[pl.ds(0, groups * TILE_ROWS)].reshape(groups, TILE_ROWS, LANES),
                                  gbuf.at[slot, kk, :, :, r, :], sem.at[slot]).wait()

    first = i < na
    gate = jnp.where(first, ra_ref[...], rb_ref[...])
    acc = jnp.where(first, xa_ref[...], xb_ref[...])
    for kk in range(TOP_K):
        acc = acc + _plain_rows(gbuf[slot, kk]) * gate[:, ROUTE_GATE + kk:ROUTE_GATE + kk + 1]
    out = _rms(acc, g_ref[...])

    @pl.when(first)
    def _():
        oa_ref[...] = out

    @pl.when(jnp.logical_not(first))
    def _():
        ob_ref[...] = out


def _combine(x2a, x2b, ys, dest, route_a, route_b, g):
    rows = COMBINE_ROWS
    na, nbs = x2a.shape[0] // rows, x2b.shape[0] // rows
    steps = na + nbs
    dest3 = dest

    def seg_a(width):
        return pl.BlockSpec((rows, width), lambda i: (jnp.minimum(i, na - 1), 0))

    def seg_b(width):
        return pl.BlockSpec((rows, width), lambda i: (jnp.maximum(i - na, 0), 0))

    smem_cur = pl.BlockSpec((1, 1, rows * TOP_K), lambda i: (i, 0, 0), memory_space=pltpu.SMEM)
    smem_next = pl.BlockSpec((1, 1, rows * TOP_K), lambda i: (jnp.minimum(i + 1, steps - 1), 0, 0),
                             memory_space=pltpu.SMEM)
    return pl.pallas_call(
        functools.partial(_combine_body, rows, na),
        grid=(steps,),
        in_specs=[smem_cur, smem_next, seg_a(D_MODEL), seg_b(D_MODEL), seg_a(LANES), seg_b(LANES),
                  pl.BlockSpec((1, D_MODEL), lambda i: (0, 0)),
                  pl.BlockSpec(memory_space=pl.ANY)],
        out_specs=[seg_a(D_MODEL), seg_b(D_MODEL)],
        out_shape=[jax.ShapeDtypeStruct(x2a.shape, F32), jax.ShapeDtypeStruct(x2b.shape, F32)],
        scratch_shapes=[pltpu.VMEM((2, TOP_K, rows // SUBLANES, TILE_ROWS, SUBLANES, LANES), F32), pltpu.SemaphoreType.DMA((2,))],
        compiler_params=pltpu.CompilerParams(dimension_semantics=("arbitrary",), vmem_limit_bytes=VMEM_LIMIT),
        name="combine",
    )(dest3, dest3, x2a, x2b, route_a, route_b, g, ys)


def _moe_layer(x2a, x2b, route_a, route_b, rec_ta, rec_tb, counts_f, w, norm_final):
    t = x2a.shape[0] + x2b.shape[0]
    tm = MOE_ROWS
    nb = (t * TOP_K + N_EXPERTS * (tm - 1) + tm - 1) // tm
    assert DISPATCH_ROWS == COMBINE_ROWS
    dest, block_e, next_e, n_used, pend, counts = _route_tables([route_a, route_b], [rec_ta, rec_tb], counts_f, tm, nb,
                                                                 DISPATCH_ROWS)
    xg = _dispatch(x2a, x2b, dest, n_used, pend, counts, w["norm_moe"], tm, nb)
    ys = _moe(xg, block_e, next_e, n_used, w, tm, nb)
    return _combine(x2a, x2b, ys, dest, route_a, route_b, norm_final.reshape(1, -1))


def _prep_weights(norm_mix, w_in, w_gate, b_gate, gla_gain, pool_w, pool_scale, w_out, norm_mem_q, w_mq, w_mo,
                  norm_moe, w_router, b_router, w_up, b_up, w_down, b_down):
    n_qkvg = 2 * GLA_KW + 2 * GLA_WIDTH
    w_qkvg = jnp.zeros((D_MODEL, n_qkvg + LANES), BF16).at[:, :n_qkvg + GATE_RANK].set(
        w_in[:, :n_qkvg + GATE_RANK].astype(BF16))
    w_g = jnp.zeros((LANES, GLA_KW), BF16).at[:GATE_RANK].set(w_gate.astype(BF16))
    return {
        "norm_mix": norm_mix.reshape(1, -1),
        "w_qkvg": w_qkvg,
        "w_u": w_in[:, n_qkvg + GATE_RANK:].astype(BF16),
        "w_gate": w_g,
        "b_gate": b_gate.reshape(1, -1),
        "gla_gain": gla_gain.reshape(1, -1),
        "pool_w": pool_w.astype(BF16),
        "pool_scale": pool_scale.reshape(1, -1),
        "w_out": w_out.astype(BF16),
        "norm_mem_q": norm_mem_q.reshape(1, -1),
        "w_mq": w_mq.astype(BF16),
        "w_mo": w_mo.astype(BF16),
        "norm_moe": norm_moe.reshape(1, -1),
        "w_router": w_router.T.astype(BF16),
        "b_router": b_router.reshape(-1, 1),
        "w_up": w_up,
        "b_up": b_up.reshape(N_EXPERTS, 1, -1),
        "w_down": w_down,
        "b_down": b_down.reshape(N_EXPERTS, 1, -1),
    }


def _mix(x, s_gla, pool_prev, hist_len, mk, mv, cnt0, w):
    batch, seq, _ = x.shape
    x2d = x.reshape(batch * seq, D_MODEL)
    pool0 = jnp.concatenate([jnp.zeros((batch, 1, POOL_WIDTH), F32), pool_prev.astype(F32)], axis=1)
    x2, route, cnt, s_new, pool_t, rec_t = _premoe(x2d, batch, seq, hist_len, s_gla.astype(F32), pool0,
                                            mk.reshape(batch, N_MEM, D_MODEL), mv.reshape(batch, N_MEM, D_MODEL),
                                            cnt0, w)
    return x2, route, rec_t, cnt, s_new, pool_t[:, 1:, :]


def kernel(x_prompt, x_sample, mem_prompt, state_gla, state_pool, cache_mem_k, cache_mem_v, norm_mix, w_in, w_gate, b_gate, gla_gain, pool_w, pool_scale, w_out, norm_mem_q, norm_mem_kv, w_mq, w_mk, w_mv, w_mo, norm_moe, w_router, b_router, w_up, b_up, w_down, b_down, norm_final):
    depth = w_in.shape[0]
    assert depth == 1
    xp, xs = x_prompt, x_sample
    bp = xp.shape[0]
    gla_p, pool_p, mk_p, mv_p, gla_s, pool_s = [], [], [], [], [], []
    for l in range(depth):
        w = _prep_weights(norm_mix[l], w_in[l], w_gate[l], b_gate[l], gla_gain[l], pool_w[l], pool_scale[l], w_out[l],
                          norm_mem_q[l], w_mq[l], w_mo[l], norm_moe[l], w_router[l], b_router[l],
                          w_up[l], b_up[l], w_down[l], b_down[l])
        mk2, mv2 = _mem_kv(mem_prompt.reshape(bp * N_MEM, D_MODEL), norm_mem_kv[l].reshape(1, -1),
                           w_mk[l].astype(BF16), w_mv[l].astype(BF16))
        mk = mk2.reshape(bp, N_MEM, MEM_HEADS, MEM_HD)
        mv = mv2.reshape(bp, N_MEM, MEM_HEADS, MEM_HD)
        cnt0 = jnp.zeros((N_EXPERTS, LANES), F32)
        x2s, route_s, rec_ts, cnt_s, ss, ps = _mix(xs, state_gla[l], state_pool[l], PAST_LEN, cache_mem_k[l], cache_mem_v[l],
                                           cnt0, w)
        gla_s.append(ss)
        pool_s.append(ps)
        s0 = jnp.zeros((bp, GLA_HEADS, GLA_DK, GLA_DV), F32)
        p0 = jnp.zeros((bp, POOL_BUF, POOL_WIDTH), F32)
        x2p, route_p, rec_tp, cnt_all, sp, pp = _mix(xp, s0, p0, 0, mk, mv, cnt_s, w)
        ys, yp = _moe_layer(x2s, x2p, route_s, route_p, rec_ts, rec_tp, cnt_all, w, norm_final)
        xs, xp = ys.reshape(xs.shape), yp.reshape(xp.shape)
        gla_p.append(sp)
        pool_p.append(pp)
        mk_p.append(mk)
        mv_p.append(mv)
    return (xp, xs, jnp.stack(gla_p), jnp.stack(pool_p), jnp.stack(mk_p), jnp.stack(mv_p),
            jnp.stack(gla_s), jnp.stack(pool_s))
```

```python
import functools

import numpy as np
import jax
import jax.numpy as jnp
from jax import lax
from jax.experimental import pallas as pl
from jax.experimental.pallas import tpu as pltpu

F32 = jnp.float32
BF16 = jnp.bfloat16
I32 = jnp.int32

D_MODEL = 1024
CHUNK = 64
GLA_HEADS = 4
GLA_DK = 64
GLA_DV = 128
GLA_KW = GLA_HEADS * GLA_DK
GLA_WIDTH = GLA_HEADS * GLA_DV
GATE_RANK = 16
GATE_TAU = 16.0
POOL_WIDTH = 512
POOL_GC = 128
POOL_WINDOWS = (2, 4, 8, 16)
POOL_BUF = 15
PAST_LEN = 4096
POOL_ROWS = 16
N_MEM = 256
MEM_HEADS = 4
MEM_HD = 256
N_EXPERTS = 32
TOP_K = 4
D_FF = 1024
SWIGLU_ALPHA = 1.702
SWIGLU_LIMIT = 7.0
EPS = 1e-6
LANES = 128
SUBLANES = 8
TILE_ROWS = D_MODEL // LANES
ROUTE_EXPERT, ROUTE_RANK, ROUTE_GATE = 0, 4, 8
ROUTE_ROWS = 16

PREMOE_ROWS = 512
PREMOE_SUB = 256
MOE_ROWS = 512
DISPATCH_ROWS = 512
COMBINE_ROWS = 512
MEMKV_ROWS = 512
VMEM_LIMIT = 56 * 1024 * 1024

assert 2 * GLA_DK == LANES and GLA_DV == LANES

NT_DIMS = (((1,), (1,)), ((), ()))
TN_DIMS = (((0,), (0,)), ((), ()))


def _rms(x, g):
    ms = jnp.mean(x * x, axis=-1, keepdims=True)
    return x * lax.rsqrt(ms + EPS) * g


def _load_token_tiles(ref, rows):
    return jnp.concatenate([ref[pl.ds(s, rows, stride=TILE_ROWS), :] for s in range(TILE_ROWS)], axis=1)


def _plain_rows(v):
    g = v.shape[0]
    return jnp.concatenate([v[:, c].reshape(g * SUBLANES, LANES) for c in range(TILE_ROWS)], axis=1)


def _store_token_tiles(ref, val):
    rows = val.shape[0]
    for s in range(TILE_ROWS):
        ref[pl.ds(s, rows, stride=TILE_ROWS), :] = val[:, s * LANES:(s + 1) * LANES]


def _const_spec(shape):
    nd = len(shape)
    return pl.BlockSpec(shape, lambda *_: (0,) * nd, pipeline_mode=pl.Buffered(1))


def _memkv_body(m_ref, g_ref, wk_ref, wv_ref, k_ref, v_ref):
    m = _rms(m_ref[...], g_ref[...]).astype(BF16)
    k_ref[...] = jnp.dot(m, wk_ref[...], preferred_element_type=F32)
    v_ref[...] = jnp.dot(m, wv_ref[...], preferred_element_type=F32)


def _mem_kv(mem2d, g, wk, wv):
    n = mem2d.shape[0]
    tm = MEMKV_ROWS
    row = pl.BlockSpec((tm, D_MODEL), lambda i: (i, 0))
    return pl.pallas_call(
        _memkv_body,
        grid=(n // tm,),
        in_specs=[row, _const_spec((1, D_MODEL)), _const_spec((D_MODEL, D_MODEL)), _const_spec((D_MODEL, D_MODEL))],
        out_specs=[row, row],
        out_shape=[jax.ShapeDtypeStruct((n, D_MODEL), F32)] * 2,
        compiler_params=pltpu.CompilerParams(dimension_semantics=("arbitrary",), vmem_limit_bytes=VMEM_LIMIT),
        name="mem_kv",
    )(mem2d, g, wk, wv)


def _premoe_body(hist, rows,
                 x_ref, s0_ref, p0_ref, k_ref, v_ref, cnt0_ref,
                 nmix_ref, wqkvg_ref, wu_ref, wgate_ref, bgate_ref, ggain_ref, poolw_ref, pscale_ref,
                 wout_ref, nq_ref, wmq_ref, wmo_ref, nmoe_ref, wrt_ref, brt_ref,
                 tri_ref, utri_ref, band_ref,
                 x2_ref, route_ref, cnt_ref, sto_ref, po_ref, *rest):
    st_sc, prev_sc, cnt_sc = rest[-3:]
    rec_t_ref = rest[0] if len(rest) == 4 else None
    b = pl.program_id(0)
    l = pl.program_id(1)
    n_chunks = rows // CHUNK
    sub = min(rows, PREMOE_SUB)
    subs = [slice(s0, s0 + sub) for s0 in range(0, rows, sub)]

    zeros_dv = jnp.zeros((GLA_DV, LANES), F32)
    half = jnp.zeros((GLA_DK, GLA_DV), F32)

    @pl.when(l == 0)
    def _():
        row_blocks = []
        for hh in range(GLA_HEADS):
            s_h = s0_ref[0, hh]
            padded = jnp.concatenate([s_h, half] if hh % 2 == 0 else [half, s_h], axis=0)
            t_h = padded.T
            row_blocks.append(jnp.concatenate([t_h, zeros_dv] if hh // 2 == 0 else [zeros_dv, t_h], axis=1))
        st_sc[...] = jnp.concatenate(row_blocks, axis=0)
        prow = lax.broadcasted_iota(I32, (POOL_ROWS, POOL_WIDTH), 0)
        prev_sc[...] = jnp.where(prow >= POOL_ROWS - hist, p0_ref[0], 0.0)

    @pl.when((b == 0) & (l == 0))
    def _():
        cnt_sc[...] = cnt0_ref[...]

    x = x_ref[...]
    h = _rms(x, nmix_ref[...]).astype(BF16)
    z = jnp.dot(h, wqkvg_ref[...], preferred_element_type=F32)
    q = z[:, 0:GLA_KW] * (GLA_DK ** -0.5)
    k = z[:, GLA_KW:2 * GLA_KW]
    v = z[:, 2 * GLA_KW:2 * GLA_KW + GLA_WIDTH]
    g = z[:, 2 * GLA_KW + GLA_WIDTH:2 * GLA_KW + 2 * GLA_WIDTH]
    r = z[:, 2 * GLA_KW + 2 * GLA_WIDTH:]

    gp = jnp.dot(r.astype(BF16), wgate_ref[...], preferred_element_type=F32) + bgate_ref[...]
    la = jax.nn.log_sigmoid(gp) * (1.0 / GATE_TAU)

    hi = la.astype(BF16)
    r1 = la - hi.astype(F32)
    mid = r1.astype(BF16)
    lo = (r1 - mid.astype(F32)).astype(BF16)
    la3 = jnp.concatenate([hi, mid, lo], axis=1)
    bcum_parts = []
    for sl in subs:
        bb = jnp.dot(tri_ref[...], la3[sl], preferred_element_type=F32)
        bcum_parts.append((bb[:, 0:GLA_KW] + bb[:, GLA_KW:2 * GLA_KW]) + bb[:, 2 * GLA_KW:])
    bcum = jnp.concatenate(bcum_parts, axis=0)
    btot = jnp.concatenate([jnp.broadcast_to(bcum[(n + 1) * CHUNK - 1:(n + 1) * CHUNK, :], (CHUNK, GLA_KW))
                            for n in range(n_chunks)], axis=0)

    qd = q * jnp.exp(bcum)
    kd = k * jnp.exp(-bcum)
    kl = k * jnp.exp(btot - bcum)

    lane_kw = lax.broadcasted_iota(I32, (rows, GLA_KW), 1)
    rowi = lax.broadcasted_iota(I32, (sub, sub), 0)
    coli = lax.broadcasted_iota(I32, (sub, sub), 1)
    amask = (coli <= rowi) & (coli >= (rowi & ~(CHUNK - 1)))
    vb = v.astype(BF16)

    o_heads = []
    for hh in range(GLA_HEADS):
        mh = (lane_kw // GLA_DK) == hh
        qh = jnp.where(mh, qd, 0.0).astype(BF16)
        kh = jnp.where(mh, kd, 0.0).astype(BF16)
        o_sub = []
        for sl in subs:
            a = lax.dot_general(qh[sl], kh[sl], NT_DIMS, preferred_element_type=F32)
            a = jnp.where(amask, a, 0.0).astype(BF16)
            o_sub.append(jnp.dot(a, vb[sl, hh * GLA_DV:(hh + 1) * GLA_DV], preferred_element_type=F32))
        o_heads.append(jnp.concatenate(o_sub, axis=0))
    o_intra = jnp.concatenate(o_heads, axis=1)

    srow = lax.broadcasted_iota(I32, (GLA_WIDTH, GLA_KW), 0)
    scol = lax.broadcasted_iota(I32, (GLA_WIDTH, GLA_KW), 1)
    smask = (srow // GLA_DV) == (scol // GLA_DK)
    qdb = qd.astype(BF16)
    klb = kl.astype(BF16)
    st = st_sc[...]
    oi_parts = []
    for n in range(n_chunks):
        lo_r, hi_r = n * CHUNK, (n + 1) * CHUNK
        oi_parts.append(lax.dot_general(qdb[lo_r:hi_r], st.astype(BF16), NT_DIMS, preferred_element_type=F32))
        upd = lax.dot_general(vb[lo_r:hi_r], klb[lo_r:hi_r], TN_DIMS, preferred_element_type=F32)
        dec = jnp.exp(btot[lo_r:lo_r + 1, :])
        st = st * dec + jnp.where(smask, upd, 0.0)
    st_sc[...] = st
    o_inter = oi_parts[0] if n_chunks == 1 else jnp.concatenate(oi_parts, axis=0)
    o = o_intra + o_inter

    gains = ggain_ref[...]
    on_parts = []
    for hh in range(GLA_HEADS):
        oh = o[:, hh * GLA_DV:(hh + 1) * GLA_DV]
        ms = jnp.mean(oh * oh, axis=-1, keepdims=True)
        on_parts.append(oh * lax.rsqrt(ms + EPS) * gains[:, hh * GLA_DV:(hh + 1) * GLA_DV])
    on = jnp.concatenate(on_parts, axis=1) * (g * jax.nn.sigmoid(g))

    u = jnp.dot(h, wu_ref[...], preferred_element_type=F32)
    prevs = [prev_sc[...]] + [u[sl.stop - POOL_ROWS:sl.stop] for sl in subs[:-1]]
    exts = [jnp.concatenate([pv, u[sl]], axis=0).astype(BF16) for pv, sl in zip(prevs, subs)]
    pos = l * rows + lax.broadcasted_iota(I32, (rows, POOL_GC), 0)
    p_parts = []
    for gi, w in enumerate(POOL_WINDOWS):
        cols = slice(gi * POOL_GC, (gi + 1) * POOL_GC)
        s = jnp.concatenate([jnp.dot(band_ref[gi], ext[:, cols], preferred_element_type=F32) for ext in exts], axis=0)
        cnt_w = jnp.minimum(w, pos + 1 + hist).astype(F32)
        dd = s / cnt_w - u[:, cols]
        p_parts.append(jnp.dot(dd.astype(BF16), poolw_ref[gi], preferred_element_type=F32))
    p = jnp.concatenate(p_parts, axis=1) * pscale_ref[...]
    tail = u[rows - POOL_ROWS:rows, :]
    prev_sc[...] = tail
    po_ref[0] = tail

    cat = jnp.concatenate([on, p], axis=1).astype(BF16)
    x1 = x + jnp.dot(cat, wout_ref[...], preferred_element_type=F32)

    h2 = _rms(x1, nq_ref[...]).astype(BF16)
    qm = jnp.dot(h2, wmq_ref[...], preferred_element_type=F32).astype(BF16)
    kk = k_ref[0].astype(BF16)
    vv = v_ref[0].astype(BF16)
    a_parts = []
    for hh in range(MEM_HEADS):
        cols = slice(hh * MEM_HD, (hh + 1) * MEM_HD)
        s = lax.dot_general(qm[:, cols], kk[:, cols], NT_DIMS, preferred_element_type=F32) * (MEM_HD ** -0.5)
        e = jnp.exp(s - jnp.max(s, axis=-1, keepdims=True))
        pr = e / jnp.sum(e, axis=-1, keepdims=True)
        a_parts.append(jnp.dot(pr.astype(BF16), vv[:, cols], preferred_element_type=F32))
    att = jnp.concatenate(a_parts, axis=1).astype(BF16)
    x2 = x1 + jnp.dot(att, wmo_ref[...], preferred_element_type=F32)
    x2_ref[...] = x2

    rp = max(rows, LANES)
    h3 = _rms(x2, nmoe_ref[...]).astype(BF16)
    if rp > rows:
        h3 = jnp.concatenate([h3, jnp.zeros((rp - rows, D_MODEL), BF16)], axis=0)
    logits = lax.dot_general(wrt_ref[...], h3, NT_DIMS, preferred_element_type=F32) + brt_ref[...]
    eid = lax.broadcasted_iota(I32, (N_EXPERTS, rp), 0).astype(F32)
    work = logits
    vals, idxs = [], []
    for _ in range(TOP_K):
        m = jnp.max(work, axis=0, keepdims=True)
        idx = jnp.min(jnp.where(work == m, eid, float(N_EXPERTS)), axis=0, keepdims=True)
        vals.append(m)
        idxs.append(idx)
        work = jnp.where(eid == idx, -jnp.inf, work)
    exps = [jnp.exp(vk - vals[0]) for vk in vals]
    den = (exps[0] + exps[1]) + (exps[2] + exps[3])
    gates = [ek / den for ek in exps]

    valid = lax.broadcasted_iota(I32, (N_EXPERTS, rp), 1) < rows
    hot = jnp.zeros((N_EXPERTS, rp), F32)
    for idx in idxs:
        hot = hot + jnp.where((eid == idx) & valid, 1.0, 0.0)
    prefix = jnp.dot(hot.astype(BF16), utri_ref[...], preferred_element_type=F32) + cnt_sc[:, 0:1]
    ranks = [jnp.sum(jnp.where(eid == idx, prefix, 0.0), axis=0, keepdims=True) for idx in idxs]
    cnt_sc[...] = cnt_sc[...] + jnp.sum(hot, axis=1, keepdims=True)
    cnt_ref[...] = cnt_sc[...]

    pieces = idxs + ranks + gates
    rec_rows = ROUTE_ROWS
    rid = lax.broadcasted_iota(I32, (rec_rows, rp), 0)
    rec = jnp.zeros((rec_rows, rp), F32)
    for j, piece in enumerate(pieces):
        rec = jnp.where(rid == j, piece, rec)
    if rec_t_ref is not None:
        rec_t_ref[...] = rec
    rec = jnp.concatenate([rec, jnp.zeros((LANES - rec_rows, rp), F32)], axis=0)
    route_ref[...] = rec.T[:rows]

    @pl.when(l == pl.num_programs(1) - 1)
    def _():
        for hh in range(GLA_HEADS):
            t_h = st_sc[hh * GLA_DV:(hh + 1) * GLA_DV, (hh // 2) * LANES:(hh // 2 + 1) * LANES]
            sto_ref[0, hh] = t_h.T[(hh % 2) * GLA_DK:(hh % 2 + 1) * GLA_DK, :]


def _premoe_consts(rows):
    sub = min(rows, PREMOE_SUB)
    i = np.arange(sub)[:, None]
    j = np.arange(sub)[None, :]
    same = (i // CHUNK) == (j // CHUNK)
    tri = (same & (j <= i)).astype(np.float32)
    je = np.arange(sub + POOL_ROWS)[None, :]
    band = np.stack([((je <= i + POOL_ROWS) & (je > i + POOL_ROWS - w)).astype(np.float32) for w in POOL_WINDOWS])
    rp = max(rows, LANES)
    utri = (np.arange(rp)[:, None] < np.arange(rp)[None, :]).astype(np.float32)
    return jnp.asarray(tri, BF16), jnp.asarray(utri, BF16), jnp.asarray(band, BF16)


def _premoe(x2d, batch, seq, hist_len, s0, pool0, kmem, vmem, cnt0, w):
    rows = min(PREMOE_ROWS, seq)
    nl = seq // rows
    t = batch * seq
    hist = min(int(hist_len), POOL_BUF)
    tri, utri, band = _premoe_consts(rows)
    row_spec = pl.BlockSpec((rows, D_MODEL), lambda b, l: (b * nl + l, 0))
    lane_spec = pl.BlockSpec((rows, LANES), lambda b, l: (b * nl + l, 0))

    def per_batch(shape):
        return pl.BlockSpec((1,) + shape, lambda b, l: (b, 0, 0))

    consts = [w["norm_mix"], w["w_qkvg"], w["w_u"], w["w_gate"], w["b_gate"], w["gla_gain"], w["pool_w"],
              w["pool_scale"], w["w_out"], w["norm_mem_q"], w["w_mq"], w["w_mo"], w["norm_moe"], w["w_router"],
              w["b_router"], tri, utri, band]
    state_spec = pl.BlockSpec((1, GLA_HEADS, GLA_DK, GLA_DV), lambda b, l: (b, 0, 0, 0))
    in_specs = [row_spec, state_spec, per_batch((POOL_ROWS, POOL_WIDTH)),
                per_batch((N_MEM, D_MODEL)), per_batch((N_MEM, D_MODEL)), _const_spec(cnt0.shape)]
    in_specs += [_const_spec(c.shape) for c in consts]
    out_specs = [row_spec, lane_spec, pl.BlockSpec((N_EXPERTS, LANES), lambda b, l: (0, 0)),
                 state_spec, per_batch((POOL_ROWS, POOL_WIDTH))]
    out_shape = [jax.ShapeDtypeStruct((t, D_MODEL), F32), jax.ShapeDtypeStruct((t, LANES), F32),
                 jax.ShapeDtypeStruct((N_EXPERTS, LANES), F32),
                 jax.ShapeDtypeStruct((batch, GLA_HEADS, GLA_DK, GLA_DV), F32),
                 jax.ShapeDtypeStruct((batch, POOL_ROWS, POOL_WIDTH), F32)]
    if rows >= LANES:
        out_specs.append(pl.BlockSpec((ROUTE_ROWS, rows), lambda b, l: (0, b * nl + l)))
        out_shape.append(jax.ShapeDtypeStruct((ROUTE_ROWS, t), F32))
    outs = pl.pallas_call(
        functools.partial(_premoe_body, hist, rows),
        grid=(batch, nl),
        in_specs=in_specs,
        out_specs=out_specs,
        out_shape=out_shape,
        scratch_shapes=[pltpu.VMEM((GLA_WIDTH, GLA_KW), F32), pltpu.VMEM((POOL_ROWS, POOL_WIDTH), F32),
                        pltpu.VMEM((N_EXPERTS, LANES), F32)],
        compiler_params=pltpu.CompilerParams(dimension_semantics=("arbitrary", "arbitrary"),
                                             vmem_limit_bytes=VMEM_LIMIT),
        name="premoe",
    )(x2d, s0, pool0, kmem, vmem, cnt0, *consts)
    return tuple(outs) if rows >= LANES else tuple(outs) + (None,)


def _route_tables(routes, recs_t, counts_f, tm, nb, rows):
    counts = counts_f[:, 0].astype(I32)
    padded = ((counts + tm - 1) // tm) * tm
    pend = jnp.cumsum(padded)
    pstart = pend - padded
    eids = jnp.arange(N_EXPERTS, dtype=I32)
    parts = []
    for route, rec_t in zip(routes, recs_t):
        if rec_t is None:
            rec_t = route[:, :ROUTE_RANK + TOP_K].T
        e = rec_t[ROUTE_EXPERT:ROUTE_EXPERT + TOP_K].astype(I32)
        rank = rec_t[ROUTE_RANK:ROUTE_RANK + TOP_K].astype(I32)
        dest = jnp.sum(jnp.where(e[:, :, None] == eids, pstart, 0), axis=-1) + rank
        steps = dest.shape[1] // rows
        parts.append(dest.reshape(TOP_K, steps, rows).transpose(1, 0, 2).reshape(steps, 1, TOP_K * rows))
    dest = jnp.concatenate(parts, axis=0)
    blk_start = jnp.arange(nb, dtype=I32) * tm
    block_e = jnp.minimum(jnp.sum((pend[None, :] <= blk_start[:, None]).astype(I32), axis=1), N_EXPERTS - 1)
    n_used = (pend[-1] // tm).astype(I32)
    run_end = pend[block_e] // tm
    next_e = jnp.where(run_end < n_used, block_e[jnp.minimum(run_end, nb - 1)], -1)
    return dest.astype(I32), block_e.astype(I32), next_e.astype(I32), n_used.reshape(1), pend.astype(I32), counts


def _dispatch_body(rows, tm, nb, na, nu_ref, pend_ref, cnt_ref, dest_ref, xa_ref, xb_ref, g_ref, xg_hbm,
                   hbuf, zbuf, sem, zsem):
    i = pl.program_id(0)
    n_steps = pl.num_programs(0)
    slot = lax.rem(i, 2)
    nu = nu_ref[0]
    groups = rows // SUBLANES

    def tile(ref, row):
        return ref.at[pl.ds(pl.multiple_of(row * TILE_ROWS, TILE_ROWS), TILE_ROWS)]

    def zero_copy(start):
        return pltpu.make_async_copy(zbuf, xg_hbm.at[pl.ds(pl.multiple_of(start * TILE_ROWS, TILE_ROWS), tm * TILE_ROWS)],
                                     zsem)

    def for_each_fill(fn):
        def per_expert(e, c):
            @pl.when(cnt_ref[e] > 0)
            def _():
                fn(pend_ref[e] - tm)
            return c
        lax.fori_loop(0, N_EXPERTS, per_expert, 0)

        def per_tail(b, c):
            fn(b * tm)
            return c
        lax.fori_loop(nu, nb, per_tail, 0)

    @pl.when(i == 0)
    def _():
        zbuf[...] = jnp.zeros_like(zbuf)
        for_each_fill(lambda start: zero_copy(start).start())
        for_each_fill(lambda start: zero_copy(start).wait())

    def row_wait(s):
        for _ in range(TOP_K):
            pltpu.make_async_copy(hbuf.at[s], xg_hbm.at[pl.ds(0, rows * TILE_ROWS)], sem.at[s]).wait()

    @pl.when(i >= 2)
    def _():
        row_wait(slot)

    x = jnp.where(i < na, xa_ref[...], xb_ref[...])
    _store_token_tiles(hbuf.at[slot], _rms(x, g_ref[...]))

    def send_group(gi, c):
        base = gi * SUBLANES
        for r in range(SUBLANES):
            for kk in range(TOP_K):
                dst = dest_ref[0, 0, kk * rows + base + r]
                pltpu.make_async_copy(tile(hbuf.at[slot], gi * SUBLANES + r), tile(xg_hbm, dst),
                                      sem.at[slot]).start(priority=kk % 2)
        return c
    lax.fori_loop(0, groups, send_group, 0)

    @pl.when(i == n_steps - 1)
    def _():
        row_wait(slot)

        @pl.when(i >= 1)
        def _():
            row_wait(1 - slot)


def _dispatch(x2a, x2b, dest, n_used, pend, counts, g, tm, nb):
    rows = DISPATCH_ROWS
    na, nbs = x2a.shape[0] // rows, x2b.shape[0] // rows
    steps = na + nbs
    dest3 = dest
    grid_spec = pltpu.PrefetchScalarGridSpec(
        num_scalar_prefetch=3,
        grid=(steps,),
        in_specs=[
            pl.BlockSpec((1, 1, rows * TOP_K), lambda i, *_: (i, 0, 0), memory_space=pltpu.SMEM),
            pl.BlockSpec((rows, D_MODEL), lambda i, *_: (jnp.minimum(i, na - 1), 0)),
            pl.BlockSpec((rows, D_MODEL), lambda i, *_: (jnp.maximum(i - na, 0), 0)),
            pl.BlockSpec((1, D_MODEL), lambda i, *_: (0, 0)),
        ],
        out_specs=pl.BlockSpec(memory_space=pl.ANY),
        scratch_shapes=[pltpu.VMEM((2, rows * TILE_ROWS, LANES), F32), pltpu.VMEM((tm * TILE_ROWS, LANES), F32),
                        pltpu.SemaphoreType.DMA((2,)), pltpu.SemaphoreType.DMA(())],
    )
    return pl.pallas_call(
        functools.partial(_dispatch_body, rows, tm, nb, na),
        grid_spec=grid_spec,
        out_shape=jax.ShapeDtypeStruct((nb * tm * TILE_ROWS, LANES), F32),
        compiler_params=pltpu.CompilerParams(dimension_semantics=("arbitrary",), vmem_limit_bytes=VMEM_LIMIT),
        name="dispatch",
    )(n_used, pend, counts, dest3, x2a, x2b, g)


def _moe_body(tm, nu_ref, be_ref, nxt_ref, xg_ref, wup_hbm, bup_ref, wdn_hbm, bdn_ref, ys_ref,
              wup_st, wdn_st, wup_bf, wdn_bf, wsem):
    b = pl.program_id(0)
    in_use = b < nu_ref[0]

    def fetch_up(e):
        return pltpu.make_async_copy(wup_hbm.at[e], wup_st, wsem.at[0])

    def fetch_dn(e):
        return pltpu.make_async_copy(wdn_hbm.at[e], wdn_st, wsem.at[1])

    @pl.when(b == 0)
    def _():
        fetch_up(be_ref[0]).start()
        fetch_dn(be_ref[0]).start()

    @pl.when(in_use & ((b == 0) | (be_ref[b] != be_ref[jnp.maximum(b - 1, 0)])))
    def _():
        fetch_up(be_ref[b]).wait()
        fetch_dn(be_ref[b]).wait()
        wup_bf[...] = wup_st[...].astype(BF16)
        wdn_bf[...] = wdn_st[...].astype(BF16)

        @pl.when(nxt_ref[b] >= 0)
        def _():
            fetch_up(nxt_ref[b]).start()
            fetch_dn(nxt_ref[b]).start()

    @pl.when(in_use)
    def _():
        hh = _load_token_tiles(xg_ref, tm).astype(BF16)
        uu = jnp.dot(hh, wup_bf[...], preferred_element_type=F32) + bup_ref[0]
        glu = jnp.minimum(uu[:, :D_FF], SWIGLU_LIMIT)
        lin = jnp.clip(uu[:, D_FF:], -SWIGLU_LIMIT, SWIGLU_LIMIT)
        act = glu * jax.nn.sigmoid(SWIGLU_ALPHA * glu) * (lin + 1.0)
        _store_token_tiles(ys_ref, jnp.dot(act.astype(BF16), wdn_bf[...], preferred_element_type=F32) + bdn_ref[0])

    @pl.when(b >= nu_ref[0])
    def _():
        ys_ref[...] = jnp.zeros_like(ys_ref)


def _moe(xg, block_e, next_e, n_used, w, tm, nb):
    def used(b, nu):
        return jnp.minimum(b, nu[0] - 1)

    def per_expert(shape):
        return pl.BlockSpec((1,) + shape, lambda b, nu, be, nxt: (be[used(b, nu)], 0, 0))

    grid_spec = pltpu.PrefetchScalarGridSpec(
        num_scalar_prefetch=3,
        grid=(nb,),
        in_specs=[
            pl.BlockSpec((tm * TILE_ROWS, LANES), lambda b, nu, be, nxt: (used(b, nu), 0)),
            pl.BlockSpec(memory_space=pl.ANY), per_expert((1, 2 * D_FF)),
            pl.BlockSpec(memory_space=pl.ANY), per_expert((1, D_MODEL)),
        ],
        out_specs=pl.BlockSpec((tm * TILE_ROWS, LANES), lambda b, nu, be, nxt: (b, 0)),
        scratch_shapes=[pltpu.VMEM((D_MODEL, 2 * D_FF), F32), pltpu.VMEM((D_FF, D_MODEL), F32),
                        pltpu.VMEM((D_MODEL, 2 * D_FF), BF16), pltpu.VMEM((D_FF, D_MODEL), BF16),
                        pltpu.SemaphoreType.DMA((2,))],
    )
    return pl.pallas_call(
        functools.partial(_moe_body, tm),
        grid_spec=grid_spec,
        out_shape=jax.ShapeDtypeStruct((nb * tm * TILE_ROWS, LANES), F32),
        compiler_params=pltpu.CompilerParams(dimension_semantics=("arbitrary",), vmem_limit_bytes=VMEM_LIMIT),
        name="moe",
    )(n_used, block_e, next_e, xg, w["w_up"], w["b_up"], w["w_down"], w["b_down"])


def _combine_body(rows, na, destc_ref, destn_ref, xa_ref, xb_ref, ra_ref, rb_ref, g_ref, ys_hbm, oa_ref, ob_ref,
                  gbuf, sem):
    i = pl.program_id(0)
    n_steps = pl.num_programs(0)
    slot = lax.rem(i, 2)
    groups = rows // SUBLANES

    def fetch(dest_ref, s):
        def fetch_group(gi, c):
            base = gi * SUBLANES
            for r in range(SUBLANES):
                for kk in range(TOP_K):
                    src = dest_ref[0, 0, kk * rows + base + r]
                    pltpu.make_async_copy(
                        ys_hbm.at[pl.ds(pl.multiple_of(src * TILE_ROWS, TILE_ROWS), TILE_ROWS)],
                        gbuf.at[s, kk, gi, :, r, :],
                        sem.at[s]).start(priority=kk % 2)
            return c
        lax.fori_loop(0, groups, fetch_group, 0)

    @pl.when(i == 0)
    def _():
        fetch(destc_ref, 0)

    @pl.when(i + 1 < n_steps)
    def _():
        fetch(destn_ref, 1 - slot)

    for kk in range(TOP_K):
        for r in range(SUBLANES):
            pltpu.make_async_copy(ys_hbm.at[pl.ds(0, groups * TILE_ROWS)].reshape(groups, TILE_ROWS, LANES),
                                  gbuf.at[slot, kk, :, :, r, :], sem.at[slot]).wait()

    first = i < na
    gate = jnp.where(first, ra_ref[...], rb_ref[...])
    acc = jnp.where(first, xa_ref[...], xb_ref[...])
    for kk in range(TOP_K):
        acc = acc + _plain_rows(gbuf[slot, kk]) * gate[:, ROUTE_GATE + kk:ROUTE_GATE + kk + 1]
    out = _rms(acc, g_ref[...])

    @pl.when(first)
    def _():
        oa_ref[...] = out

    @pl.when(jnp.logical_not(first))
    def _():
        ob_ref[...] = out


def _combine(x2a, x2b, ys, dest, route_a, route_b, g):
    rows = COMBINE_ROWS
    na, nbs = x2a.shape[0] // rows, x2b.shape[0] // rows
    steps = na + nbs
    dest3 = dest

    def seg_a(width):
        return pl.BlockSpec((rows, width), lambda i: (jnp.minimum(i, na - 1), 0))

    def seg_b(width):
        return pl.BlockSpec((rows, width), lambda i: (jnp.maximum(i - na, 0), 0))

    smem_cur = pl.BlockSpec((1, 1, rows * TOP_K), lambda i: (i, 0, 0), memory_space=pltpu.SMEM)
    smem_next = pl.BlockSpec((1, 1, rows * TOP_K), lambda i: (jnp.minimum(i + 1, steps - 1), 0, 0),
                             memory_space=pltpu.SMEM)
    return pl.pallas_call(
        functools.partial(_combine_body, rows, na),
        grid=(steps,),
        in_specs=[smem_cur, smem_next, seg_a(D_MODEL), seg_b(D_MODEL), seg_a(LANES), seg_b(LANES),
                  pl.BlockSpec((1, D_MODEL), lambda i: (0, 0)),
                  pl.BlockSpec(memory_space=pl.ANY)],
        out_specs=[seg_a(D_MODEL), seg_b(D_MODEL)],
        out_shape=[jax.ShapeDtypeStruct(x2a.shape, F32), jax.ShapeDtypeStruct(x2b.shape, F32)],
        scratch_shapes=[pltpu.VMEM((2, TOP_K, rows // SUBLANES, TILE_ROWS, SUBLANES, LANES), F32), pltpu.SemaphoreType.DMA((2,))],
        compiler_params=pltpu.CompilerParams(dimension_semantics=("arbitrary",), vmem_limit_bytes=VMEM_LIMIT),
        name="combine",
    )(dest3, dest3, x2a, x2b, route_a, route_b, g, ys)


def _moe_layer(x2a, x2b, route_a, route_b, rec_ta, rec_tb, counts_f, w, norm_final):
    t = x2a.shape[0] + x2b.shape[0]
    tm = MOE_ROWS
    nb = (t * TOP_K + N_EXPERTS * (tm - 1) + tm - 1) // tm
    assert DISPATCH_ROWS == COMBINE_ROWS
    dest, block_e, next_e, n_used, pend, counts = _route_tables([route_a, route_b], [rec_ta, rec_tb], counts_f, tm, nb,
                                                                 DISPATCH_ROWS)
    xg = _dispatch(x2a, x2b, dest, n_used, pend, counts, w["norm_moe"], tm, nb)
    ys = _moe(xg, block_e, next_e, n_used, w, tm, nb)
    return _combine(x2a, x2b, ys, dest, route_a, route_b, norm_final.reshape(1, -1))


def _prep_weights(norm_mix, w_in, w_gate, b_gate, gla_gain, pool_w, pool_scale, w_out, norm_mem_q, w_mq, w_mo,
                  norm_moe, w_router, b_router, w_up, b_up, w_down, b_down):
    n_qkvg = 2 * GLA_KW + 2 * GLA_WIDTH
    w_qkvg = jnp.zeros((D_MODEL, n_qkvg + LANES), BF16).at[:, :n_qkvg + GATE_RANK].set(
        w_in[:, :n_qkvg + GATE_RANK].astype(BF16))
    w_g = jnp.zeros((LANES, GLA_KW), BF16).at[:GATE_RANK].set(w_gate.astype(BF16))
    return {
        "norm_mix": norm_mix.reshape(1, -1),
        "w_qkvg": w_qkvg,
        "w_u": w_in[:, n_qkvg + GATE_RANK:].astype(BF16),
        "w_gate": w_g,
        "b_gate": b_gate.reshape(1, -1),
        "gla_gain": gla_gain.reshape(1, -1),
        "pool_w": pool_w.astype(BF16),
        "pool_scale": pool_scale.reshape(1, -1),
        "w_out": w_out.astype(BF16),
        "norm_mem_q": norm_mem_q.reshape(1, -1),
        "w_mq": w_mq.astype(BF16),
        "w_mo": w_mo.astype(BF16),
        "norm_moe": norm_moe.reshape(1, -1),
        "w_router": w_router.T.astype(BF16),
        "b_router": b_router.reshape(-1, 1),
        "w_up": w_up,
        "b_up": b_up.reshape(N_EXPERTS, 1, -1),
        "w_down": w_down,
        "b_down": b_down.reshape(N_EXPERTS, 1, -1),
    }


def _mix(x, s_gla, pool_prev, hist_len, mk, mv, cnt0, w):
    batch, seq, _ = x.shape
    x2d = x.reshape(batch * seq, D_MODEL)
    pool0 = jnp.concatenate([jnp.zeros((batch, 1, POOL_WIDTH), F32), pool_prev.astype(F32)], axis=1)
    x2, route, cnt, s_new, pool_t, rec_t = _premoe(x2d, batch, seq, hist_len, s_gla.astype(F32), pool0,
                                            mk.reshape(batch, N_MEM, D_MODEL), mv.reshape(batch, N_MEM, D_MODEL),
                                            cnt0, w)
    return x2, route, rec_t, cnt, s_new, pool_t[:, 1:, :]


def kernel(x_prompt, x_sample, mem_prompt, state_gla, state_pool, cache_mem_k, cache_mem_v, norm_mix, w_in, w_gate, b_gate, gla_gain, pool_w, pool_scale, w_out, norm_mem_q, norm_mem_kv, w_mq, w_mk, w_mv, w_mo, norm_moe, w_router, b_router, w_up, b_up, w_down, b_down, norm_final):
    depth = w_in.shape[0]
    assert depth == 1
    xp, xs = x_prompt, x_sample
    bp = xp.shape[0]
    gla_p, pool_p, mk_p, mv_p, gla_s, pool_s = [], [], [], [], [], []
    for l in range(depth):
        w = _prep_weights(norm_mix[l], w_in[l], w_gate[l], b_gate[l], gla_gain[l], pool_w[l], pool_scale[l], w_out[l],
                          norm_mem_q[l], w_mq[l], w_mo[l], norm_moe[l], w_router[l], b_router[l],
                          w_up[l], b_up[l], w_down[l], b_down[l])
        mk2, mv2 = _mem_kv(mem_prompt.reshape(bp * N_MEM, D_MODEL), norm_mem_kv[l].reshape(1, -1),
                           w_mk[l].astype(BF16), w_mv[l].astype(BF16))
        mk = mk2.reshape(bp, N_MEM, MEM_HEADS, MEM_HD)
        mv = mv2.reshape(bp, N_MEM, MEM_HEADS, MEM_HD)
        cnt0 = jnp.zeros((N_EXPERTS, LANES), F32)
        x2s, route_s, rec_ts, cnt_s, ss, ps = _mix(xs, state_gla[l], state_pool[l], PAST_LEN, cache_mem_k[l], cache_mem_v[l],
                                           cnt0, w)
        gla_s.append(ss)
        pool_s.append(ps)
        s0 = jnp.zeros((bp, GLA_HEADS, GLA_DK, GLA_DV), F32)
        p0 = jnp.zeros((bp, POOL_BUF, POOL_WIDTH), F32)
        x2p, route_p, rec_tp, cnt_all, sp, pp = _mix(xp, s0, p0, 0, mk, mv, cnt_s, w)
        ys, yp = _moe_layer(x2s, x2p, route_s, route_p, rec_ts, rec_tp, cnt_all, w, norm_final)
        xs, xp = ys.reshape(xs.shape), yp.reshape(xp.shape)
        gla_p.append(sp)
        pool_p.append(pp)
        mk_p.append(mk)
        mv_p.append(mv)
    return (xp, xs, jnp.stack(gla_p), jnp.stack(pool_p), jnp.stack(mk_p), jnp.stack(mv_p),
            jnp.stack(gla_s), jnp.stack(pool_s))
```

```python
import functools

import numpy as np
import jax
import jax.numpy as jnp
from jax import lax
from jax.experimental import pallas as pl
from jax.experimental.pallas import tpu as pltpu

F32 = jnp.float32
BF16 = jnp.bfloat16
I32 = jnp.int32

D_MODEL = 1024
CHUNK = 64
GLA_HEADS = 4
GLA_DK = 64
GLA_DV = 128
GLA_KW = GLA_HEADS * GLA_DK
GLA_WIDTH = GLA_HEADS * GLA_DV
GATE_RANK = 16
GATE_TAU = 16.0
POOL_WIDTH = 512
POOL_GC = 128
POOL_WINDOWS = (2, 4, 8, 16)
POOL_BUF = 15
PAST_LEN = 4096
POOL_ROWS = 16
N_MEM = 256
MEM_HEADS = 4
MEM_HD = 256
N_EXPERTS = 32
TOP_K = 4
D_FF = 1024
SWIGLU_ALPHA = 1.702
SWIGLU_LIMIT = 7.0
EPS = 1e-6
LANES = 128
SUBLANES = 8
TILE_ROWS = D_MODEL // LANES
ROUTE_EXPERT, ROUTE_RANK, ROUTE_GATE = 0, 4, 8
ROUTE_ROWS = 16

PREMOE_ROWS = 512
PREMOE_SUB = 256
MOE_ROWS = 512
DISPATCH_ROWS = 512
COMBINE_ROWS = 512
MEMKV_ROWS = 512
VMEM_LIMIT = 56 * 1024 * 1024

assert 2 * GLA_DK == LANES and GLA_DV == LANES

NT_DIMS = (((1,), (1,)), ((), ()))
TN_DIMS = (((0,), (0,)), ((), ()))


def _rms(x, g):
    ms = jnp.mean(x * x, axis=-1, keepdims=True)
    return x * lax.rsqrt(ms + EPS) * g


def _load_token_tiles(ref, rows):
    return jnp.concatenate([ref[pl.ds(s, rows, stride=TILE_ROWS), :] for s in range(TILE_ROWS)], axis=1)


def _plain_rows(v):
    g = v.shape[0]
    return jnp.concatenate([v[:, c].reshape(g * SUBLANES, LANES) for c in range(TILE_ROWS)], axis=1)


def _store_token_tiles(ref, val):
    rows = val.shape[0]
    for s in range(TILE_ROWS):
        ref[pl.ds(s, rows, stride=TILE_ROWS), :] = val[:, s * LANES:(s + 1) * LANES]


def _const_spec(shape):
    nd = len(shape)
    return pl.BlockSpec(shape, lambda *_: (0,) * nd, pipeline_mode=pl.Buffered(1))


def _memkv_body(m_ref, g_ref, wk_ref, wv_ref, k_ref, v_ref):
    m = _rms(m_ref[...], g_ref[...]).astype(BF16)
    k_ref[...] = jnp.dot(m, wk_ref[...], preferred_element_type=F32)
    v_ref[...] = jnp.dot(m, wv_ref[...], preferred_element_type=F32)


def _mem_kv(mem2d, g, wk, wv):
    n = mem2d.shape[0]
    tm = MEMKV_ROWS
    row = pl.BlockSpec((tm, D_MODEL), lambda i: (i, 0))
    return pl.pallas_call(
        _memkv_body,
        grid=(n // tm,),
        in_specs=[row, _const_spec((1, D_MODEL)), _const_spec((D_MODEL, D_MODEL)), _const_spec((D_MODEL, D_MODEL))],
        out_specs=[row, row],
        out_shape=[jax.ShapeDtypeStruct((n, D_MODEL), F32)] * 2,
        compiler_params=pltpu.CompilerParams(dimension_semantics=("arbitrary",), vmem_limit_bytes=VMEM_LIMIT),
        name="mem_kv",
    )(mem2d, g, wk, wv)


def _premoe_body(hist, rows,
                 x_ref, s0_ref, p0_ref, k_ref, v_ref, cnt0_ref,
                 nmix_ref, wqkvg_ref, wgate_ref, bgate_ref, ggain_ref, poolw_ref, pscale_ref,
                 wout_ref, nq_ref, wmq_ref, wmo_ref, nmoe_ref, wrt_ref, brt_ref,
                 tri_ref, utri_ref, band_ref,
                 x2_ref, route_ref, cnt_ref, sto_ref, po_ref, *rest):
    st_sc, prev_sc, cnt_sc = rest[-3:]
    rec_t_ref = rest[0] if len(rest) == 4 else None
    b = pl.program_id(0)
    l = pl.program_id(1)
    n_chunks = rows // CHUNK
    sub = min(rows, PREMOE_SUB)
    subs = [slice(s0, s0 + sub) for s0 in range(0, rows, sub)]

    zeros_dv = jnp.zeros((GLA_DV, LANES), F32)
    half = jnp.zeros((GLA_DK, GLA_DV), F32)

    @pl.when(l == 0)
    def _():
        row_blocks = []
        for hh in range(GLA_HEADS):
            s_h = s0_ref[0, hh]
            padded = jnp.concatenate([s_h, half] if hh % 2 == 0 else [half, s_h], axis=0)
            t_h = padded.T
            row_blocks.append(jnp.concatenate([t_h, zeros_dv] if hh // 2 == 0 else [zeros_dv, t_h], axis=1))
        st_sc[...] = jnp.concatenate(row_blocks, axis=0)
        prow = lax.broadcasted_iota(I32, (POOL_ROWS, POOL_WIDTH), 0)
        prev_sc[...] = jnp.where(prow >= POOL_ROWS - hist, p0_ref[0], 0.0)

    @pl.when((b == 0) & (l == 0))
    def _():
        cnt_sc[...] = cnt0_ref[...]

    x = x_ref[...]
    h = _rms(x, nmix_ref[...]).astype(BF16)
    z = jnp.dot(h, wqkvg_ref[...], preferred_element_type=F32)
    q = z[:, 0:GLA_KW] * (GLA_DK ** -0.5)
    k = z[:, GLA_KW:2 * GLA_KW]
    v = z[:, 2 * GLA_KW:2 * GLA_KW + GLA_WIDTH]
    g = z[:, 2 * GLA_KW + GLA_WIDTH:2 * GLA_KW + 2 * GLA_WIDTH]
    r = z[:, 2 * GLA_KW + 2 * GLA_WIDTH:2 * GLA_KW + 2 * GLA_WIDTH + LANES]
    u = z[:, 2 * GLA_KW + 2 * GLA_WIDTH + LANES:]

    gp = jnp.dot(r.astype(BF16), wgate_ref[...], preferred_element_type=F32) + bgate_ref[...]
    la = jax.nn.log_sigmoid(gp) * (1.0 / GATE_TAU)

    hi = la.astype(BF16)
    r1 = la - hi.astype(F32)
    mid = r1.astype(BF16)
    lo = (r1 - mid.astype(F32)).astype(BF16)
    la3 = jnp.concatenate([hi, mid, lo], axis=1)
    bcum_parts = []
    for sl in subs:
        bb = jnp.dot(tri_ref[...], la3[sl], preferred_element_type=F32)
        bcum_parts.append((bb[:, 0:GLA_KW] + bb[:, GLA_KW:2 * GLA_KW]) + bb[:, 2 * GLA_KW:])
    bcum = jnp.concatenate(bcum_parts, axis=0)
    btot = jnp.concatenate([jnp.broadcast_to(bcum[(n + 1) * CHUNK - 1:(n + 1) * CHUNK, :], (CHUNK, GLA_KW))
                            for n in range(n_chunks)], axis=0)

    qd = q * jnp.exp(bcum)
    kd = k * jnp.exp(-bcum)
    kl = k * jnp.exp(btot - bcum)

    lane_kw = lax.broadcasted_iota(I32, (rows, GLA_KW), 1)
    rowi = lax.broadcasted_iota(I32, (sub, sub), 0)
    coli = lax.broadcasted_iota(I32, (sub, sub), 1)
    amask = (coli <= rowi) & (coli >= (rowi & ~(CHUNK - 1)))
    vb = v.astype(BF16)

    o_heads = []
    for hh in range(GLA_HEADS):
        mh = (lane_kw // GLA_DK) == hh
        qh = jnp.where(mh, qd, 0.0).astype(BF16)
        kh = jnp.where(mh, kd, 0.0).astype(BF16)
        o_sub = []
        for sl in subs:
            a = lax.dot_general(qh[sl], kh[sl], NT_DIMS, preferred_element_type=F32)
            a = jnp.where(amask, a, 0.0).astype(BF16)
            o_sub.append(jnp.dot(a, vb[sl, hh * GLA_DV:(hh + 1) * GLA_DV], preferred_element_type=F32))
        o_heads.append(jnp.concatenate(o_sub, axis=0))
    o_intra = jnp.concatenate(o_heads, axis=1)

    srow = lax.broadcasted_iota(I32, (GLA_WIDTH, GLA_KW), 0)
    scol = lax.broadcasted_iota(I32, (GLA_WIDTH, GLA_KW), 1)
    smask = (srow // GLA_DV) == (scol // GLA_DK)
    qdb = qd.astype(BF16)
    klb = kl.astype(BF16)
    st = st_sc[...]
    oi_parts = []
    for n in range(n_chunks):
        lo_r, hi_r = n * CHUNK, (n + 1) * CHUNK
        oi_parts.append(lax.dot_general(qdb[lo_r:hi_r], st.astype(BF16), NT_DIMS, preferred_element_type=F32))
        upd = lax.dot_general(vb[lo_r:hi_r], klb[lo_r:hi_r], TN_DIMS, preferred_element_type=F32)
        dec = jnp.exp(btot[lo_r:lo_r + 1, :])
        st = st * dec + jnp.where(smask, upd, 0.0)
    st_sc[...] = st
    o_inter = oi_parts[0] if n_chunks == 1 else jnp.concatenate(oi_parts, axis=0)
    o = o_intra + o_inter

    gains = ggain_ref[...]
    on_parts = []
    for hh in range(GLA_HEADS):
        oh = o[:, hh * GLA_DV:(hh + 1) * GLA_DV]
        ms = jnp.mean(oh * oh, axis=-1, keepdims=True)
        on_parts.append(oh * lax.rsqrt(ms + EPS) * gains[:, hh * GLA_DV:(hh + 1) * GLA_DV])
    on = jnp.concatenate(on_parts, axis=1) * (g * jax.nn.sigmoid(g))

    prevs = [prev_sc[...]] + [u[sl.stop - POOL_ROWS:sl.stop] for sl in subs[:-1]]
    exts = [jnp.concatenate([pv, u[sl]], axis=0).astype(BF16) for pv, sl in zip(prevs, subs)]
    pos = l * rows + lax.broadcasted_iota(I32, (rows, POOL_GC), 0)
    p_parts = []
    for gi, w in enumerate(POOL_WINDOWS):
        cols = slice(gi * POOL_GC, (gi + 1) * POOL_GC)
        s = jnp.concatenate([jnp.dot(band_ref[gi], ext[:, cols], preferred_element_type=F32) for ext in exts], axis=0)
        cnt_w = jnp.minimum(w, pos + 1 + hist).astype(F32)
        dd = s / cnt_w - u[:, cols]
        p_parts.append(jnp.dot(dd.astype(BF16), poolw_ref[gi], preferred_element_type=F32))
    p = jnp.concatenate(p_parts, axis=1) * pscale_ref[...]
    tail = u[rows - POOL_ROWS:rows, :]
    prev_sc[...] = tail
    po_ref[0] = tail

    cat = jnp.concatenate([on, p], axis=1).astype(BF16)
    x1 = x + jnp.dot(cat, wout_ref[...], preferred_element_type=F32)

    h2 = _rms(x1, nq_ref[...]).astype(BF16)
    qm = jnp.dot(h2, wmq_ref[...], preferred_element_type=F32).astype(BF16)
    kk = k_ref[0].astype(BF16)
    vv = v_ref[0].astype(BF16)
    a_parts = []
    for hh in range(MEM_HEADS):
        cols = slice(hh * MEM_HD, (hh + 1) * MEM_HD)
        s = lax.dot_general(qm[:, cols], kk[:, cols], NT_DIMS, preferred_element_type=F32) * (MEM_HD ** -0.5)
        e = jnp.exp(s - jnp.max(s, axis=-1, keepdims=True))
        pr = e / jnp.sum(e, axis=-1, keepdims=True)
        a_parts.append(jnp.dot(pr.astype(BF16), vv[:, cols], preferred_element_type=F32))
    att = jnp.concatenate(a_parts, axis=1).astype(BF16)
    x2 = x1 + jnp.dot(att, wmo_ref[...], preferred_element_type=F32)
    x2_ref[...] = x2

    rp = max(rows, LANES)
    h3 = _rms(x2, nmoe_ref[...]).astype(BF16)
    if rp > rows:
        h3 = jnp.concatenate([h3, jnp.zeros((rp - rows, D_MODEL), BF16)], axis=0)
    logits = lax.dot_general(wrt_ref[...], h3, NT_DIMS, preferred_element_type=F32) + brt_ref[...]
    eid = lax.broadcasted_iota(I32, (N_EXPERTS, rp), 0).astype(F32)
    work = logits
    vals, idxs = [], []
    for _ in range(TOP_K):
        m = jnp.max(work, axis=0, keepdims=True)
        idx = jnp.min(jnp.where(work == m, eid, float(N_EXPERTS)), axis=0, keepdims=True)
        vals.append(m)
        idxs.append(idx)
        work = jnp.where(eid == idx, -jnp.inf, work)
    exps = [jnp.exp(vk - vals[0]) for vk in vals]
    den = (exps[0] + exps[1]) + (exps[2] + exps[3])
    gates = [ek / den for ek in exps]

    valid = lax.broadcasted_iota(I32, (N_EXPERTS, rp), 1) < rows
    hot = jnp.zeros((N_EXPERTS, rp), F32)
    for idx in idxs:
        hot = hot + jnp.where((eid == idx) & valid, 1.0, 0.0)
    prefix = jnp.dot(hot.astype(BF16), utri_ref[...], preferred_element_type=F32) + cnt_sc[:, 0:1]
    ranks = [jnp.sum(jnp.where(eid == idx, prefix, 0.0), axis=0, keepdims=True) for idx in idxs]
    cnt_sc[...] = cnt_sc[...] + jnp.sum(hot, axis=1, keepdims=True)
    cnt_ref[...] = cnt_sc[...]

    pieces = idxs + ranks + gates
    rec_rows = ROUTE_ROWS
    rid = lax.broadcasted_iota(I32, (rec_rows, rp), 0)
    rec = jnp.zeros((rec_rows, rp), F32)
    for j, piece in enumerate(pieces):
        rec = jnp.where(rid == j, piece, rec)
    if rec_t_ref is not None:
        rec_t_ref[...] = rec
    rec = jnp.concatenate([rec, jnp.zeros((LANES - rec_rows, rp), F32)], axis=0)
    route_ref[...] = rec.T[:rows]

    @pl.when(l == pl.num_programs(1) - 1)
    def _():
        for hh in range(GLA_HEADS):
            t_h = st_sc[hh * GLA_DV:(hh + 1) * GLA_DV, (hh // 2) * LANES:(hh // 2 + 1) * LANES]
            sto_ref[0, hh] = t_h.T[(hh % 2) * GLA_DK:(hh % 2 + 1) * GLA_DK, :]


def _premoe_consts(rows):
    sub = min(rows, PREMOE_SUB)
    i = np.arange(sub)[:, None]
    j = np.arange(sub)[None, :]
    same = (i // CHUNK) == (j // CHUNK)
    tri = (same & (j <= i)).astype(np.float32)
    je = np.arange(sub + POOL_ROWS)[None, :]
    band = np.stack([((je <= i + POOL_ROWS) & (je > i + POOL_ROWS - w)).astype(np.float32) for w in POOL_WINDOWS])
    rp = max(rows, LANES)
    utri = (np.arange(rp)[:, None] < np.arange(rp)[None, :]).astype(np.float32)
    return jnp.asarray(tri, BF16), jnp.asarray(utri, BF16), jnp.asarray(band, BF16)


def _premoe(x2d, batch, seq, hist_len, s0, pool0, kmem, vmem, cnt0, w):
    rows = min(PREMOE_ROWS, seq)
    nl = seq // rows
    t = batch * seq
    hist = min(int(hist_len), POOL_BUF)
    tri, utri, band = _premoe_consts(rows)
    row_spec = pl.BlockSpec((rows, D_MODEL), lambda b, l: (b * nl + l, 0))
    lane_spec = pl.BlockSpec((rows, LANES), lambda b, l: (b * nl + l, 0))

    def per_batch(shape):
        return pl.BlockSpec((1,) + shape, lambda b, l: (b, 0, 0))

    consts = [w["norm_mix"], w["w_qkvg"], w["w_gate"], w["b_gate"], w["gla_gain"], w["pool_w"],
              w["pool_scale"], w["w_out"], w["norm_mem_q"], w["w_mq"], w["w_mo"], w["norm_moe"], w["w_router"],
              w["b_router"], tri, utri, band]
    state_spec = pl.BlockSpec((1, GLA_HEADS, GLA_DK, GLA_DV), lambda b, l: (b, 0, 0, 0))
    in_specs = [row_spec, state_spec, per_batch((POOL_ROWS, POOL_WIDTH)),
                per_batch((N_MEM, D_MODEL)), per_batch((N_MEM, D_MODEL)), _const_spec(cnt0.shape)]
    in_specs += [_const_spec(c.shape) for c in consts]
    out_specs = [row_spec, lane_spec, pl.BlockSpec((N_EXPERTS, LANES), lambda b, l: (0, 0)),
                 state_spec, per_batch((POOL_ROWS, POOL_WIDTH))]
    out_shape = [jax.ShapeDtypeStruct((t, D_MODEL), F32), jax.ShapeDtypeStruct((t, LANES), F32),
                 jax.ShapeDtypeStruct((N_EXPERTS, LANES), F32),
                 jax.ShapeDtypeStruct((batch, GLA_HEADS, GLA_DK, GLA_DV), F32),
                 jax.ShapeDtypeStruct((batch, POOL_ROWS, POOL_WIDTH), F32)]
    if rows >= LANES:
        out_specs.append(pl.BlockSpec((ROUTE_ROWS, rows), lambda b, l: (0, b * nl + l)))
        out_shape.append(jax.ShapeDtypeStruct((ROUTE_ROWS, t), F32))
    outs = pl.pallas_call(
        functools.partial(_premoe_body, hist, rows),
        grid=(batch, nl),
        in_specs=in_specs,
        out_specs=out_specs,
        out_shape=out_shape,
        scratch_shapes=[pltpu.VMEM((GLA_WIDTH, GLA_KW), F32), pltpu.VMEM((POOL_ROWS, POOL_WIDTH), F32),
                        pltpu.VMEM((N_EXPERTS, LANES), F32)],
        compiler_params=pltpu.CompilerParams(dimension_semantics=("arbitrary", "arbitrary"),
                                             vmem_limit_bytes=VMEM_LIMIT),
        name="premoe",
    )(x2d, s0, pool0, kmem, vmem, cnt0, *consts)
    return tuple(outs) if rows >= LANES else tuple(outs) + (None,)


def _route_tables(routes, recs_t, counts_f, tm, nb, rows):
    counts = counts_f[:, 0].astype(I32)
    padded = ((counts + tm - 1) // tm) * tm
    pend = jnp.cumsum(padded)
    pstart = pend - padded
    eids = jnp.arange(N_EXPERTS, dtype=I32)
    parts = []
    for route, rec_t in zip(routes, recs_t):
        if rec_t is None:
            rec_t = route[:, :ROUTE_RANK + TOP_K].T
        e = rec_t[ROUTE_EXPERT:ROUTE_EXPERT + TOP_K].astype(I32)
        rank = rec_t[ROUTE_RANK:ROUTE_RANK + TOP_K].astype(I32)
        dest = jnp.sum(jnp.where(e[:, :, None] == eids, pstart, 0), axis=-1) + rank
        steps = dest.shape[1] // rows
        parts.append(dest.reshape(TOP_K, steps, rows).transpose(1, 0, 2).reshape(steps, 1, TOP_K * rows))
    dest = jnp.concatenate(parts, axis=0)
    blk_start = jnp.arange(nb, dtype=I32) * tm
    block_e = jnp.minimum(jnp.sum((pend[None, :] <= blk_start[:, None]).astype(I32), axis=1), N_EXPERTS - 1)
    n_used = (pend[-1] // tm).astype(I32)
    run_end = pend[block_e] // tm
    next_e = jnp.where(run_end < n_used, block_e[jnp.minimum(run_end, nb - 1)], -1)
    return dest.astype(I32), block_e.astype(I32), next_e.astype(I32), n_used.reshape(1), pend.astype(I32), counts


def _dispatch_body(rows, tm, nb, na, nu_ref, pend_ref, cnt_ref, dest_ref, xa_ref, xb_ref, g_ref, xg_hbm,
                   hbuf, zbuf, sem, zsem):
    i = pl.program_id(0)
    n_steps = pl.num_programs(0)
    slot = lax.rem(i, 2)
    nu = nu_ref[0]
    groups = rows // SUBLANES

    def tile(ref, row):
        return ref.at[pl.ds(pl.multiple_of(row * TILE_ROWS, TILE_ROWS), TILE_ROWS)]

    def zero_copy(start):
        return pltpu.make_async_copy(zbuf, xg_hbm.at[pl.ds(pl.multiple_of(start * TILE_ROWS, TILE_ROWS), tm * TILE_ROWS)],
                                     zsem)

    def for_each_fill(fn):
        def per_expert(e, c):
            @pl.when(cnt_ref[e] > 0)
            def _():
                fn(pend_ref[e] - tm)
            return c
        lax.fori_loop(0, N_EXPERTS, per_expert, 0)

        def per_tail(b, c):
            fn(b * tm)
            return c
        lax.fori_loop(nu, nb, per_tail, 0)

    @pl.when(i == 0)
    def _():
        zbuf[...] = jnp.zeros_like(zbuf)
        for_each_fill(lambda start: zero_copy(start).start())
        for_each_fill(lambda start: zero_copy(start).wait())

    def row_wait(s):
        for _ in range(TOP_K):
            pltpu.make_async_copy(hbuf.at[s], xg_hbm.at[pl.ds(0, rows * TILE_ROWS)], sem.at[s]).wait()

    @pl.when(i >= 2)
    def _():
        row_wait(slot)

    x = jnp.where(i < na, xa_ref[...], xb_ref[...])
    _store_token_tiles(hbuf.at[slot], _rms(x, g_ref[...]))

    def send_group(gi, c):
        base = gi * SUBLANES
        for r in range(SUBLANES):
            for kk in range(TOP_K):
                dst = dest_ref[0, 0, kk * rows + base + r]
                pltpu.make_async_copy(tile(hbuf.at[slot], gi * SUBLANES + r), tile(xg_hbm, dst),
                                      sem.at[slot]).start(priority=kk % 2)
        return c
    lax.fori_loop(0, groups, send_group, 0)

    @pl.when(i == n_steps - 1)
    def _():
        row_wait(slot)

        @pl.when(i >= 1)
        def _():
            row_wait(1 - slot)


def _dispatch(x2a, x2b, dest, n_used, pend, counts, g, tm, nb):
    rows = DISPATCH_ROWS
    na, nbs = x2a.shape[0] // rows, x2b.shape[0] // rows
    steps = na + nbs
    dest3 = dest
    grid_spec = pltpu.PrefetchScalarGridSpec(
        num_scalar_prefetch=3,
        grid=(steps,),
        in_specs=[
            pl.BlockSpec((1, 1, rows * TOP_K), lambda i, *_: (i, 0, 0), memory_space=pltpu.SMEM),
            pl.BlockSpec((rows, D_MODEL), lambda i, *_: (jnp.minimum(i, na - 1), 0)),
            pl.BlockSpec((rows, D_MODEL), lambda i, *_: (jnp.maximum(i - na, 0), 0)),
            pl.BlockSpec((1, D_MODEL), lambda i, *_: (0, 0)),
        ],
        out_specs=pl.BlockSpec(memory_space=pl.ANY),
        scratch_shapes=[pltpu.VMEM((2, rows * TILE_ROWS, LANES), F32), pltpu.VMEM((tm * TILE_ROWS, LANES), F32),
                        pltpu.SemaphoreType.DMA((2,)), pltpu.SemaphoreType.DMA(())],
    )
    return pl.pallas_call(
        functools.partial(_dispatch_body, rows, tm, nb, na),
        grid_spec=grid_spec,
        out_shape=jax.ShapeDtypeStruct((nb * tm * TILE_ROWS, LANES), F32),
        compiler_params=pltpu.CompilerParams(dimension_semantics=("arbitrary",), vmem_limit_bytes=VMEM_LIMIT),
        name="dispatch",
    )(n_used, pend, counts, dest3, x2a, x2b, g)


def _moe_body(tm, nu_ref, be_ref, nxt_ref, xg_ref, wup_hbm, bup_ref, wdn_hbm, bdn_ref, ys_ref,
              wup_st, wdn_st, wup_bf, wdn_bf, wsem):
    b = pl.program_id(0)
    in_use = b < nu_ref[0]

    def fetch_up(e):
        return pltpu.make_async_copy(wup_hbm.at[e], wup_st, wsem.at[0])

    def fetch_dn(e):
        return pltpu.make_async_copy(wdn_hbm.at[e], wdn_st, wsem.at[1])

    @pl.when(b == 0)
    def _():
        fetch_up(be_ref[0]).start()
        fetch_dn(be_ref[0]).start()

    @pl.when(in_use & ((b == 0) | (be_ref[b] != be_ref[jnp.maximum(b - 1, 0)])))
    def _():
        fetch_up(be_ref[b]).wait()
        fetch_dn(be_ref[b]).wait()
        wup_bf[...] = wup_st[...].astype(BF16)
        wdn_bf[...] = wdn_st[...].astype(BF16)

        @pl.when(nxt_ref[b] >= 0)
        def _():
            fetch_up(nxt_ref[b]).start()
            fetch_dn(nxt_ref[b]).start()

    @pl.when(in_use)
    def _():
        hh = _load_token_tiles(xg_ref, tm).astype(BF16)
        uu = jnp.dot(hh, wup_bf[...], preferred_element_type=F32) + bup_ref[0]
        glu = jnp.minimum(uu[:, :D_FF], SWIGLU_LIMIT)
        lin = jnp.clip(uu[:, D_FF:], -SWIGLU_LIMIT, SWIGLU_LIMIT)
        act = glu * jax.nn.sigmoid(SWIGLU_ALPHA * glu) * (lin + 1.0)
        _store_token_tiles(ys_ref, jnp.dot(act.astype(BF16), wdn_bf[...], preferred_element_type=F32) + bdn_ref[0])

    @pl.when(b >= nu_ref[0])
    def _():
        ys_ref[...] = jnp.zeros_like(ys_ref)


def _moe(xg, block_e, next_e, n_used, w, tm, nb):
    def used(b, nu):
        return jnp.minimum(b, nu[0] - 1)

    def per_expert(shape):
        return pl.BlockSpec((1,) + shape, lambda b, nu, be, nxt: (be[used(b, nu)], 0, 0))

    grid_spec = pltpu.PrefetchScalarGridSpec(
        num_scalar_prefetch=3,
        grid=(nb,),
        in_specs=[
            pl.BlockSpec((tm * TILE_ROWS, LANES), lambda b, nu, be, nxt: (used(b, nu), 0)),
            pl.BlockSpec(memory_space=pl.ANY), per_expert((1, 2 * D_FF)),
            pl.BlockSpec(memory_space=pl.ANY), per_expert((1, D_MODEL)),
        ],
        out_specs=pl.BlockSpec((tm * TILE_ROWS, LANES), lambda b, nu, be, nxt: (b, 0)),
        scratch_shapes=[pltpu.VMEM((D_MODEL, 2 * D_FF), F32), pltpu.VMEM((D_FF, D_MODEL), F32),
                        pltpu.VMEM((D_MODEL, 2 * D_FF), BF16), pltpu.VMEM((D_FF, D_MODEL), BF16),
                        pltpu.SemaphoreType.DMA((2,))],
    )
    return pl.pallas_call(
        functools.partial(_moe_body, tm),
        grid_spec=grid_spec,
        out_shape=jax.ShapeDtypeStruct((nb * tm * TILE_ROWS, LANES), F32),
        compiler_params=pltpu.CompilerParams(dimension_semantics=("arbitrary",), vmem_limit_bytes=VMEM_LIMIT),
        name="moe",
    )(n_used, block_e, next_e, xg, w["w_up"], w["b_up"], w["w_down"], w["b_down"])


def _combine_body(rows, na, destc_ref, destn_ref, xa_ref, xb_ref, ra_ref, rb_ref, g_ref, ys_hbm, oa_ref, ob_ref,
                  gbuf, sem):
    i = pl.program_id(0)
    n_steps = pl.num_programs(0)
    slot = lax.rem(i, 2)
    groups = rows // SUBLANES

    def fetch(dest_ref, s):
        def fetch_group(gi, c):
            base = gi * SUBLANES
            for r in range(SUBLANES):
                for kk in range(TOP_K):
                    src = dest_ref[0, 0, kk * rows + base + r]
                    pltpu.make_async_copy(
                        ys_hbm.at[pl.ds(pl.multiple_of(src * TILE_ROWS, TILE_ROWS), TILE_ROWS)],
                        gbuf.at[s, kk, gi, :, r, :],
                        sem.at[s]).start(priority=kk % 2)
            return c
        lax.fori_loop(0, groups, fetch_group, 0)

    @pl.when(i == 0)
    def _():
        fetch(destc_ref, 0)

    @pl.when(i + 1 < n_steps)
    def _():
        fetch(destn_ref, 1 - slot)

    for kk in range(TOP_K):
        for r in range(SUBLANES):
            pltpu.make_async_copy(ys_hbm.at[pl.ds(0, groups * TILE_ROWS)].reshape(groups, TILE_ROWS, LANES),
                                  gbuf.at[slot, kk, :, :, r, :], sem.at[slot]).wait()

    first = i < na
    gate = jnp.where(first, ra_ref[...], rb_ref[...])
    acc = jnp.where(first, xa_ref[...], xb_ref[...])
    for kk in range(TOP_K):
        acc = acc + _plain_rows(gbuf[slot, kk]) * gate[:, ROUTE_GATE + kk:ROUTE_GATE + kk + 1]
    out = _rms(acc, g_ref[...])

    @pl.when(first)
    def _():
        oa_ref[...] = out

    @pl.when(jnp.logical_not(first))
    def _():
        ob_ref[...] = out


def _combine(x2a, x2b, ys, dest, route_a, route_b, g):
    rows = COMBINE_ROWS
    na, nbs = x2a.shape[0] // rows, x2b.shape[0] // rows
    steps = na + nbs
    dest3 = dest

    def seg_a(width):
        return pl.BlockSpec((rows, width), lambda i: (jnp.minimum(i, na - 1), 0))

    def seg_b(width):
        return pl.BlockSpec((rows, width), lambda i: (jnp.maximum(i - na, 0), 0))

    smem_cur = pl.BlockSpec((1, 1, rows * TOP_K), lambda i: (i, 0, 0), memory_space=pltpu.SMEM)
    smem_next = pl.BlockSpec((1, 1, rows * TOP_K), lambda i: (jnp.minimum(i + 1, steps - 1), 0, 0),
                             memory_space=pltpu.SMEM)
    return pl.pallas_call(
        functools.partial(_combine_body, rows, na),
        grid=(steps,),
        in_specs=[smem_cur, smem_next, seg_a(D_MODEL), seg_b(D_MODEL), seg_a(LANES), seg_b(LANES),
                  pl.BlockSpec((1, D_MODEL), lambda i: (0, 0)),
                  pl.BlockSpec(memory_space=pl.ANY)],
        out_specs=[seg_a(D_MODEL), seg_b(D_MODEL)],
        out_shape=[jax.ShapeDtypeStruct(x2a.shape, F32), jax.ShapeDtypeStruct(x2b.shape, F32)],
        scratch_shapes=[pltpu.VMEM((2, TOP_K, rows // SUBLANES, TILE_ROWS, SUBLANES, LANES), F32), pltpu.SemaphoreType.DMA((2,))],
        compiler_params=pltpu.CompilerParams(dimension_semantics=("arbitrary",), vmem_limit_bytes=VMEM_LIMIT),
        name="combine",
    )(dest3, dest3, x2a, x2b, route_a, route_b, g, ys)


def _moe_layer(x2a, x2b, route_a, route_b, rec_ta, rec_tb, counts_f, w, norm_final):
    t = x2a.shape[0] + x2b.shape[0]
    tm = MOE_ROWS
    nb = (t * TOP_K + N_EXPERTS * (tm - 1) + tm - 1) // tm
    assert DISPATCH_ROWS == COMBINE_ROWS
    dest, block_e, next_e, n_used, pend, counts = _route_tables([route_a, route_b], [rec_ta, rec_tb], counts_f, tm, nb,
                                                                 DISPATCH_ROWS)
    xg = _dispatch(x2a, x2b, dest, n_used, pend, counts, w["norm_moe"], tm, nb)
    ys = _moe(xg, block_e, next_e, n_used, w, tm, nb)
    return _combine(x2a, x2b, ys, dest, route_a, route_b, norm_final.reshape(1, -1))


def _prep_weights(norm_mix, w_in, w_gate, b_gate, gla_gain, pool_w, pool_scale, w_out, norm_mem_q, w_mq, w_mo,
                  norm_moe, w_router, b_router, w_up, b_up, w_down, b_down):
    n_qkvg = 2 * GLA_KW + 2 * GLA_WIDTH
    w_qkvg = jnp.zeros((D_MODEL, n_qkvg + LANES + POOL_WIDTH), BF16).at[:, :n_qkvg + GATE_RANK].set(
        w_in[:, :n_qkvg + GATE_RANK].astype(BF16)).at[:, n_qkvg + LANES:].set(w_in[:, n_qkvg + GATE_RANK:].astype(BF16))
    w_g = jnp.zeros((LANES, GLA_KW), BF16).at[:GATE_RANK].set(w_gate.astype(BF16))
    return {
        "norm_mix": norm_mix.reshape(1, -1),
        "w_qkvg": w_qkvg,
        "w_gate": w_g,
        "b_gate": b_gate.reshape(1, -1),
        "gla_gain": gla_gain.reshape(1, -1),
        "pool_w": pool_w.astype(BF16),
        "pool_scale": pool_scale.reshape(1, -1),
        "w_out": w_out.astype(BF16),
        "norm_mem_q": norm_mem_q.reshape(1, -1),
        "w_mq": w_mq.astype(BF16),
        "w_mo": w_mo.astype(BF16),
        "norm_moe": norm_moe.reshape(1, -1),
        "w_router": w_router.T.astype(BF16),
        "b_router": b_router.reshape(-1, 1),
        "w_up": w_up,
        "b_up": b_up.reshape(N_EXPERTS, 1, -1),
        "w_down": w_down,
        "b_down": b_down.reshape(N_EXPERTS, 1, -1),
    }


def _mix(x, s_gla, pool_prev, hist_len, mk, mv, cnt0, w):
    batch, seq, _ = x.shape
    x2d = x.reshape(batch * seq, D_MODEL)
    pool0 = jnp.concatenate([jnp.zeros((batch, 1, POOL_WIDTH), F32), pool_prev.astype(F32)], axis=1)
    x2, route, cnt, s_new, pool_t, rec_t = _premoe(x2d, batch, seq, hist_len, s_gla.astype(F32), pool0,
                                            mk.reshape(batch, N_MEM, D_MODEL), mv.reshape(batch, N_MEM, D_MODEL),
                                            cnt0, w)
    return x2, route, rec_t, cnt, s_new, pool_t[:, 1:, :]


def kernel(x_prompt, x_sample, mem_prompt, state_gla, state_pool, cache_mem_k, cache_mem_v, norm_mix, w_in, w_gate, b_gate, gla_gain, pool_w, pool_scale, w_out, norm_mem_q, norm_mem_kv, w_mq, w_mk, w_mv, w_mo, norm_moe, w_router, b_router, w_up, b_up, w_down, b_down, norm_final):
    depth = w_in.shape[0]
    assert depth == 1
    xp, xs = x_prompt, x_sample
    bp = xp.shape[0]
    gla_p, pool_p, mk_p, mv_p, gla_s, pool_s = [], [], [], [], [], []
    for l in range(depth):
        w = _prep_weights(norm_mix[l], w_in[l], w_gate[l], b_gate[l], gla_gain[l], pool_w[l], pool_scale[l], w_out[l],
                          norm_mem_q[l], w_mq[l], w_mo[l], norm_moe[l], w_router[l], b_router[l],
                          w_up[l], b_up[l], w_down[l], b_down[l])
        mk2, mv2 = _mem_kv(mem_prompt.reshape(bp * N_MEM, D_MODEL), norm_mem_kv[l].reshape(1, -1),
                           w_mk[l].astype(BF16), w_mv[l].astype(BF16))
        mk = mk2.reshape(bp, N_MEM, MEM_HEADS, MEM_HD)
        mv = mv2.reshape(bp, N_MEM, MEM_HEADS, MEM_HD)
        cnt0 = jnp.zeros((N_EXPERTS, LANES), F32)
        x2s, route_s, rec_ts, cnt_s, ss, ps = _mix(xs, state_gla[l], state_pool[l], PAST_LEN, cache_mem_k[l], cache_mem_v[l],
                                           cnt0, w)
        gla_s.append(ss)
        pool_s.append(ps)
        s0 = jnp.zeros((bp, GLA_HEADS, GLA_DK, GLA_DV), F32)
        p0 = jnp.zeros((bp, POOL_BUF, POOL_WIDTH), F32)
        x2p, route_p, rec_tp, cnt_all, sp, pp = _mix(xp, s0, p0, 0, mk, mv, cnt_s, w)
        ys, yp = _moe_layer(x2s, x2p, route_s, route_p, rec_ts, rec_tp, cnt_all, w, norm_final)
        xs, xp = ys.reshape(xs.shape), yp.reshape(xp.shape)
        gla_p.append(sp)
        pool_p.append(pp)
        mk_p.append(mk)
        mv_p.append(mv)
    return (xp, xs, jnp.stack(gla_p), jnp.stack(pool_p), jnp.stack(mk_p), jnp.stack(mv_p),
            jnp.stack(gla_s), jnp.stack(pool_s))
```
